```python
import jax, jax.numpy as jnp
from jax import lax
import numpy as np

D_MODEL = 1024
BATCH = 1
SEQ = 16384
DEPTH = 2
DEC_BATCH = 16
DEC_SEQ = 64
PAST_LEN = 1024

CHUNK = 64
N_HEADS_A = 8
N_KV_A = 2
HD_A = 64
WINDOW = 128
WIN_CHUNKS = WINDOW // CHUNK
ROPE_DIM = HD_A // 4
ROPE_THETA = 500000.0
N_QK_B = 4
N_V_B = 8
DK_B = 128
DV_B = 128
CONV_W = 4
N_GROUPS = 4
EXPERTS_PER_GROUP = 4
N_EXPERTS = N_GROUPS * EXPERTS_PER_GROUP
TOP_K_IN_GROUP = 2
D_EXPERT = 256
EPS = 1e-6

Q_A = N_HEADS_A * HD_A
KVW_A = N_KV_A * HD_A
QK_B = N_QK_B * DK_B
V_B = N_V_B * DV_B
CONV_DIM_B = 2 * QK_B + V_B
IN_SIZES = (Q_A, KVW_A, KVW_A, CONV_DIM_B, V_B, N_V_B, N_V_B, D_MODEL, D_MODEL)
IN_DIM = sum(IN_SIZES)

kernel_name = "hybrid_swa_gdn_hmoe_stream_step"


def rmsnorm(x, w):
    xf = x.astype(jnp.float32)
    y = xf * lax.rsqrt(jnp.mean(xf * xf, axis=-1, keepdims=True) + EPS)
    return (y * w.astype(jnp.float32)).astype(x.dtype)


def l2norm(x):
    return x * lax.rsqrt(jnp.sum(x * x, axis=-1, keepdims=True) + EPS)


def partial_rope(x, pos):
    half = ROPE_DIM // 2
    inv = ROPE_THETA ** (-jnp.arange(half, dtype=jnp.float32) / half)
    ang = pos.astype(jnp.float32)[:, None] * inv[None, :]
    cos = jnp.cos(ang)[None, :, None, :]
    sin = jnp.sin(ang)[None, :, None, :]
    xr = x[..., :ROPE_DIM].astype(jnp.float32)
    x1, x2 = xr[..., :half], xr[..., half:]
    rot = jnp.concatenate([x1 * cos - x2 * sin, x2 * cos + x1 * sin], axis=-1)
    return jnp.concatenate([rot.astype(x.dtype), x[..., ROPE_DIM:]], axis=-1)


def sink_attention(q, k, v, mask, sinks):
    G = q.shape[4]
    s = jnp.einsum('bnqkgd,bnskd->bnkgqs', q.astype(jnp.float32), k.astype(jnp.float32)) * (HD_A ** -0.5)
    s = jnp.where(mask[None, :, None, None], s, -jnp.inf)
    sink = jnp.broadcast_to(sinks.astype(jnp.float32).reshape(1, 1, N_KV_A, G, 1, 1), s.shape[:-1] + (1,))
    p = jax.nn.softmax(jnp.concatenate([s, sink], axis=-1), axis=-1)[..., :-1]
    return jnp.einsum('bnkgqs,bnskd->bnqkgd', p.astype(v.dtype), v)


def window_attention_prompt(q, k, v, sinks):
    B, T, H, _ = q.shape
    N = T // CHUNK
    G = H // N_KV_A
    qb = q.reshape(B, N, CHUNK, N_KV_A, G, HD_A)

    def band(a):
        a = a.reshape(B, N, CHUNK, N_KV_A, HD_A)
        a = jnp.pad(a, ((0, 0), (WIN_CHUNKS, 0), (0, 0), (0, 0), (0, 0)))
        return jnp.concatenate([a[:, i:i + N] for i in range(WIN_CHUNKS + 1)], axis=2)

    src = jnp.arange(N)[:, None] + jnp.arange(WIN_CHUNKS + 1)[None, :] - WIN_CHUNKS
    mask = jnp.repeat(src >= 0, CHUNK, axis=1)[:, None, :]
    o = sink_attention(qb, band(k), band(v), mask, sinks)
    return o.reshape(B, T, H * HD_A)


def window_attention_step(q, k_new, v_new, k_cache, v_cache, sinks):
    B, T, H, _ = q.shape
    G = H // N_KV_A
    kk = jnp.concatenate([k_cache.astype(k_new.dtype), k_new], axis=1)[:, None]
    vv = jnp.concatenate([v_cache.astype(v_new.dtype), v_new], axis=1)[:, None]
    mask = jnp.ones((1, 1, kk.shape[2]), dtype=bool)
    o = sink_attention(q.reshape(B, 1, T, N_KV_A, G, HD_A), kk, vv, mask, sinks)
    return o.reshape(B, T, H * HD_A)


def causal_conv(xpad, w):
    C = xpad.shape[-1]
    return lax.conv_general_dilated(xpad, w[:, None, :], window_strides=(1,), padding='VALID',
                                    dimension_numbers=('NWC', 'WIO', 'NWC'), feature_group_count=C)


def gated_delta_rule(q, k, v, g, beta, s0):
    B, T, H, DK = k.shape
    DV = v.shape[-1]
    C = min(CHUNK, T)
    N = T // C

    def blocks(a):
        return jnp.moveaxis(a.reshape((B, N, C, H) + a.shape[3:]), (1, 3), (0, 2))

    qc, kc, vc, bc = blocks(q), blocks(k), blocks(v), blocks(beta)
    gc = jnp.cumsum(blocks(g), axis=-1)
    causal = jnp.tril(jnp.ones((C, C), dtype=bool))
    strict = jnp.tril(jnp.ones((C, C), dtype=bool), -1)
    decay = jnp.exp(jnp.where(causal, gc[..., :, None] - gc[..., None, :], -jnp.inf))
    kb = kc * bc[..., None]
    a_kk = jnp.where(strict, jnp.einsum('nbhid,nbhjd->nbhij', kb, kc) * decay, 0.0)
    lmat = a_kk + jnp.eye(C, dtype=jnp.float32)
    rhs = jnp.concatenate([vc * bc[..., None], kb * jnp.exp(gc)[..., None]], axis=-1)
    sol = lax.linalg.triangular_solve(lmat, rhs, left_side=True, lower=True, unit_diagonal=True)
    u, w = sol[..., :DV], sol[..., DV:]
    a_qk = jnp.einsum('nbhid,nbhjd->nbhij', qc, kc) * decay

    def step(s, xs):
        qi, ki, ui, wi, gi, ai = xs
        v_new = ui - jnp.einsum('bhck,bhkv->bhcv', wi, s)
        o = (jnp.einsum('bhck,bhkv->bhcv', qi * jnp.exp(gi)[..., None], s)
             + jnp.einsum('bhij,bhjv->bhiv', ai, v_new))
        g_last = gi[..., -1:]
        s = (s * jnp.exp(g_last)[..., None]
             + jnp.einsum('bhck,bhcv->bhkv', ki * jnp.exp(g_last - gi)[..., None], v_new))
        return s, o

    s, o = lax.scan(step, s0, (qc, kc, u, w, gc, a_qk))
    o = jnp.moveaxis(o, (0, 2), (1, 3)).reshape(B, T, H, DV)
    return o, s


def token_mixers(h, pos0, lp, st):
    B, T, _ = h.shape
    proj = h @ lp['w_in']
    qa, ka, va, xb, zb, bb, ab, gate_a, gate_b = jnp.split(proj, np.cumsum(IN_SIZES)[:-1], axis=-1)
    pos = pos0 + jnp.arange(T)
    qa = partial_rope(qa.reshape(B, T, N_HEADS_A, HD_A), pos)
    ka = partial_rope(ka.reshape(B, T, N_KV_A, HD_A), pos)
    va = va.reshape(B, T, N_KV_A, HD_A)
    if st is None:
        oa = window_attention_prompt(qa, ka, va, lp['sinks'])
        keep = min(WINDOW, T)
        new_k, new_v = ka[:, T - keep:], va[:, T - keep:]
        conv_prev = jnp.zeros((B, CONV_W - 1, CONV_DIM_B), h.dtype)
        s0 = jnp.zeros((B, N_V_B, DK_B, DV_B), jnp.float32)
    else:
        oa = window_attention_step(qa, ka, va, st[0], st[1], lp['sinks'])
        new_k, new_v = ka, va
        conv_prev = st[2].astype(h.dtype)
        s0 = st[3].astype(jnp.float32)
    ya = oa @ lp['w_o_a']
    xpad = jnp.concatenate([conv_prev, xb], axis=1)
    new_conv = xpad[:, -(CONV_W - 1):]
    qkv = jax.nn.silu(causal_conv(xpad, lp['conv']))
    qb, kb, vb = jnp.split(qkv, [QK_B, 2 * QK_B], axis=-1)
    rep = N_V_B // N_QK_B
    qb = l2norm(jnp.repeat(qb.reshape(B, T, N_QK_B, DK_B), rep, axis=2).astype(jnp.float32)) * (DK_B ** -0.5)
    kb = l2norm(jnp.repeat(kb.reshape(B, T, N_QK_B, DK_B), rep, axis=2).astype(jnp.float32))
    vb = vb.reshape(B, T, N_V_B, DV_B).astype(jnp.float32)
    beta = jax.nn.sigmoid(bb.astype(jnp.float32))
    g = -jnp.exp(lp['a_log'].astype(jnp.float32)) * jax.nn.softplus(ab.astype(jnp.float32) + lp['dt_bias'].astype(jnp.float32))
    ob, s_new = gated_delta_rule(qb, kb, vb, g, beta, s0)
    ob = rmsnorm(ob, lp['norm_o']) * jax.nn.silu(zb.reshape(B, T, N_V_B, DV_B).astype(jnp.float32))
    yb = ob.astype(h.dtype).reshape(B, T, V_B) @ lp['w_o_b']
    merged = jax.nn.sigmoid(gate_a) * ya + jax.nn.sigmoid(gate_b) * yb
    return merged @ lp['w_out'], (new_k, new_v, new_conv, s_new.astype(h.dtype))


def hier_moe(h, lp):
    B, T, D = h.shape
    xt = h.reshape(B * T, D)
    lg = (xt @ lp['router_g'] + lp['router_g_b']).astype(jnp.float32)
    _, gi = lax.top_k(lg, 1)
    onehot_g = jax.nn.one_hot(gi[:, 0], N_GROUPS, dtype=jnp.float32)
    p_group = jnp.sum(jax.nn.softmax(lg, axis=-1) * onehot_g, axis=-1, keepdims=True)
    le = (jnp.einsum('nd,gde->nge', xt, lp['router_e']) + lp['router_e_b']).astype(jnp.float32)
    le_sel = jnp.einsum('nge,ng->ne', le, onehot_g)
    tv, ti = lax.top_k(le_sel, TOP_K_IN_GROUP)
    wk = jax.nn.softmax(tv, axis=-1) * p_group
    w_local = jnp.einsum('nk,nke->ne', wk, jax.nn.one_hot(ti, EXPERTS_PER_GROUP, dtype=jnp.float32))
    w_exp = (onehot_g[:, :, None] * w_local[:, None, :]).reshape(-1, N_EXPERTS).astype(h.dtype)
    hg = jnp.einsum('nd,edf->nef', xt, lp['w_gate'])
    hu = jnp.einsum('nd,edf->nef', xt, lp['w_up'])
    y = jnp.einsum('nef,efd->nd', jax.nn.silu(hg) * hu * w_exp[:, :, None], lp['w_down'])
    return y.reshape(B, T, D)


def trunk_layer(x, c, pos0, lp, st):
    mod = (jax.nn.silu(c) @ lp['w_mod'] + lp['b_mod'])[:, None, :]
    sh1, sc1, g1, sh2, sc2, g2 = jnp.split(mod, 6, axis=-1)
    h = rmsnorm(x, lp['norm_mix']) * (1 + sc1) + sh1
    mix, new_st = token_mixers(h, pos0, lp, st)
    x = x + g1 * mix
    h = rmsnorm(x, lp['norm_ffn']) * (1 + sc2) + sh2
    x = x + g2 * hier_moe(h, lp)
    return x, new_st


def setup_inputs(seed: int = 0) -> dict:
    key = jax.random.key(seed)
    ks = jax.random.split(key, 29)
    f32 = jnp.float32
    nrm = lambda k, shape, s: jax.random.normal(k, shape, f32) * s
    a_rows = min(WINDOW, PAST_LEN)
    D = D_MODEL
    return {
        'x_prompt': nrm(ks[0], (BATCH, SEQ, D), 1.0),
        'x_sample': nrm(ks[1], (DEC_BATCH, DEC_SEQ, D), 1.0),
        'cache_k_a': nrm(ks[2], (DEPTH, DEC_BATCH, a_rows, N_KV_A, HD_A), 1.0),
        'cache_v_a': nrm(ks[3], (DEPTH, DEC_BATCH, a_rows, N_KV_A, HD_A), 1.0),
        'state_conv_b': nrm(ks[4], (DEPTH, DEC_BATCH, CONV_W - 1, CONV_DIM_B), 1.0),
        'state_ssm_b': nrm(ks[5], (DEPTH, DEC_BATCH, N_V_B, DK_B, DV_B), 0.1),
        'c_prompt': nrm(ks[6], (BATCH, D), 1.0),
        'c_sample': nrm(ks[7], (DEC_BATCH, D), 1.0),
        'w_mod': nrm(ks[8], (DEPTH, D, 6 * D), 0.5 * D ** -0.5),
        'b_mod': nrm(ks[9], (DEPTH, 6 * D), 0.01),
        'norm_mix': 1.0 + nrm(ks[10], (DEPTH, D), 0.1),
        'w_in': nrm(ks[11], (DEPTH, D, IN_DIM), D ** -0.5),
        'sinks_a': nrm(ks[12], (DEPTH, N_HEADS_A), 1.0),
        'w_o_a': nrm(ks[13], (DEPTH, Q_A, D), Q_A ** -0.5),
        'conv_b': nrm(ks[14], (DEPTH, CONV_W, CONV_DIM_B), CONV_W ** -0.5),
        'a_log_b': jnp.log(jax.random.uniform(ks[15], (DEPTH, N_V_B), f32, 1.0, 16.0)),
        'dt_bias_b': nrm(ks[16], (DEPTH, N_V_B), 0.1),
        'norm_o_b': 1.0 + nrm(ks[17], (DEPTH, DV_B), 0.1),
        'w_o_b': nrm(ks[18], (DEPTH, V_B, D), V_B ** -0.5),
        'w_out': nrm(ks[19], (DEPTH, D, D), D ** -0.5),
        'norm_ffn': 1.0 + nrm(ks[20], (DEPTH, D), 0.1),
        'router_g': nrm(ks[21], (DEPTH, D, N_GROUPS), D ** -0.5),
        'router_g_b': nrm(ks[22], (DEPTH, N_GROUPS), 0.01),
        'router_e': nrm(ks[23], (DEPTH, N_GROUPS, D, EXPERTS_PER_GROUP), D ** -0.5),
        'router_e_b': nrm(ks[24], (DEPTH, N_GROUPS, EXPERTS_PER_GROUP), 0.01),
        'w_gate_e': nrm(ks[25], (DEPTH, N_EXPERTS, D, D_EXPERT), D ** -0.5),
        'w_up_e': nrm(ks[26], (DEPTH, N_EXPERTS, D, D_EXPERT), D ** -0.5),
        'w_down_e': nrm(ks[27], (DEPTH, N_EXPERTS, D_EXPERT, D), D_EXPERT ** -0.5),
        'norm_final': 1.0 + nrm(ks[28], (D,), 0.1),
    }


def reference(x_prompt, x_sample, cache_k_a, cache_v_a, state_conv_b, state_ssm_b, c_prompt, c_sample,
              w_mod, b_mod, norm_mix, w_in, sinks_a, w_o_a, conv_b, a_log_b, dt_bias_b, norm_o_b, w_o_b,
              w_out, norm_ffn, router_g, router_g_b, router_e, router_e_b, w_gate_e, w_up_e, w_down_e,
              norm_final):
    xp, xs = x_prompt, x_sample
    st_p, st_s = [], []
    for l in range(DEPTH):
        lp = dict(w_mod=w_mod[l], b_mod=b_mod[l], norm_mix=norm_mix[l], w_in=w_in[l], sinks=sinks_a[l],
                  w_o_a=w_o_a[l], conv=conv_b[l], a_log=a_log_b[l], dt_bias=dt_bias_b[l], norm_o=norm_o_b[l],
                  w_o_b=w_o_b[l], w_out=w_out[l], norm_ffn=norm_ffn[l], router_g=router_g[l],
                  router_g_b=router_g_b[l], router_e=router_e[l], router_e_b=router_e_b[l],
                  w_gate=w_gate_e[l], w_up=w_up_e[l], w_down=w_down_e[l])
        xp, sp = trunk_layer(xp, c_prompt, 0, lp, None)
        xs, ss = trunk_layer(xs, c_sample, PAST_LEN, lp,
                             (cache_k_a[l], cache_v_a[l], state_conv_b[l], state_ssm_b[l]))
        st_p.append(sp)
        st_s.append(ss)
    y_prompt = rmsnorm(xp, norm_final)
    y_sample = rmsnorm(xs, norm_final)
    k_a_prompt = jnp.stack([s[0] for s in st_p])
    v_a_prompt = jnp.stack([s[1] for s in st_p])
    conv_prompt = jnp.stack([s[2] for s in st_p])
    ssm_prompt = jnp.stack([s[3] for s in st_p])
    k_a_sample = jnp.stack([s[0] for s in st_s])
    v_a_sample = jnp.stack([s[1] for s in st_s])
    conv_sample = jnp.stack([s[2] for s in st_s])
    ssm_sample = jnp.stack([s[3] for s in st_s])
    return (y_prompt, y_sample, k_a_prompt, v_a_prompt, conv_prompt, ssm_prompt, k_a_sample, v_a_sample, conv_sample, ssm_sample)
```

```python
import functools

import numpy as np
import jax
import jax.numpy as jnp
from jax import lax
from jax.experimental import pallas as pl
from jax.experimental.pallas import tpu as pltpu

CHUNK = 64
PAST_LEN = 1024
N_HEADS_A = 8
N_KV_A = 2
HD_A = 64
WIN_CHUNKS = 2
ROPE_DIM = 16
ROPE_THETA = 500000.0
N_QK_B = 4
N_V_B = 8
DK_B = 128
DV_B = 128
CONV_W = 4
N_GROUPS = 4
EXPERTS_PER_GROUP = 4
N_EXPERTS = 16
D_EXPERT = 256
EPS = 1e-6
LANES = 128
VMEM_LIMIT = 56 * 1024 * 1024

F32 = jnp.float32
BF16 = jnp.bfloat16
HIGHEST = lax.Precision.HIGHEST


def _pick_tile(total, pref):
    t = pref
    while total % t:
        t //= 2
    assert t >= CHUNK
    return t


def _const_spec(shape):
    nd = len(shape)
    return pl.BlockSpec(shape, lambda *_: (0,) * nd)


def _silu(x):
    return x * (1.0 / (1.0 + jnp.exp(-x)))


def _sigmoid(x):
    return 1.0 / (1.0 + jnp.exp(-x))


def _dot(a, b):
    return jnp.dot(a, b, preferred_element_type=F32)


def _dot_nt(a, b):
    return lax.dot_general(a, b, (((1,), (1,)), ((), ())), preferred_element_type=F32)


def _dot_tn(a, b):
    return lax.dot_general(a, b, (((0,), (0,)), ((), ())), preferred_element_type=F32)


def _dot_hi(a, b):
    return jnp.dot(a, b, preferred_element_type=F32, precision=HIGHEST)


def _mod_kernel(c_ref, w_ref, b_ref, o_ref):
    a = _silu(c_ref[...]).astype(BF16)
    o_ref[...] = _dot(a, w_ref[...].astype(BF16)) + b_ref[...]


def _mod_call(c_all, w_mod, b_mod):
    depth, d, d6 = w_mod.shape
    rows = c_all.shape[0]
    tn = 1024
    return pl.pallas_call(
        _mod_kernel,
        out_shape=jax.ShapeDtypeStruct((depth, rows, d6), F32),
        grid=(depth, d6 // tn),
        in_specs=[
            pl.BlockSpec((rows, d), lambda l, j: (0, 0)),
            pl.BlockSpec((None, d, tn), lambda l, j: (l, 0, j)),
            pl.BlockSpec((None, 1, tn), lambda l, j: (l, 0, j)),
        ],
        out_specs=pl.BlockSpec((None, rows, tn), lambda l, j: (l, 0, j)),
        compiler_params=pltpu.CompilerParams(vmem_limit_bytes=VMEM_LIMIT),
        name="mod",
    )(c_all, w_mod, b_mod.reshape(depth, 1, d6))


def _rope(x, cs):
    return (x * cs[:, :LANES]
            + pltpu.roll(x, 8, axis=1) * cs[:, LANES:2 * LANES]
            + pltpu.roll(x, LANES - 8, axis=1) * cs[:, 2 * LANES:])


def _inproj_kernel(x_ref, mod_ref, nw_ref, cs_ref, wqkv_ref, wxb_ref, wz_ref, wba_ref, wga_ref, wgb_ref,
                   q_ref, k_ref, v_ref, xb_ref, z_ref, ba_ref, ga_ref, gb_ref):
    x = x_ref[...]
    tm, d = x.shape
    nch = tm // CHUNK
    xn = x * lax.rsqrt(jnp.mean(x * x, axis=-1, keepdims=True) + EPS) * nw_ref[...]
    m = mod_ref[...]
    h = xn.reshape(nch, CHUNK, d) * (1.0 + m[:, :, d:]) + m[:, :, :d]
    h = h.reshape(tm, d).astype(BF16)
    cs = cs_ref[...]
    qkv = _dot(h, wqkv_ref[...])
    nq = N_HEADS_A * HD_A
    for g in range(nq // LANES):
        q_ref[:, g * LANES:(g + 1) * LANES] = _rope(qkv[:, g * LANES:(g + 1) * LANES], cs).astype(BF16)
    k_ref[...] = _rope(qkv[:, nq:nq + LANES], cs)
    v_ref[...] = qkv[:, nq + LANES:]
    xb_ref[...] = _dot(h, wxb_ref[...])
    z_ref[...] = _dot(h, wz_ref[...])
    ba_ref[...] = _dot(h, wba_ref[...])
    ga_ref[...] = _dot(h, wga_ref[...])
    gb_ref[...] = _dot(h, wgb_ref[...])


def _inproj_call(x, modc, norm_w, cs, wqkv, wxb, wz, wba, wga, wgb, tm):
    t, d = x.shape
    nch = tm // CHUNK
    widths = [N_HEADS_A * HD_A, LANES, LANES, wxb.shape[1], wz.shape[1], LANES, d, d]
    dtypes = [BF16, F32, F32, F32, F32, F32, F32, F32]
    tok = lambda w: pl.BlockSpec((tm, w), lambda i: (i, 0))
    return pl.pallas_call(
        _inproj_kernel,
        out_shape=[jax.ShapeDtypeStruct((t, w), dt) for w, dt in zip(widths, dtypes)],
        grid=(t // tm,),
        in_specs=[
            tok(d),
            pl.BlockSpec((nch, 1, 2 * d), lambda i: (i, 0, 0)),
            _const_spec((1, d)),
            tok(3 * LANES),
            _const_spec(wqkv.shape), _const_spec(wxb.shape), _const_spec(wz.shape),
            _const_spec(wba.shape), _const_spec(wga.shape), _const_spec(wgb.shape),
        ],
        out_specs=[tok(w) for w in widths],
        compiler_params=pltpu.CompilerParams(vmem_limit_bytes=VMEM_LIMIT),
        name="inproj",
    )(x, modc, norm_w, cs, wqkv, wxb, wz, wba, wga, wgb)


def _attn_kernel(own_ref, nvalid_ref, sink_ref, q_ref, k0_ref, k1_ref, k2_ref, v0_ref, v1_ref, v2_ref, o_ref):
    i = pl.program_id(0)
    nvalid = nvalid_ref[i]
    kcat = jnp.concatenate([k0_ref[...], k1_ref[...], k2_ref[...]], axis=0).astype(BF16)
    vcat = jnp.concatenate([v0_ref[...], v1_ref[...], v2_ref[...]], axis=0).astype(BF16)
    nk = (WIN_CHUNKS + 1) * CHUNK
    key_chunk = lax.broadcasted_iota(jnp.int32, (CHUNK, nk), 1) // CHUNK
    valid = key_chunk >= (WIN_CHUNKS - nvalid)
    q = q_ref[...]
    group = N_HEADS_A // N_KV_A
    for h in range(N_HEADS_A):
        kv = h // group
        qh = q[:, h * HD_A:(h + 1) * HD_A]
        kh = kcat[:, kv * HD_A:(kv + 1) * HD_A]
        vh = vcat[:, kv * HD_A:(kv + 1) * HD_A]
        s = _dot_nt(qh, kh) * (HD_A ** -0.5)
        s = jnp.where(valid, s, -jnp.inf)
        sink = sink_ref[h]
        m = jnp.maximum(jnp.max(s, axis=-1, keepdims=True), sink)
        p = jnp.exp(s - m)
        den = jnp.sum(p, axis=-1, keepdims=True) + jnp.exp(sink - m)
        o = _dot(p.astype(BF16), vh) / den
        o_ref[:, h * HD_A:(h + 1) * HD_A] = o.astype(BF16)


def _attn_call(own, nvalid, sinks, q, k_ext, v_ext):
    t = q.shape[0]
    n = t // CHUNK
    kvw = N_KV_A * HD_A
    ext = lambda off: pl.BlockSpec((CHUNK, kvw), lambda i, own, nv: (own[i] - off, 0))
    return pl.pallas_call(
        _attn_kernel,
        out_shape=jax.ShapeDtypeStruct((t, N_HEADS_A * HD_A), BF16),
        grid_spec=pltpu.PrefetchScalarGridSpec(
            num_scalar_prefetch=2,
            grid=(n,),
            in_specs=[
                pl.BlockSpec(memory_space=pltpu.SMEM),
                pl.BlockSpec((CHUNK, N_HEADS_A * HD_A), lambda i, own, nv: (i, 0)),
                ext(2), ext(1), ext(0), ext(2), ext(1), ext(0),
            ],
            out_specs=pl.BlockSpec((CHUNK, N_HEADS_A * HD_A), lambda i, own, nv: (i, 0)),
        ),
        name="attn",
    )(own, nvalid, sinks, q, k_ext, k_ext, k_ext, v_ext, v_ext, v_ext)


def _l2norm(x):
    return x * lax.rsqrt(jnp.sum(x * x, axis=-1, keepdims=True) + EPS)


def _gdn_kernel(seq_ref, first_ref, last_ref,
                xb_ref, cinit_ref, sinit_ref, cw_ref, ba_ref, apar_ref, z_ref, nw_ref,
                ob_ref, sout_ref, pad_ref, state_ref):
    i = pl.program_id(0)
    c = CHUNK

    @pl.when(first_ref[i] == 1)
    def _():
        pad_ref[0:8, :] = cinit_ref[...]
        state_ref[...] = sinit_ref[...]

    pad_ref[8:8 + c, :] = xb_ref[...]
    cw = cw_ref[...]
    conv = pad_ref[5:5 + c, :] * cw[0:1, :]
    for j in range(1, CONV_W):
        conv = conv + pad_ref[5 + j:5 + j + c, :] * cw[j:j + 1, :]
    pad_ref[0:8, :] = pad_ref[c:c + 8, :]
    qkv = _silu(conv)
    nqk = N_QK_B * DK_B

    ba = ba_ref[...]
    apar = apar_ref[...]
    beta = _sigmoid(ba[:, 0:N_V_B])
    sp_in = ba + apar[1:2, :]
    softplus = jnp.maximum(sp_in, 0.0) + jnp.log(1.0 + jnp.exp(-jnp.abs(sp_in)))
    g_all = -jnp.exp(apar[0:1, :]) * softplus
    row = lax.broadcasted_iota(jnp.int32, (c, c), 0)
    col = lax.broadcasted_iota(jnp.int32, (c, c), 1)
    lower = (row >= col).astype(F32)
    gc_all = _dot_hi(lower, g_all)
    gc_t = gc_all.T
    eye = (row == col).astype(F32)

    qn, kn, kk, qk = [], [], [], []
    for j in range(N_QK_B):
        qj = _l2norm(qkv[:, j * DK_B:(j + 1) * DK_B]) * (DK_B ** -0.5)
        kj = _l2norm(qkv[:, nqk + j * DK_B:nqk + (j + 1) * DK_B])
        qn.append(qj)
        kn.append(kj)
        kjb = kj.astype(BF16)
        kk.append(_dot_nt(kjb, kjb))
        qk.append(_dot_nt(qj.astype(BF16), kjb))

    rep = N_V_B // N_QK_B
    nw = nw_ref[...]
    for h in range(N_V_B):
        j = h // rep
        gcol = gc_all[:, N_V_B + h:N_V_B + h + 1]
        grow = gc_t[N_V_B + h:N_V_B + h + 1, :]
        bcol = beta[:, h:h + 1]
        decay = jnp.exp(jnp.where(row >= col, gcol - grow, -jnp.inf))
        a = jnp.where(row > col, bcol * kk[j] * decay, 0.0)
        x = eye - a
        pw = _dot_hi(a, a)
        for lvl in range(5):
            x = x + _dot_hi(x, pw)
            if lvl < 4:
                pw = _dot_hi(pw, pw)
        egc = jnp.exp(gcol)
        vh = qkv[:, 2 * nqk + h * DV_B:2 * nqk + (h + 1) * DV_B]
        rhs = jnp.concatenate([vh * bcol, kn[j] * (bcol * egc)], axis=1).astype(BF16)
        sol = _dot(x.astype(BF16), rhs)
        u, w = sol[:, :DV_B], sol[:, DV_B:]
        s = state_ref[h]
        sb = s.astype(BF16)
        v_new = u - _dot(w.astype(BF16), sb)
        v_new_b = v_new.astype(BF16)
        o = _dot((qn[j] * egc).astype(BF16), sb) + _dot((qk[j] * decay).astype(BF16), v_new_b)
        g_last = gcol[c - 1:c, :]
        k_dec = (kn[j] * jnp.exp(g_last - gcol)).astype(BF16)
        state_ref[h] = s * jnp.exp(g_last) + _dot_tn(k_dec, v_new_b)
        on = o * lax.rsqrt(jnp.mean(o * o, axis=-1, keepdims=True) + EPS) * nw
        zh = z_ref[:, h * DV_B:(h + 1) * DV_B]
        ob_ref[:, h * DV_B:(h + 1) * DV_B] = (on * _silu(zh)).astype(BF16)

    @pl.when(last_ref[i] == 1)
    def _():
        sout_ref[...] = state_ref[...]


def _gdn_call(seq, first, last, xb, conv_init, ssm_init, conv_w, ba, apar, z, norm_o):
    t, cd = xb.shape
    n = t // CHUNK
    nseq = conv_init.shape[0]
    vw = N_V_B * DV_B
    chunk = lambda w: pl.BlockSpec((CHUNK, w), lambda i, s, f, l: (i, 0))
    return pl.pallas_call(
        _gdn_kernel,
        out_shape=[jax.ShapeDtypeStruct((t, vw), BF16),
                   jax.ShapeDtypeStruct((nseq, N_V_B, DK_B, DV_B), F32)],
        grid_spec=pltpu.PrefetchScalarGridSpec(
            num_scalar_prefetch=3,
            grid=(n,),
            in_specs=[
                chunk(cd),
                pl.BlockSpec((None, 8, cd), lambda i, s, f, l: (s[i], 0, 0)),
                pl.BlockSpec((None, N_V_B, DK_B, DV_B), lambda i, s, f, l: (s[i], 0, 0, 0)),
                pl.BlockSpec((CONV_W, cd), lambda i, s, f, l: (0, 0)),
                chunk(LANES),
                pl.BlockSpec((2, LANES), lambda i, s, f, l: (0, 0)),
                chunk(vw),
                pl.BlockSpec((1, DV_B), lambda i, s, f, l: (0, 0)),
            ],
            out_specs=[chunk(vw),
                       pl.BlockSpec((None, N_V_B, DK_B, DV_B), lambda i, s, f, l: (s[i], 0, 0, 0))],
            scratch_shapes=[pltpu.VMEM((CHUNK + 8, cd), F32),
                            pltpu.VMEM((N_V_B, DK_B, DV_B), F32)],
        ),
        compiler_params=pltpu.CompilerParams(dimension_semantics=("arbitrary",),
                                             vmem_limit_bytes=VMEM_LIMIT),
        name="gdn",
    )(seq, first, last, xb, conv_init, ssm_init, conv_w, ba, apar, z, norm_o)


def _outproj_kernel(x_ref, oa_ref, ob_ref, ga_ref, gb_ref, mod_ref, woa_ref, wob_ref, wout_ref, nw_ref,
                    wr_ref, br_ref, x1_ref, h2_ref, wexp_ref):
    x = x_ref[...]
    tm, d = x.shape
    nch = tm // CHUNK
    ya = _dot(oa_ref[...], woa_ref[...])
    yb = _dot(ob_ref[...], wob_ref[...])
    merged = _sigmoid(ga_ref[...]) * ya + _sigmoid(gb_ref[...]) * yb
    mix = _dot(merged.astype(BF16), wout_ref[...])
    m = mod_ref[...]
    x1 = (x.reshape(nch, CHUNK, d) + m[:, :, 2 * d:3 * d] * mix.reshape(nch, CHUNK, d))
    x1_ref[...] = x1.reshape(tm, d)
    xf = x1.reshape(tm, d)
    xn = xf * lax.rsqrt(jnp.mean(xf * xf, axis=-1, keepdims=True) + EPS) * nw_ref[...]
    h2 = (xn.reshape(nch, CHUNK, d) * (1.0 + m[:, :, 4 * d:5 * d]) + m[:, :, 3 * d:4 * d]).reshape(tm, d)
    h2_ref[...] = h2.astype(BF16)

    logits = _dot_hi(h2, wr_ref[...]) + br_ref[...]
    lane = lax.broadcasted_iota(jnp.int32, logits.shape, 1)
    neg = -jnp.inf
    lg = jnp.where(lane < N_GROUPS, logits, neg)
    mx = jnp.max(lg, axis=-1, keepdims=True)
    gi = jnp.min(jnp.where(lg == mx, lane, LANES), axis=-1, keepdims=True)
    p_group = 1.0 / jnp.sum(jnp.exp(lg - mx), axis=-1, keepdims=True)
    lo = N_GROUPS + EXPERTS_PER_GROUP * gi
    le = jnp.where((lane >= lo) & (lane < lo + EXPERTS_PER_GROUP), logits, neg)
    v1 = jnp.max(le, axis=-1, keepdims=True)
    i1 = jnp.min(jnp.where(le == v1, lane, LANES), axis=-1, keepdims=True)
    le2 = jnp.where(lane == i1, neg, le)
    v2 = jnp.max(le2, axis=-1, keepdims=True)
    i2 = jnp.min(jnp.where(le2 == v2, lane, LANES), axis=-1, keepdims=True)
    e2 = jnp.exp(v2 - v1)
    w1 = p_group / (1.0 + e2)
    w2 = p_group * e2 / (1.0 + e2)
    wexp_ref[...] = jnp.where(lane == i1, w1, 0.0) + jnp.where(lane == i2, w2, 0.0)


def _outproj_call(x, oa, ob, ga, gb, modc, woa, wob, wout, norm_w, wr, br, tm):
    t, d = x.shape
    nch = tm // CHUNK
    tok = lambda w: pl.BlockSpec((tm, w), lambda i: (i, 0))
    return pl.pallas_call(
        _outproj_kernel,
        out_shape=[jax.ShapeDtypeStruct((t, d), F32), jax.ShapeDtypeStruct((t, d), BF16),
                   jax.ShapeDtypeStruct((t, LANES), F32)],
        grid=(t // tm,),
        in_specs=[
            tok(d), tok(oa.shape[1]), tok(ob.shape[1]), tok(d), tok(d),
            pl.BlockSpec((nch, 1, 6 * d), lambda i: (i, 0, 0)),
            _const_spec(woa.shape), _const_spec(wob.shape), _const_spec(wout.shape),
            _const_spec((1, d)), _const_spec(wr.shape), _const_spec((1, LANES)),
        ],
        out_specs=[tok(d), tok(d), tok(LANES)],
        compiler_params=pltpu.CompilerParams(vmem_limit_bytes=VMEM_LIMIT),
        name="outproj",
    )(x, oa, ob, ga, gb, modc, woa, wob, wout, norm_w, wr, br)


def _moe_kernel(h_ref, wexp_ref, x1_ref, mod_ref, wg_ref, wu_ref, wd_ref, o_ref, acc_ref):
    e = pl.program_id(1)

    @pl.when(e == 0)
    def _():
        acc_ref[...] = jnp.zeros_like(acc_ref)

    h = h_ref[...]
    wexp = wexp_ref[...]
    lane = lax.broadcasted_iota(jnp.int32, wexp.shape, 1)
    we = jnp.sum(jnp.where(lane == N_GROUPS + e, wexp, 0.0), axis=-1, keepdims=True)
    hg = _dot(h, wg_ref[...])
    hu = _dot(h, wu_ref[...])
    act = (_silu(hg) * hu * we).astype(BF16)
    acc_ref[...] += _dot(act, wd_ref[...])

    @pl.when(e == pl.num_programs(1) - 1)
    def _():
        tm, d = acc_ref.shape
        nch = tm // CHUNK
        g2 = mod_ref[...]
        o_ref[...] = (x1_ref[...].reshape(nch, CHUNK, d) + g2 * acc_ref[...].reshape(nch, CHUNK, d)).reshape(tm, d)


def _moe_call(h2, wexp, x1, modc, wg, wu, wd, tm):
    t, d = x1.shape
    nch = tm // CHUNK
    ne, _, de = wg.shape
    tok = lambda w: pl.BlockSpec((tm, w), lambda i, e: (i, 0))
    return pl.pallas_call(
        _moe_kernel,
        out_shape=jax.ShapeDtypeStruct((t, d), F32),
        grid=(t // tm, ne),
        in_specs=[
            tok(d), tok(LANES), tok(d),
            pl.BlockSpec((nch, 1, d), lambda i, e: (i, 0, 5)),
            pl.BlockSpec((None, d, de), lambda i, e: (e, 0, 0)),
            pl.BlockSpec((None, d, de), lambda i, e: (e, 0, 0)),
            pl.BlockSpec((None, de, d), lambda i, e: (e, 0, 0)),
        ],
        out_specs=tok(d),
        scratch_shapes=[pltpu.VMEM((tm, d), F32)],
        compiler_params=pltpu.CompilerParams(dimension_semantics=("arbitrary", "arbitrary"),
                                             vmem_limit_bytes=VMEM_LIMIT),
        name="moe",
    )(h2, wexp, x1, modc, wg, wu, wd)


def _final_kernel(x_ref, nw_ref, o_ref):
    x = x_ref[...]
    o_ref[...] = x * lax.rsqrt(jnp.mean(x * x, axis=-1, keepdims=True) + EPS) * nw_ref[...]


def _final_call(x, norm_w, tm):
    t, d = x.shape
    return pl.pallas_call(
        _final_kernel,
        out_shape=jax.ShapeDtypeStruct((t, d), F32),
        grid=(t // tm,),
        in_specs=[pl.BlockSpec((tm, d), lambda i: (i, 0)), _const_spec((1, d))],
        out_specs=pl.BlockSpec((tm, d), lambda i: (i, 0)),
        name="final_norm",
    )(x, norm_w)


def _rope_table(pos):
    half = ROPE_DIM // 2
    inv = ROPE_THETA ** (-jnp.arange(half, dtype=F32) / half)
    ang = pos.astype(F32)[:, None] * inv[None, :]
    cos, sin = jnp.cos(ang), jnp.sin(ang)
    t = pos.shape[0]
    ones = jnp.ones((t, HD_A - ROPE_DIM), F32)
    zeros = jnp.zeros((t, HD_A - ROPE_DIM), F32)
    zh = jnp.zeros((t, half), F32)
    c_head = jnp.concatenate([cos, cos, ones], axis=1)
    s_lo = jnp.concatenate([zh, sin, zeros], axis=1)
    s_hi = jnp.concatenate([-sin, zh, zeros], axis=1)
    rep = LANES // HD_A
    return jnp.concatenate([jnp.tile(c_head, (1, rep)), jnp.tile(s_lo, (1, rep)), jnp.tile(s_hi, (1, rep))], axis=1)


def kernel(x_prompt, x_sample, cache_k_a, cache_v_a, state_conv_b, state_ssm_b, c_prompt, c_sample, w_mod, b_mod, norm_mix, w_in, sinks_a, w_o_a, conv_b, a_log_b, dt_bias_b, norm_o_b, w_o_b, w_out, norm_ffn, router_g, router_g_b, router_e, router_e_b, w_gate_e, w_up_e, w_down_e, norm_final):
    bp, seq, d = x_prompt.shape
    bs, dseq, _ = x_sample.shape
    depth = w_mod.shape[0]
    assert seq % CHUNK == 0 and dseq == CHUNK and d % LANES == 0
    assert cache_k_a.shape[2] == WIN_CHUNKS * CHUNK
    npc = seq // CHUNK
    tp, ts = bp * seq, bs * dseq
    t = tp + ts
    n_chunks = t // CHUNK
    nseq = bp + bs
    kvw = N_KV_A * HD_A
    cd = 2 * N_QK_B * DK_B + N_V_B * DV_B
    vw = N_V_B * DV_B
    nq = N_HEADS_A * HD_A

    seq_np = np.concatenate([np.repeat(np.arange(bp), npc), bp + np.arange(bs)]).astype(np.int32)
    local_np = np.concatenate([np.tile(np.arange(npc), bp), np.zeros(bs, np.int64)])
    first_np = (local_np == 0).astype(np.int32)
    last_np = np.concatenate([np.tile(np.arange(npc) == npc - 1, bp), np.ones(bs, bool)]).astype(np.int32)
    nvalid_np = np.concatenate([np.minimum(np.tile(np.arange(npc), bp), WIN_CHUNKS),
                                np.full(bs, WIN_CHUNKS)]).astype(np.int32)
    own_np = np.concatenate([
        (np.repeat(np.arange(bp), npc) * (npc + WIN_CHUNKS) + np.tile(np.arange(npc), bp) + WIN_CHUNKS),
        bp * (npc + WIN_CHUNKS) + np.arange(bs) * (WIN_CHUNKS + 1) + WIN_CHUNKS]).astype(np.int32)
    seq_i, first_i, last_i = jnp.asarray(seq_np), jnp.asarray(first_np), jnp.asarray(last_np)
    nvalid_i, own_i = jnp.asarray(nvalid_np), jnp.asarray(own_np)

    pos = jnp.concatenate([jnp.tile(jnp.arange(seq), bp), jnp.tile(PAST_LEN + jnp.arange(dseq), bs)])
    cs = _rope_table(pos)

    x = jnp.concatenate([x_prompt.reshape(tp, d), x_sample.reshape(ts, d)], axis=0)
    c_all = jnp.concatenate([c_prompt, c_sample], axis=0)
    c_rows = -(-nseq // 8) * 8
    c_all = jnp.pad(c_all, ((0, c_rows - nseq), (0, 0)))
    mod = _mod_call(c_all, w_mod, b_mod)
    modc = jnp.concatenate([
        jnp.broadcast_to(mod[:, :bp, None, :], (depth, bp, npc, 6 * d)).reshape(depth, bp * npc, 6 * d),
        mod[:, bp:nseq]], axis=1).reshape(depth, n_chunks, 1, 6 * d)

    tm = _pick_tile(t, 256)
    tm_moe = _pick_tile(t, 1024)
    sizes = np.cumsum([0, nq, kvw, kvw, cd, vw, N_V_B, N_V_B, d, d])
    outs = {k: [] for k in ("k_p", "v_p", "c_p", "s_p", "k_s", "v_s", "c_s", "s_s")}
    for l in range(depth):
        wl = w_in[l]
        seg = lambda a, b: wl[:, sizes[a]:sizes[b]].astype(BF16)
        wqkv, wxb, wz = seg(0, 3), seg(3, 4), seg(4, 5)
        wba = jnp.pad(seg(5, 7), ((0, 0), (0, LANES - 2 * N_V_B)))
        wga, wgb = seg(7, 8), seg(8, 9)
        q, k, v, xb, z, ba, ga, gb = _inproj_call(
            x, modc[l], norm_mix[l].reshape(1, d), cs, wqkv, wxb, wz, wba, wga, wgb, tm)

        def ext(new, cache):
            zp = jnp.zeros((bp, WIN_CHUNKS * CHUNK, kvw), F32)
            p = jnp.concatenate([zp, new[:tp].reshape(bp, seq, kvw)], axis=1).reshape(-1, kvw)
            s = jnp.concatenate([cache.reshape(bs, WIN_CHUNKS * CHUNK, kvw).astype(F32),
                                 new[tp:].reshape(bs, dseq, kvw)], axis=1).reshape(-1, kvw)
            return jnp.concatenate([p, s], axis=0)
        oa = _attn_call(own_i, nvalid_i, sinks_a[l], q, ext(k, cache_k_a[l]), ext(v, cache_v_a[l]))

        conv_init = jnp.concatenate([jnp.zeros((bp, CONV_W - 1, cd), F32), state_conv_b[l]], axis=0)
        conv_init = jnp.pad(conv_init, ((0, 0), (8 - (CONV_W - 1), 0), (0, 0)))
        ssm_init = jnp.concatenate([jnp.zeros((bp, N_V_B, DK_B, DV_B), F32), state_ssm_b[l]], axis=0)
        apar = jnp.zeros((2, LANES), F32)
        apar = apar.at[0, N_V_B:2 * N_V_B].set(a_log_b[l]).at[1, N_V_B:2 * N_V_B].set(dt_bias_b[l])
        ob, ssm_out = _gdn_call(seq_i, first_i, last_i, xb, conv_init, ssm_init, conv_b[l], ba, apar, z,
                                norm_o_b[l].reshape(1, DV_B))

        wr = jnp.concatenate([router_g[l], jnp.transpose(router_e[l], (1, 0, 2)).reshape(d, N_EXPERTS)], axis=1)
        wr = jnp.pad(wr, ((0, 0), (0, LANES - wr.shape[1])))
        br = jnp.concatenate([router_g_b[l], router_e_b[l].reshape(-1)])
        br = jnp.pad(br, (0, LANES - br.shape[0])).reshape(1, LANES)
        x1, h2, wexp = _outproj_call(x, oa, ob, ga, gb, modc[l], w_o_a[l].astype(BF16), w_o_b[l].astype(BF16),
                                     w_out[l].astype(BF16), norm_ffn[l].reshape(1, d), wr, br, tm)
        x = _moe_call(h2, wexp, x1, modc[l], w_gate_e[l].astype(BF16), w_up_e[l].astype(BF16),
                      w_down_e[l].astype(BF16), tm_moe)

        keep = min(WIN_CHUNKS * CHUNK, seq)
        kp, vp = k[:tp].reshape(bp, seq, N_KV_A, HD_A), v[:tp].reshape(bp, seq, N_KV_A, HD_A)
        outs["k_p"].append(kp[:, seq - keep:])
        outs["v_p"].append(vp[:, seq - keep:])
        outs["k_s"].append(k[tp:].reshape(bs, dseq, N_KV_A, HD_A))
        outs["v_s"].append(v[tp:].reshape(bs, dseq, N_KV_A, HD_A))
        outs["c_p"].append(xb[:tp].reshape(bp, seq, cd)[:, seq - (CONV_W - 1):])
        outs["c_s"].append(xb[tp:].reshape(bs, dseq, cd)[:, dseq - (CONV_W - 1):])
        outs["s_p"].append(ssm_out[:bp])
        outs["s_s"].append(ssm_out[bp:])

    y = _final_call(x, norm_final.reshape(1, d), tm)
    st = lambda key: jnp.stack(outs[key])
    return (y[:tp].reshape(bp, seq, d), y[tp:].reshape(bs, dseq, d),
            st("k_p"), st("v_p"), st("c_p"), st("s_p"), st("k_s"), st("v_s"), st("c_s"), st("s_s"))
```

```python
import functools

import numpy as np
import jax
import jax.numpy as jnp
from jax import lax
from jax.experimental import pallas as pl
from jax.experimental.pallas import tpu as pltpu

CHUNK = 64
PAST_LEN = 1024
N_HEADS_A = 8
N_KV_A = 2
HD_A = 64
WIN_CHUNKS = 2
ROPE_DIM = 16
ROPE_THETA = 500000.0
N_QK_B = 4
N_V_B = 8
DK_B = 128
DV_B = 128
CONV_W = 4
N_GROUPS = 4
EXPERTS_PER_GROUP = 4
N_EXPERTS = 16
D_EXPERT = 256
EPS = 1e-6
LANES = 128
VMEM_LIMIT = 56 * 1024 * 1024

F32 = jnp.float32
BF16 = jnp.bfloat16
HIGHEST = lax.Precision.HIGHEST


def _pick_tile(total, pref):
    t = pref
    while total % t:
        t //= 2
    assert t >= CHUNK
    return t


def _const_spec(shape):
    nd = len(shape)
    return pl.BlockSpec(shape, lambda *_: (0,) * nd)


def _silu(x):
    return x * (1.0 / (1.0 + jnp.exp(-x)))


def _sigmoid(x):
    return 1.0 / (1.0 + jnp.exp(-x))


def _dot(a, b):
    return jnp.dot(a, b, preferred_element_type=F32)


def _dot_nt(a, b):
    return lax.dot_general(a, b, (((1,), (1,)), ((), ())), preferred_element_type=F32)


def _dot_tn(a, b):
    return lax.dot_general(a, b, (((0,), (0,)), ((), ())), preferred_element_type=F32)


def _dot_hi(a, b):
    return jnp.dot(a, b, preferred_element_type=F32, precision=HIGHEST)


def _mod_kernel(c_ref, w_ref, b_ref, o_ref):
    o_ref[...] = _dot_hi(_silu(c_ref[...]), w_ref[...]) + b_ref[...]


def _mod_call(c_all, w_mod, b_mod):
    depth, d, d6 = w_mod.shape
    rows = c_all.shape[0]
    tn = 1024
    return pl.pallas_call(
        _mod_kernel,
        out_shape=jax.ShapeDtypeStruct((depth, rows, d6), F32),
        grid=(depth, d6 // tn),
        in_specs=[
            pl.BlockSpec((rows, d), lambda l, j: (0, 0)),
            pl.BlockSpec((None, d, tn), lambda l, j: (l, 0, j)),
            pl.BlockSpec((None, 1, tn), lambda l, j: (l, 0, j)),
        ],
        out_specs=pl.BlockSpec((None, rows, tn), lambda l, j: (l, 0, j)),
        compiler_params=pltpu.CompilerParams(vmem_limit_bytes=VMEM_LIMIT),
        name="mod",
    )(c_all, w_mod, b_mod.reshape(depth, 1, d6))


def _rope(x, cs):
    return (x * cs[:, :LANES]
            + pltpu.roll(x, 8, axis=1) * cs[:, LANES:2 * LANES]
            + pltpu.roll(x, LANES - 8, axis=1) * cs[:, 2 * LANES:])


def _inproj_kernel(x_ref, mod_ref, nw_ref, cs_ref, wqkv_ref, wxb_ref, wz_ref, wba_ref, wga_ref, wgb_ref,
                   q_ref, k_ref, v_ref, xb_ref, z_ref, ba_ref, ga_ref, gb_ref):
    x = x_ref[...]
    tm, d = x.shape
    nch = tm // CHUNK
    xn = x * lax.rsqrt(jnp.mean(x * x, axis=-1, keepdims=True) + EPS) * nw_ref[...]
    m = mod_ref[...]
    h = xn.reshape(nch, CHUNK, d) * (1.0 + m[:, :, d:]) + m[:, :, :d]
    h = h.reshape(tm, d).astype(BF16)
    cs = cs_ref[...]
    qkv = _dot(h, wqkv_ref[...])
    nq = N_HEADS_A * HD_A
    for g in range(nq // LANES):
        q_ref[:, g * LANES:(g + 1) * LANES] = _rope(qkv[:, g * LANES:(g + 1) * LANES], cs).astype(BF16)
    k_ref[...] = _rope(qkv[:, nq:nq + LANES], cs)
    v_ref[...] = qkv[:, nq + LANES:]
    xb_ref[...] = _dot(h, wxb_ref[...])
    z_ref[...] = _dot(h, wz_ref[...])
    ba_ref[...] = _dot(h, wba_ref[...])
    ga_ref[...] = _dot(h, wga_ref[...])
    gb_ref[...] = _dot(h, wgb_ref[...])


def _inproj_call(x, modc, norm_w, cs, wqkv, wxb, wz, wba, wga, wgb, tm):
    t, d = x.shape
    nch = tm // CHUNK
    widths = [N_HEADS_A * HD_A, LANES, LANES, wxb.shape[1], wz.shape[1], LANES, d, d]
    dtypes = [BF16, F32, F32, F32, F32, F32, F32, F32]
    tok = lambda w: pl.BlockSpec((tm, w), lambda i: (i, 0))
    return pl.pallas_call(
        _inproj_kernel,
        out_shape=[jax.ShapeDtypeStruct((t, w), dt) for w, dt in zip(widths, dtypes)],
        grid=(t // tm,),
        in_specs=[
            tok(d),
            pl.BlockSpec((nch, 1, 2 * d), lambda i: (i, 0, 0)),
            _const_spec((1, d)),
            tok(3 * LANES),
            _const_spec(wqkv.shape), _const_spec(wxb.shape), _const_spec(wz.shape),
            _const_spec(wba.shape), _const_spec(wga.shape), _const_spec(wgb.shape),
        ],
        out_specs=[tok(w) for w in widths],
        compiler_params=pltpu.CompilerParams(vmem_limit_bytes=VMEM_LIMIT),
        name="inproj",
    )(x, modc, norm_w, cs, wqkv, wxb, wz, wba, wga, wgb)


def _attn_kernel(nvalid_ref, sink_ref, q_ref, kp_ref, kt_ref, vp_ref, vt_ref, o_ref, *, cb):
    i = pl.program_id(0)
    kcat = jnp.concatenate([kp_ref[...], kt_ref[...]], axis=0).astype(BF16)
    vcat = jnp.concatenate([vp_ref[...], vt_ref[...]], axis=0).astype(BF16)
    nk = (WIN_CHUNKS + 1) * CHUNK
    key_chunk = lax.broadcasted_iota(jnp.int32, (CHUNK, nk), 1) // CHUNK
    group = N_HEADS_A // N_KV_A
    heads = range(N_HEADS_A)

    def scores(c):
        q = q_ref[c * CHUNK:(c + 1) * CHUNK, :]
        return [_dot_nt(q[:, h * HD_A:(h + 1) * HD_A],
                        kcat[c * CHUNK:c * CHUNK + nk, (h // group) * HD_A:(h // group + 1) * HD_A])
                for h in heads]

    s_next = scores(0)
    for c in range(cb):
        s_cur = s_next
        if c + 1 < cb:
            s_next = scores(c + 1)
        valid = key_chunk >= (WIN_CHUNKS - nvalid_ref[i * cb + c])
        p, den = [], []
        for h in heads:
            s = jnp.where(valid, s_cur[h] * (HD_A ** -0.5), -jnp.inf)
            sink = sink_ref[h]
            m = jnp.maximum(jnp.max(s, axis=-1, keepdims=True), sink)
            e = jnp.exp(s - m)
            p.append(e.astype(BF16))
            den.append(jnp.sum(e, axis=-1, keepdims=True) + jnp.exp(sink - m))
        o = [_dot(p[h], vcat[c * CHUNK:c * CHUNK + nk, (h // group) * HD_A:(h // group + 1) * HD_A])
             for h in heads]
        for h in heads:
            o_ref[c * CHUNK:(c + 1) * CHUNK, h * HD_A:(h + 1) * HD_A] = (o[h] / den[h]).astype(BF16)


def _attn_call(nvalid, sinks, q, k_prev, k_new, v_prev, v_new, cb, n_tiles, tile_off, prev_map):
    kvw = N_KV_A * HD_A
    qw = N_HEADS_A * HD_A
    tile = lambda w: pl.BlockSpec((cb * CHUNK, w), lambda i, nv: (i + tile_off, 0))
    prev = pl.BlockSpec((WIN_CHUNKS * CHUNK, kvw), lambda i, nv: (prev_map(i), 0))
    return pl.pallas_call(
        functools.partial(_attn_kernel, cb=cb),
        out_shape=jax.ShapeDtypeStruct((n_tiles * cb * CHUNK, qw), BF16),
        grid_spec=pltpu.PrefetchScalarGridSpec(
            num_scalar_prefetch=1,
            grid=(n_tiles,),
            in_specs=[pl.BlockSpec(memory_space=pltpu.SMEM), tile(qw), prev, tile(kvw), prev, tile(kvw)],
            out_specs=pl.BlockSpec((cb * CHUNK, qw), lambda i, nv: (i, 0)),
        ),
        compiler_params=pltpu.CompilerParams(vmem_limit_bytes=VMEM_LIMIT),
        name="attn",
    )(nvalid, sinks, q, k_prev, k_new, v_prev, v_new)


def _l2norm(x):
    return x * lax.rsqrt(jnp.sum(x * x, axis=-1, keepdims=True) + EPS)


def _gdn_kernel(seq_ref, first_ref, last_ref,
                xb_ref, cinit_ref, sinit_ref, cw_ref, ba_ref, apar_ref, z_ref, nw_ref,
                ob_ref, sout_ref, pad_ref, state_ref):
    i = pl.program_id(0)
    c = CHUNK

    @pl.when(first_ref[i] == 1)
    def _():
        pad_ref[0:8, :] = cinit_ref[...]
        state_ref[...] = sinit_ref[...]

    pad_ref[8:8 + c, :] = xb_ref[...]
    cw = cw_ref[...]
    conv = pad_ref[5:5 + c, :] * cw[0:1, :]
    for j in range(1, CONV_W):
        conv = conv + pad_ref[5 + j:5 + j + c, :] * cw[j:j + 1, :]
    pad_ref[0:8, :] = pad_ref[c:c + 8, :]
    qkv = _silu(conv)
    nqk = N_QK_B * DK_B

    ba = ba_ref[...]
    apar = apar_ref[...]
    beta = _sigmoid(ba[:, 0:N_V_B])
    sp_in = ba + apar[1:2, :]
    softplus = jnp.maximum(sp_in, 0.0) + jnp.log(1.0 + jnp.exp(-jnp.abs(sp_in)))
    g_all = -jnp.exp(apar[0:1, :]) * softplus
    row = lax.broadcasted_iota(jnp.int32, (c, c), 0)
    col = lax.broadcasted_iota(jnp.int32, (c, c), 1)
    lower = (row >= col).astype(BF16)
    g1 = g_all.astype(BF16)
    r1 = g_all - g1.astype(F32)
    g2 = r1.astype(BF16)
    g3 = (r1 - g2.astype(F32)).astype(BF16)
    gs = _dot(lower, jnp.concatenate([g1, g2, g3], axis=1))
    gc_all = gs[:, :LANES] + gs[:, LANES:2 * LANES] + gs[:, 2 * LANES:]
    gc_t = gc_all.T

    qn, kn, kt, kk, qk = [], [], [], [], []
    for j in range(N_QK_B):
        qj = _l2norm(qkv[:, j * DK_B:(j + 1) * DK_B]) * (DK_B ** -0.5)
        kj = _l2norm(qkv[:, nqk + j * DK_B:nqk + (j + 1) * DK_B])
        kjt = kj.T
        r = _dot(jnp.concatenate([qj, kj], axis=0).astype(BF16), kjt.astype(BF16))
        qn.append(qj)
        kn.append(kj)
        kt.append(kjt)
        qk.append(r[:c])
        kk.append(r[c:])

    rep = N_V_B // N_QK_B
    heads = range(N_V_B)
    gcol = [gc_all[:, N_V_B + h:N_V_B + h + 1] for h in heads]
    grow = [gc_t[N_V_B + h:N_V_B + h + 1, :] for h in heads]
    bcol = [beta[:, h:h + 1] for h in heads]
    decay = [jnp.exp(jnp.where(row >= col, gcol[h] - grow[h], -jnp.inf)) for h in heads]
    a = [jnp.where(row > col, bcol[h] * kk[h // rep] * decay[h], 0.0) for h in heads]
    xm = [-a[h] for h in heads]
    p = [_dot(a[h].astype(BF16), a[h].astype(BF16)) for h in heads]
    for _ in range(4):
        r = [_dot(jnp.concatenate([xm[h], p[h]], axis=0).astype(BF16), p[h].astype(BF16)) for h in heads]
        xm = [xm[h] + p[h] + r[h][:c] for h in heads]
        p = [r[h][c:] for h in heads]
    r = [_dot(xm[h].astype(BF16), p[h].astype(BF16)) for h in heads]
    xm = [xm[h] + p[h] + r[h] for h in heads]
    egc = [jnp.exp(gcol[h]) for h in heads]
    rhs = [jnp.concatenate([qkv[:, 2 * nqk + h * DV_B:2 * nqk + (h + 1) * DV_B] * bcol[h],
                            kn[h // rep] * (bcol[h] * egc[h])], axis=1) for h in heads]
    sol = [rhs[h] + _dot(xm[h].astype(BF16), rhs[h].astype(BF16)) for h in heads]
    s_old = [state_ref[h] for h in heads]
    r = [_dot(jnp.concatenate([sol[h][:, DV_B:], qn[h // rep] * egc[h]], axis=0).astype(BF16),
              s_old[h].astype(BF16)) for h in heads]
    v_new = [(sol[h][:, :DV_B] - r[h][:c]).astype(BF16) for h in heads]
    g_last = [gcol[h][c - 1:c, :] for h in heads]
    k_dec_t = [(kt[h // rep] * jnp.exp(g_last[h] - grow[h])).astype(BF16) for h in heads]
    for h in heads:
        state_ref[h] = s_old[h] * jnp.exp(g_last[h]) + _dot(k_dec_t[h], v_new[h])
    o = [r[h][c:] + _dot((qk[h // rep] * decay[h]).astype(BF16), v_new[h]) for h in heads]
    nw = nw_ref[...]
    for h in heads:
        on = o[h] * lax.rsqrt(jnp.mean(o[h] * o[h], axis=-1, keepdims=True) + EPS) * nw
        zh = z_ref[:, h * DV_B:(h + 1) * DV_B]
        ob_ref[:, h * DV_B:(h + 1) * DV_B] = (on * _silu(zh)).astype(BF16)

    @pl.when(last_ref[i] == 1)
    def _():
        sout_ref[...] = state_ref[...]


def _gdn_call(seq, first, last, xb, conv_init, ssm_init, conv_w, ba, apar, z, norm_o):
    t, cd = xb.shape
    n = t // CHUNK
    nseq = conv_init.shape[0]
    vw = N_V_B * DV_B
    chunk = lambda w: pl.BlockSpec((CHUNK, w), lambda i, s, f, l: (i, 0))
    return pl.pallas_call(
        _gdn_kernel,
        out_shape=[jax.ShapeDtypeStruct((t, vw), BF16),
                   jax.ShapeDtypeStruct((nseq, N_V_B, DK_B, DV_B), F32)],
        grid_spec=pltpu.PrefetchScalarGridSpec(
            num_scalar_prefetch=3,
            grid=(n,),
            in_specs=[
                chunk(cd),
                pl.BlockSpec((None, 8, cd), lambda i, s, f, l: (s[i], 0, 0)),
                pl.BlockSpec((None, N_V_B, DK_B, DV_B), lambda i, s, f, l: (s[i], 0, 0, 0)),
                pl.BlockSpec((CONV_W, cd), lambda i, s, f, l: (0, 0)),
                chunk(LANES),
                pl.BlockSpec((2, LANES), lambda i, s, f, l: (0, 0)),
                chunk(vw),
                pl.BlockSpec((1, DV_B), lambda i, s, f, l: (0, 0)),
            ],
            out_specs=[chunk(vw),
                       pl.BlockSpec((None, N_V_B, DK_B, DV_B), lambda i, s, f, l: (s[i], 0, 0, 0))],
            scratch_shapes=[pltpu.VMEM((CHUNK + 8, cd), F32),
                            pltpu.VMEM((N_V_B, DK_B, DV_B), F32)],
        ),
        compiler_params=pltpu.CompilerParams(dimension_semantics=("arbitrary",),
                                             vmem_limit_bytes=VMEM_LIMIT),
        name="gdn",
    )(seq, first, last, xb, conv_init, ssm_init, conv_w, ba, apar, z, norm_o)


def _outproj_kernel(x_ref, oa_ref, ob_ref, ga_ref, gb_ref, mod_ref, woa_ref, wob_ref, wout_ref, nw_ref,
                    wr_ref, br_ref, x1_ref, h2_ref, wexp_ref):
    x = x_ref[...]
    tm, d = x.shape
    nch = tm // CHUNK
    ya = _dot(oa_ref[...], woa_ref[...])
    yb = _dot(ob_ref[...], wob_ref[...])
    merged = _sigmoid(ga_ref[...]) * ya + _sigmoid(gb_ref[...]) * yb
    mix = _dot(merged.astype(BF16), wout_ref[...])
    m = mod_ref[...]
    x1 = (x.reshape(nch, CHUNK, d) + m[:, :, 2 * d:3 * d] * mix.reshape(nch, CHUNK, d))
    x1_ref[...] = x1.reshape(tm, d)
    xf = x1.reshape(tm, d)
    xn = xf * lax.rsqrt(jnp.mean(xf * xf, axis=-1, keepdims=True) + EPS) * nw_ref[...]
    h2 = (xn.reshape(nch, CHUNK, d) * (1.0 + m[:, :, 4 * d:5 * d]) + m[:, :, 3 * d:4 * d]).reshape(tm, d)
    h2_ref[...] = h2.astype(BF16)

    logits = _dot_hi(h2, wr_ref[...]) + br_ref[...]
    lane = lax.broadcasted_iota(jnp.int32, logits.shape, 1)
    neg = -jnp.inf
    lg = jnp.where(lane < N_GROUPS, logits, neg)
    mx = jnp.max(lg, axis=-1, keepdims=True)
    gi = jnp.min(jnp.where(lg == mx, lane, LANES), axis=-1, keepdims=True)
    p_group = 1.0 / jnp.sum(jnp.exp(lg - mx), axis=-1, keepdims=True)
    lo = N_GROUPS + EXPERTS_PER_GROUP * gi
    le = jnp.where((lane >= lo) & (lane < lo + EXPERTS_PER_GROUP), logits, neg)
    v1 = jnp.max(le, axis=-1, keepdims=True)
    i1 = jnp.min(jnp.where(le == v1, lane, LANES), axis=-1, keepdims=True)
    le2 = jnp.where(lane == i1, neg, le)
    v2 = jnp.max(le2, axis=-1, keepdims=True)
    i2 = jnp.min(jnp.where(le2 == v2, lane, LANES), axis=-1, keepdims=True)
    e2 = jnp.exp(v2 - v1)
    w1 = p_group / (1.0 + e2)
    w2 = p_group * e2 / (1.0 + e2)
    wexp_ref[...] = jnp.where(lane == i1, w1, 0.0) + jnp.where(lane == i2, w2, 0.0)


def _outproj_call(x, oa, ob, ga, gb, modc, woa, wob, wout, norm_w, wr, br, tm):
    t, d = x.shape
    nch = tm // CHUNK
    tok = lambda w: pl.BlockSpec((tm, w), lambda i: (i, 0))
    return pl.pallas_call(
        _outproj_kernel,
        out_shape=[jax.ShapeDtypeStruct((t, d), F32), jax.ShapeDtypeStruct((t, d), BF16),
                   jax.ShapeDtypeStruct((t, LANES), F32)],
        grid=(t // tm,),
        in_specs=[
            tok(d), tok(oa.shape[1]), tok(ob.shape[1]), tok(d), tok(d),
            pl.BlockSpec((nch, 1, 6 * d), lambda i: (i, 0, 0)),
            _const_spec(woa.shape), _const_spec(wob.shape), _const_spec(wout.shape),
            _const_spec((1, d)), _const_spec(wr.shape), _const_spec((1, LANES)),
        ],
        out_specs=[tok(d), tok(d), tok(LANES)],
        compiler_params=pltpu.CompilerParams(vmem_limit_bytes=VMEM_LIMIT),
        name="outproj",
    )(x, oa, ob, ga, gb, modc, woa, wob, wout, norm_w, wr, br)


def _moe_kernel(h_ref, wexp_ref, x1_ref, mod_ref, wg_ref, wu_ref, wd_ref, o_ref, acc_ref):
    e = pl.program_id(1)

    @pl.when(e == 0)
    def _():
        acc_ref[...] = jnp.zeros_like(acc_ref)

    h = h_ref[...]
    wexp = wexp_ref[...]
    lane = lax.broadcasted_iota(jnp.int32, wexp.shape, 1)
    we = jnp.sum(jnp.where(lane == N_GROUPS + e, wexp, 0.0), axis=-1, keepdims=True)
    hg = _dot(h, wg_ref[...])
    hu = _dot(h, wu_ref[...])
    act = (_silu(hg) * hu * we).astype(BF16)
    acc_ref[...] += _dot(act, wd_ref[...])

    @pl.when(e == pl.num_programs(1) - 1)
    def _():
        tm, d = acc_ref.shape
        nch = tm // CHUNK
        g2 = mod_ref[...]
        o_ref[...] = (x1_ref[...].reshape(nch, CHUNK, d) + g2 * acc_ref[...].reshape(nch, CHUNK, d)).reshape(tm, d)


def _moe_call(h2, wexp, x1, modc, wg, wu, wd, tm):
    t, d = x1.shape
    nch = tm // CHUNK
    ne, _, de = wg.shape
    tok = lambda w: pl.BlockSpec((tm, w), lambda i, e: (i, 0))
    return pl.pallas_call(
        _moe_kernel,
        out_shape=jax.ShapeDtypeStruct((t, d), F32),
        grid=(t // tm, ne),
        in_specs=[
            tok(d), tok(LANES), tok(d),
            pl.BlockSpec((nch, 1, d), lambda i, e: (i, 0, 5)),
            pl.BlockSpec((None, d, de), lambda i, e: (e, 0, 0)),
            pl.BlockSpec((None, d, de), lambda i, e: (e, 0, 0)),
            pl.BlockSpec((None, de, d), lambda i, e: (e, 0, 0)),
        ],
        out_specs=tok(d),
        scratch_shapes=[pltpu.VMEM((tm, d), F32)],
        compiler_params=pltpu.CompilerParams(dimension_semantics=("arbitrary", "arbitrary"),
                                             vmem_limit_bytes=VMEM_LIMIT),
        name="moe",
    )(h2, wexp, x1, modc, wg, wu, wd)


def _final_kernel(x_ref, nw_ref, o_ref):
    x = x_ref[...]
    o_ref[...] = x * lax.rsqrt(jnp.mean(x * x, axis=-1, keepdims=True) + EPS) * nw_ref[...]


def _final_call(x, norm_w, tm):
    t, d = x.shape
    return pl.pallas_call(
        _final_kernel,
        out_shape=jax.ShapeDtypeStruct((t, d), F32),
        grid=(t // tm,),
        in_specs=[pl.BlockSpec((tm, d), lambda i: (i, 0)), _const_spec((1, d))],
        out_specs=pl.BlockSpec((tm, d), lambda i: (i, 0)),
        name="final_norm",
    )(x, norm_w)


def _rope_table(pos):
    half = ROPE_DIM // 2
    inv = ROPE_THETA ** (-jnp.arange(half, dtype=F32) / half)
    ang = pos.astype(F32)[:, None] * inv[None, :]
    cos, sin = jnp.cos(ang), jnp.sin(ang)
    t = pos.shape[0]
    ones = jnp.ones((t, HD_A - ROPE_DIM), F32)
    zeros = jnp.zeros((t, HD_A - ROPE_DIM), F32)
    zh = jnp.zeros((t, half), F32)
    c_head = jnp.concatenate([cos, cos, ones], axis=1)
    s_lo = jnp.concatenate([zh, sin, zeros], axis=1)
    s_hi = jnp.concatenate([-sin, zh, zeros], axis=1)
    rep = LANES // HD_A
    return jnp.concatenate([jnp.tile(c_head, (1, rep)), jnp.tile(s_lo, (1, rep)), jnp.tile(s_hi, (1, rep))], axis=1)


def kernel(x_prompt, x_sample, cache_k_a, cache_v_a, state_conv_b, state_ssm_b, c_prompt, c_sample, w_mod, b_mod, norm_mix, w_in, sinks_a, w_o_a, conv_b, a_log_b, dt_bias_b, norm_o_b, w_o_b, w_out, norm_ffn, router_g, router_g_b, router_e, router_e_b, w_gate_e, w_up_e, w_down_e, norm_final):
    bp, seq, d = x_prompt.shape
    bs, dseq, _ = x_sample.shape
    depth = w_mod.shape[0]
    assert seq % CHUNK == 0 and dseq == CHUNK and d % LANES == 0
    assert cache_k_a.shape[2] == WIN_CHUNKS * CHUNK
    npc = seq // CHUNK
    tp, ts = bp * seq, bs * dseq
    t = tp + ts
    n_chunks = t // CHUNK
    nseq = bp + bs
    kvw = N_KV_A * HD_A
    cd = 2 * N_QK_B * DK_B + N_V_B * DV_B
    vw = N_V_B * DV_B
    nq = N_HEADS_A * HD_A

    seq_np = np.concatenate([np.repeat(np.arange(bp), npc), bp + np.arange(bs)]).astype(np.int32)
    local_np = np.concatenate([np.tile(np.arange(npc), bp), np.zeros(bs, np.int64)])
    first_np = (local_np == 0).astype(np.int32)
    last_np = np.concatenate([np.tile(np.arange(npc) == npc - 1, bp), np.ones(bs, bool)]).astype(np.int32)
    nvalid_np = np.concatenate([np.minimum(np.tile(np.arange(npc), bp), WIN_CHUNKS),
                                np.full(bs, WIN_CHUNKS)]).astype(np.int32)
    seq_i, first_i, last_i = jnp.asarray(seq_np), jnp.asarray(first_np), jnp.asarray(last_np)
    nvalid_p, nvalid_s = jnp.asarray(nvalid_np[:bp * npc]), jnp.asarray(nvalid_np[bp * npc:])
    cb = 4 if npc % 4 == 0 else 2
    assert npc % cb == 0

    pos = jnp.concatenate([jnp.tile(jnp.arange(seq), bp), jnp.tile(PAST_LEN + jnp.arange(dseq), bs)])
    cs = _rope_table(pos)

    x = jnp.concatenate([x_prompt.reshape(tp, d), x_sample.reshape(ts, d)], axis=0)
    c_all = jnp.concatenate([c_prompt, c_sample], axis=0)
    c_rows = -(-nseq // 8) * 8
    c_all = jnp.pad(c_all, ((0, c_rows - nseq), (0, 0)))
    mod = _mod_call(c_all, w_mod, b_mod)
    modc = jnp.concatenate([
        jnp.broadcast_to(mod[:, :bp, None, :], (depth, bp, npc, 6 * d)).reshape(depth, bp * npc, 6 * d),
        mod[:, bp:nseq]], axis=1).reshape(depth, n_chunks, 1, 6 * d)

    tm = _pick_tile(t, 256)
    tm_moe = _pick_tile(t, 1024)
    sizes = np.cumsum([0, nq, kvw, kvw, cd, vw, N_V_B, N_V_B, d, d])
    outs = {k: [] for k in ("k_p", "v_p", "c_p", "s_p", "k_s", "v_s", "c_s", "s_s")}
    for l in range(depth):
        wl = w_in[l]
        seg = lambda a, b: wl[:, sizes[a]:sizes[b]].astype(BF16)
        wqkv, wxb, wz = seg(0, 3), seg(3, 4), seg(4, 5)
        wba = jnp.pad(seg(5, 7), ((0, 0), (0, LANES - 2 * N_V_B)))
        wga, wgb = seg(7, 8), seg(8, 9)
        q, k, v, xb, z, ba, ga, gb = _inproj_call(
            x, modc[l], norm_mix[l].reshape(1, d), cs, wqkv, wxb, wz, wba, wga, wgb, tm)

        oa_p = _attn_call(nvalid_p, sinks_a[l], q, k, k, v, v, cb, tp // (cb * CHUNK), 0,
                          lambda i: jnp.maximum(i * (cb // WIN_CHUNKS) - 1, 0))
        oa_s = _attn_call(nvalid_s, sinks_a[l], q, cache_k_a[l].reshape(bs * WIN_CHUNKS * CHUNK, kvw), k,
                          cache_v_a[l].reshape(bs * WIN_CHUNKS * CHUNK, kvw), v, 1, bs, tp // CHUNK, lambda i: i)
        oa = jnp.concatenate([oa_p, oa_s], axis=0)

        conv_init = jnp.concatenate([jnp.zeros((bp, CONV_W - 1, cd), F32), state_conv_b[l]], axis=0)
        conv_init = jnp.pad(conv_init, ((0, 0), (8 - (CONV_W - 1), 0), (0, 0)))
        ssm_init = jnp.concatenate([jnp.zeros((bp, N_V_B, DK_B, DV_B), F32), state_ssm_b[l]], axis=0)
        apar = jnp.zeros((2, LANES), F32)
        apar = apar.at[0, N_V_B:2 * N_V_B].set(a_log_b[l]).at[1, N_V_B:2 * N_V_B].set(dt_bias_b[l])
        ob, ssm_out = _gdn_call(seq_i, first_i, last_i, xb, conv_init, ssm_init, conv_b[l], ba, apar, z,
                                norm_o_b[l].reshape(1, DV_B))

        wr = jnp.concatenate([router_g[l], jnp.transpose(router_e[l], (1, 0, 2)).reshape(d, N_EXPERTS)], axis=1)
        wr = jnp.pad(wr, ((0, 0), (0, LANES - wr.shape[1])))
        br = jnp.concatenate([router_g_b[l], router_e_b[l].reshape(-1)])
        br = jnp.pad(br, (0, LANES - br.shape[0])).reshape(1, LANES)
        x1, h2, wexp = _outproj_call(x, oa, ob, ga, gb, modc[l], w_o_a[l].astype(BF16), w_o_b[l].astype(BF16),
                                     w_out[l].astype(BF16), norm_ffn[l].reshape(1, d), wr, br, tm)
        x = _moe_call(h2, wexp, x1, modc[l], w_gate_e[l].astype(BF16), w_up_e[l].astype(BF16),
                      w_down_e[l].astype(BF16), tm_moe)

        keep = min(WIN_CHUNKS * CHUNK, seq)
        kp, vp = k[:tp].reshape(bp, seq, N_KV_A, HD_A), v[:tp].reshape(bp, seq, N_KV_A, HD_A)
        outs["k_p"].append(kp[:, seq - keep:])
        outs["v_p"].append(vp[:, seq - keep:])
        outs["k_s"].append(k[tp:].reshape(bs, dseq, N_KV_A, HD_A))
        outs["v_s"].append(v[tp:].reshape(bs, dseq, N_KV_A, HD_A))
        outs["c_p"].append(xb[:tp].reshape(bp, seq, cd)[:, seq - (CONV_W - 1):])
        outs["c_s"].append(xb[tp:].reshape(bs, dseq, cd)[:, dseq - (CONV_W - 1):])
        outs["s_p"].append(ssm_out[:bp])
        outs["s_s"].append(ssm_out[bp:])

    y = _final_call(x, norm_final.reshape(1, d), tm)
    st = lambda key: jnp.stack(outs[key])
    return (y[:tp].reshape(bp, seq, d), y[tp:].reshape(bs, dseq, d),
            st("k_p"), st("v_p"), st("c_p"), st("s_p"), st("k_s"), st("v_s"), st("c_s"), st("s_s"))
```

```python
import functools

import numpy as np
import jax
import jax.numpy as jnp
from jax import lax
from jax.experimental import pallas as pl
from jax.experimental.pallas import tpu as pltpu

CHUNK = 64
PAST_LEN = 1024
N_HEADS_A = 8
N_KV_A = 2
HD_A = 64
WIN_CHUNKS = 2
ROPE_DIM = 16
ROPE_THETA = 500000.0
N_QK_B = 4
N_V_B = 8
DK_B = 128
DV_B = 128
CONV_W = 4
N_GROUPS = 4
EXPERTS_PER_GROUP = 4
N_EXPERTS = 16
D_EXPERT = 256
EPS = 1e-6
LANES = 128
VMEM_LIMIT = 56 * 1024 * 1024

F32 = jnp.float32
BF16 = jnp.bfloat16
HIGHEST = lax.Precision.HIGHEST


def _pick_tile(total, pref):
    t = pref
    while total % t:
        t //= 2
    assert t >= CHUNK
    return t


def _const_spec(shape):
    nd = len(shape)
    return pl.BlockSpec(shape, lambda *_: (0,) * nd)


def _silu(x):
    return x * (1.0 / (1.0 + jnp.exp(-x)))


def _sigmoid(x):
    return 1.0 / (1.0 + jnp.exp(-x))


def _dot(a, b):
    return jnp.dot(a, b, preferred_element_type=F32)


def _dot_nt(a, b):
    return lax.dot_general(a, b, (((1,), (1,)), ((), ())), preferred_element_type=F32)


def _dot_tn(a, b):
    return lax.dot_general(a, b, (((0,), (0,)), ((), ())), preferred_element_type=F32)


def _dot_hi(a, b):
    return jnp.dot(a, b, preferred_element_type=F32, precision=HIGHEST)


def _mod_kernel(c_ref, w_ref, b_ref, o_ref):
    o_ref[...] = _dot_hi(_silu(c_ref[...]), w_ref[...]) + b_ref[...]


def _mod_call(c_all, w_mod, b_mod):
    depth, d, d6 = w_mod.shape
    rows = c_all.shape[0]
    tn = 1024
    return pl.pallas_call(
        _mod_kernel,
        out_shape=jax.ShapeDtypeStruct((depth, rows, d6), F32),
        grid=(depth, d6 // tn),
        in_specs=[
            pl.BlockSpec((rows, d), lambda l, j: (0, 0)),
            pl.BlockSpec((None, d, tn), lambda l, j: (l, 0, j)),
            pl.BlockSpec((None, 1, tn), lambda l, j: (l, 0, j)),
        ],
        out_specs=pl.BlockSpec((None, rows, tn), lambda l, j: (l, 0, j)),
        compiler_params=pltpu.CompilerParams(vmem_limit_bytes=VMEM_LIMIT),
        name="mod",
    )(c_all, w_mod, b_mod.reshape(depth, 1, d6))


def _rope(x, cs):
    return (x * cs[:, :LANES]
            + pltpu.roll(x, 8, axis=1) * cs[:, LANES:2 * LANES]
            + pltpu.roll(x, LANES - 8, axis=1) * cs[:, 2 * LANES:])


def _x_specs(xs, tm):
    d = xs[0].shape[1]
    if len(xs) == 1:
        return [pl.BlockSpec((tm, d), lambda i, *_: (i, 0))]
    n_first = xs[0].shape[0] // tm
    assert xs[0].shape[0] % tm == 0 and xs[1].shape[0] % tm == 0
    return [pl.BlockSpec((tm, d), lambda i, *_: (jnp.minimum(i, n_first - 1), 0)),
            pl.BlockSpec((tm, d), lambda i, *_: (jnp.maximum(i - n_first, 0), 0))]


def _load_x(i, x_refs, n_first):
    if len(x_refs) == 1:
        return x_refs[0][...]
    return jnp.where(i < n_first, x_refs[0][...], x_refs[1][...])


def _chunk_rows(seq_ref, mod_ref, i, nch, col, d):
    return [mod_ref[pl.ds(seq_ref[i * nch + c], 1), col * d:(col + 1) * d] for c in range(nch)]


def _per_chunk(x, fn):
    nch = x.shape[0] // CHUNK
    return jnp.concatenate([fn(c, x[c * CHUNK:(c + 1) * CHUNK]) for c in range(nch)], axis=0)


def _inproj_kernel(seq_ref, *refs, n_x, n_first):
    x_refs, (mod_ref, nw_ref, cs_ref, wqkv_ref, wxb_ref, wz_ref, wba_ref, wga_ref, wgb_ref,
             q_ref, k_ref, v_ref, xb_ref, z_ref, ba_ref, ga_ref, gb_ref) = refs[:n_x], refs[n_x:]
    i = pl.program_id(0)
    x = _load_x(i, x_refs, n_first)
    tm, d = x.shape
    nch = tm // CHUNK
    xn = x * lax.rsqrt(jnp.mean(x * x, axis=-1, keepdims=True) + EPS) * nw_ref[...]
    shift = _chunk_rows(seq_ref, mod_ref, i, nch, 0, d)
    scale = _chunk_rows(seq_ref, mod_ref, i, nch, 1, d)
    h = _per_chunk(xn, lambda c, r: r * (1.0 + scale[c]) + shift[c]).astype(BF16)
    cs = cs_ref[...]
    qkv = _dot(h, wqkv_ref[...])
    nq = N_HEADS_A * HD_A
    for g in range(nq // LANES):
        q_ref[:, g * LANES:(g + 1) * LANES] = _rope(qkv[:, g * LANES:(g + 1) * LANES], cs).astype(BF16)
    k_ref[...] = _rope(qkv[:, nq:nq + LANES], cs)
    v_ref[...] = qkv[:, nq + LANES:]
    xb_ref[...] = _dot(h, wxb_ref[...])
    z_ref[...] = _dot(h, wz_ref[...])
    ba_ref[...] = _dot(h, wba_ref[...])
    ga_ref[...] = _dot(h, wga_ref[...])
    gb_ref[...] = _dot(h, wgb_ref[...])


def _inproj_call(seq, xs, mod, norm_w, cs, wqkv, wxb, wz, wba, wga, wgb, tm):
    t = sum(x.shape[0] for x in xs)
    d = xs[0].shape[1]
    widths = [N_HEADS_A * HD_A, LANES, LANES, wxb.shape[1], wz.shape[1], LANES, d, d]
    dtypes = [BF16, F32, F32, F32, F32, F32, F32, F32]
    tok = lambda w: pl.BlockSpec((tm, w), lambda i, s: (i, 0))
    return pl.pallas_call(
        functools.partial(_inproj_kernel, n_x=len(xs), n_first=xs[0].shape[0] // tm),
        out_shape=[jax.ShapeDtypeStruct((t, w), dt) for w, dt in zip(widths, dtypes)],
        grid_spec=pltpu.PrefetchScalarGridSpec(
            num_scalar_prefetch=1,
            grid=(t // tm,),
            in_specs=_x_specs(xs, tm) + [
                _const_spec(mod.shape),
                _const_spec((1, d)),
                tok(3 * LANES),
                _const_spec(wqkv.shape), _const_spec(wxb.shape), _const_spec(wz.shape),
                _const_spec(wba.shape), _const_spec(wga.shape), _const_spec(wgb.shape),
            ],
            out_specs=[tok(w) for w in widths],
        ),
        compiler_params=pltpu.CompilerParams(vmem_limit_bytes=VMEM_LIMIT),
        name="inproj",
    )(seq, *xs, mod, norm_w, cs, wqkv, wxb, wz, wba, wga, wgb)


def _attn_kernel(nvalid_ref, sink_ref, q_ref, kp_ref, kt_ref, vp_ref, vt_ref, *rest, cb):
    o_ref = rest[-1]
    i = pl.program_id(0)
    kcat = jnp.concatenate([kp_ref[...], kt_ref[...]], axis=0).astype(BF16)
    vcat = jnp.concatenate([vp_ref[...], vt_ref[...]], axis=0).astype(BF16)
    nk = (WIN_CHUNKS + 1) * CHUNK
    key_chunk = lax.broadcasted_iota(jnp.int32, (CHUNK, nk), 1) // CHUNK
    group = N_HEADS_A // N_KV_A
    heads = range(N_HEADS_A)

    def scores(c):
        q = q_ref[c * CHUNK:(c + 1) * CHUNK, :]
        return [_dot_nt(q[:, h * HD_A:(h + 1) * HD_A],
                        kcat[c * CHUNK:c * CHUNK + nk, (h // group) * HD_A:(h // group + 1) * HD_A])
                for h in heads]

    s_next = scores(0)
    for c in range(cb):
        s_cur = s_next
        if c + 1 < cb:
            s_next = scores(c + 1)
        valid = key_chunk >= (WIN_CHUNKS - nvalid_ref[i * cb + c])
        p, den = [], []
        for h in heads:
            s = jnp.where(valid, s_cur[h] * (HD_A ** -0.5), -jnp.inf)
            sink = sink_ref[h]
            m = jnp.maximum(jnp.max(s, axis=-1, keepdims=True), sink)
            e = jnp.exp(s - m)
            p.append(e.astype(BF16))
            den.append(jnp.sum(e, axis=-1, keepdims=True) + jnp.exp(sink - m))
        o = [_dot(p[h], vcat[c * CHUNK:c * CHUNK + nk, (h // group) * HD_A:(h // group + 1) * HD_A])
             for h in heads]
        for h in heads:
            o_ref[c * CHUNK:(c + 1) * CHUNK, h * HD_A:(h + 1) * HD_A] = (o[h] / den[h]).astype(BF16)


def _attn_call(nvalid, sinks, q, k_prev, k_new, v_prev, v_new, cb, n_tiles, tile_off, prev_map, out_full=None):
    kvw = N_KV_A * HD_A
    qw = N_HEADS_A * HD_A
    tile = lambda w: pl.BlockSpec((cb * CHUNK, w), lambda i, nv: (i + tile_off, 0))
    prev = pl.BlockSpec((WIN_CHUNKS * CHUNK, kvw), lambda i, nv: (prev_map(i), 0))
    args = [nvalid, sinks, q, k_prev, k_new, v_prev, v_new]
    in_specs = [pl.BlockSpec(memory_space=pltpu.SMEM), tile(qw), prev, tile(kvw), prev, tile(kvw)]
    aliases = {}
    if out_full is not None:
        aliases = {len(args): 0}
        args.append(out_full)
        in_specs.append(pl.BlockSpec(memory_space=pl.ANY))
    return pl.pallas_call(
        functools.partial(_attn_kernel, cb=cb),
        out_shape=jax.ShapeDtypeStruct((q.shape[0], qw), BF16),
        grid_spec=pltpu.PrefetchScalarGridSpec(
            num_scalar_prefetch=1,
            grid=(n_tiles,),
            in_specs=in_specs,
            out_specs=tile(qw),
        ),
        input_output_aliases=aliases,
        compiler_params=pltpu.CompilerParams(vmem_limit_bytes=VMEM_LIMIT),
        name="attn",
    )(*args)


def _l2norm(x):
    return x * lax.rsqrt(jnp.sum(x * x, axis=-1, keepdims=True) + EPS)


def _gdn_kernel(seq_ref, first_ref, last_ref,
                xb_ref, cinit_ref, sinit_ref, cw_ref, ba_ref, apar_ref, z_ref, nw_ref,
                ob_ref, sout_ref, pad_ref, state_ref):
    i = pl.program_id(0)
    c = CHUNK

    @pl.when(first_ref[i] == 1)
    def _():
        pad_ref[0:8, :] = cinit_ref[...]
        state_ref[...] = sinit_ref[...]

    pad_ref[8:8 + c, :] = xb_ref[...]
    cw = cw_ref[...]
    conv = pad_ref[5:5 + c, :] * cw[0:1, :]
    for j in range(1, CONV_W):
        conv = conv + pad_ref[5 + j:5 + j + c, :] * cw[j:j + 1, :]
    pad_ref[0:8, :] = pad_ref[c:c + 8, :]
    qkv = _silu(conv)
    nqk = N_QK_B * DK_B

    ba = ba_ref[...]
    apar = apar_ref[...]
    beta = _sigmoid(ba[:, 0:N_V_B])
    sp_in = ba + apar[1:2, :]
    softplus = jnp.maximum(sp_in, 0.0) + jnp.log(1.0 + jnp.exp(-jnp.abs(sp_in)))
    g_all = -jnp.exp(apar[0:1, :]) * softplus
    row = lax.broadcasted_iota(jnp.int32, (c, c), 0)
    col = lax.broadcasted_iota(jnp.int32, (c, c), 1)
    lower = (row >= col).astype(BF16)
    g1 = g_all.astype(BF16)
    r1 = g_all - g1.astype(F32)
    g2 = r1.astype(BF16)
    g3 = (r1 - g2.astype(F32)).astype(BF16)
    gs = _dot(lower, jnp.concatenate([g1, g2, g3], axis=1))
    gc_all = gs[:, :LANES] + gs[:, LANES:2 * LANES] + gs[:, 2 * LANES:]
    gc_t = gc_all.T

    qn, kn, kt, kk, qk = [], [], [], [], []
    for j in range(N_QK_B):
        qj = _l2norm(qkv[:, j * DK_B:(j + 1) * DK_B]) * (DK_B ** -0.5)
        kj = _l2norm(qkv[:, nqk + j * DK_B:nqk + (j + 1) * DK_B])
        kjt = kj.T
        r = _dot(jnp.concatenate([qj, kj], axis=0).astype(BF16), kjt.astype(BF16))
        qn.append(qj)
        kn.append(kj)
        kt.append(kjt)
        qk.append(r[:c])
        kk.append(r[c:])

    rep = N_V_B // N_QK_B
    heads = range(N_V_B)
    gcol = [gc_all[:, N_V_B + h:N_V_B + h + 1] for h in heads]
    grow = [gc_t[N_V_B + h:N_V_B + h + 1, :] for h in heads]
    bcol = [beta[:, h:h + 1] for h in heads]
    decay = [jnp.exp(jnp.where(row >= col, gcol[h] - grow[h], -jnp.inf)) for h in heads]
    a = [jnp.where(row > col, bcol[h] * kk[h // rep] * decay[h], 0.0) for h in heads]
    xm = [-a[h] for h in heads]
    p = [_dot(a[h].astype(BF16), a[h].astype(BF16)) for h in heads]
    for _ in range(4):
        r = [_dot(jnp.concatenate([xm[h], p[h]], axis=0).astype(BF16), p[h].astype(BF16)) for h in heads]
        xm = [xm[h] + p[h] + r[h][:c] for h in heads]
        p = [r[h][c:] for h in heads]
    r = [_dot(xm[h].astype(BF16), p[h].astype(BF16)) for h in heads]
    xm = [xm[h] + p[h] + r[h] for h in heads]
    egc = [jnp.exp(gcol[h]) for h in heads]
    rhs = [jnp.concatenate([qkv[:, 2 * nqk + h * DV_B:2 * nqk + (h + 1) * DV_B] * bcol[h],
                            kn[h // rep] * (bcol[h] * egc[h])], axis=1) for h in heads]
    sol = [rhs[h] + _dot(xm[h].astype(BF16), rhs[h].astype(BF16)) for h in heads]
    s_old = [state_ref[h] for h in heads]
    r = [_dot(jnp.concatenate([sol[h][:, DV_B:], qn[h // rep] * egc[h]], axis=0).astype(BF16),
              s_old[h].astype(BF16)) for h in heads]
    v_new = [(sol[h][:, :DV_B] - r[h][:c]).astype(BF16) for h in heads]
    g_last = [gcol[h][c - 1:c, :] for h in heads]
    k_dec_t = [(kt[h // rep] * jnp.exp(g_last[h] - grow[h])).astype(BF16) for h in heads]
    for h in heads:
        state_ref[h] = s_old[h] * jnp.exp(g_last[h]) + _dot(k_dec_t[h], v_new[h])
    o = [r[h][c:] + _dot((qk[h // rep] * decay[h]).astype(BF16), v_new[h]) for h in heads]
    nw = nw_ref[...]
    for h in heads:
        on = o[h] * lax.rsqrt(jnp.mean(o[h] * o[h], axis=-1, keepdims=True) + EPS) * nw
        zh = z_ref[:, h * DV_B:(h + 1) * DV_B]
        ob_ref[:, h * DV_B:(h + 1) * DV_B] = (on * _silu(zh)).astype(BF16)

    @pl.when(last_ref[i] == 1)
    def _():
        sout_ref[...] = state_ref[...]


def _gdn_call(seq, first, last, xb, conv_init, ssm_init, conv_w, ba, apar, z, norm_o):
    t, cd = xb.shape
    n = t // CHUNK
    nseq = conv_init.shape[0]
    vw = N_V_B * DV_B
    chunk = lambda w: pl.BlockSpec((CHUNK, w), lambda i, s, f, l: (i, 0))
    return pl.pallas_call(
        _gdn_kernel,
        out_shape=[jax.ShapeDtypeStruct((t, vw), BF16),
                   jax.ShapeDtypeStruct((nseq, N_V_B, DK_B, DV_B), F32)],
        grid_spec=pltpu.PrefetchScalarGridSpec(
            num_scalar_prefetch=3,
            grid=(n,),
            in_specs=[
                chunk(cd),
                pl.BlockSpec((None, 8, cd), lambda i, s, f, l: (s[i], 0, 0)),
                pl.BlockSpec((None, N_V_B, DK_B, DV_B), lambda i, s, f, l: (s[i], 0, 0, 0)),
                pl.BlockSpec((CONV_W, cd), lambda i, s, f, l: (0, 0)),
                chunk(LANES),
                pl.BlockSpec((2, LANES), lambda i, s, f, l: (0, 0)),
                chunk(vw),
                pl.BlockSpec((1, DV_B), lambda i, s, f, l: (0, 0)),
            ],
            out_specs=[chunk(vw),
                       pl.BlockSpec((None, N_V_B, DK_B, DV_B), lambda i, s, f, l: (s[i], 0, 0, 0))],
            scratch_shapes=[pltpu.VMEM((CHUNK + 8, cd), F32),
                            pltpu.VMEM((N_V_B, DK_B, DV_B), F32)],
        ),
        compiler_params=pltpu.CompilerParams(dimension_semantics=("arbitrary",),
                                             vmem_limit_bytes=VMEM_LIMIT),
        name="gdn",
    )(seq, first, last, xb, conv_init, ssm_init, conv_w, ba, apar, z, norm_o)


def _outproj_kernel(seq_ref, *refs, n_x, n_first):
    x_refs, (oa_ref, ob_ref, ga_ref, gb_ref, mod_ref, woa_ref, wob_ref, wout_ref, nw_ref,
             wr_ref, br_ref, x1_ref, h2_ref, wexp_ref) = refs[:n_x], refs[n_x:]
    i = pl.program_id(0)
    x = _load_x(i, x_refs, n_first)
    tm, d = x.shape
    nch = tm // CHUNK
    ya = _dot(oa_ref[...], woa_ref[...])
    yb = _dot(ob_ref[...], wob_ref[...])
    merged = _sigmoid(ga_ref[...]) * ya + _sigmoid(gb_ref[...]) * yb
    mix = _dot(merged.astype(BF16), wout_ref[...])
    g1 = _chunk_rows(seq_ref, mod_ref, i, nch, 2, d)
    x1 = x + _per_chunk(mix, lambda c, r: g1[c] * r)
    x1_ref[...] = x1
    xn = x1 * lax.rsqrt(jnp.mean(x1 * x1, axis=-1, keepdims=True) + EPS) * nw_ref[...]
    shift = _chunk_rows(seq_ref, mod_ref, i, nch, 3, d)
    scale = _chunk_rows(seq_ref, mod_ref, i, nch, 4, d)
    h2 = _per_chunk(xn, lambda c, r: r * (1.0 + scale[c]) + shift[c])
    h2_hi = h2.astype(BF16)
    h2_ref[...] = h2_hi

    h2_lo = (h2 - h2_hi.astype(F32)).astype(BF16)
    wr = wr_ref[...]
    hi = _dot(h2_hi, wr)
    logits = hi[:, :LANES] + hi[:, LANES:] + _dot(h2_lo, wr[:, :LANES]) + br_ref[...]
    lane = lax.broadcasted_iota(jnp.int32, logits.shape, 1)
    neg = -jnp.inf
    lg = jnp.where(lane < N_GROUPS, logits, neg)
    mx = jnp.max(lg, axis=-1, keepdims=True)
    gi = jnp.min(jnp.where(lg == mx, lane, LANES), axis=-1, keepdims=True)
    p_group = 1.0 / jnp.sum(jnp.exp(lg - mx), axis=-1, keepdims=True)
    lo = N_GROUPS + EXPERTS_PER_GROUP * gi
    le = jnp.where((lane >= lo) & (lane < lo + EXPERTS_PER_GROUP), logits, neg)
    v1 = jnp.max(le, axis=-1, keepdims=True)
    i1 = jnp.min(jnp.where(le == v1, lane, LANES), axis=-1, keepdims=True)
    le2 = jnp.where(lane == i1, neg, le)
    v2 = jnp.max(le2, axis=-1, keepdims=True)
    i2 = jnp.min(jnp.where(le2 == v2, lane, LANES), axis=-1, keepdims=True)
    e2 = jnp.exp(v2 - v1)
    w1 = p_group / (1.0 + e2)
    w2 = p_group * e2 / (1.0 + e2)
    wexp_ref[...] = jnp.where(lane == i1, w1, 0.0) + jnp.where(lane == i2, w2, 0.0)


def _outproj_call(seq, xs, oa, ob, ga, gb, mod, woa, wob, wout, norm_w, wr, br, tm):
    t = sum(x.shape[0] for x in xs)
    d = xs[0].shape[1]
    tok = lambda w: pl.BlockSpec((tm, w), lambda i, s: (i, 0))
    return pl.pallas_call(
        functools.partial(_outproj_kernel, n_x=len(xs), n_first=xs[0].shape[0] // tm),
        out_shape=[jax.ShapeDtypeStruct((t, d), F32), jax.ShapeDtypeStruct((t, d), BF16),
                   jax.ShapeDtypeStruct((t, LANES), F32)],
        grid_spec=pltpu.PrefetchScalarGridSpec(
            num_scalar_prefetch=1,
            grid=(t // tm,),
            in_specs=_x_specs(xs, tm) + [
                tok(oa.shape[1]), tok(ob.shape[1]), tok(d), tok(d),
                _const_spec(mod.shape),
                _const_spec(woa.shape), _const_spec(wob.shape), _const_spec(wout.shape),
                _const_spec((1, d)), _const_spec(wr.shape), _const_spec((1, LANES)),
            ],
            out_specs=[tok(d), tok(d), tok(LANES)],
        ),
        compiler_params=pltpu.CompilerParams(vmem_limit_bytes=VMEM_LIMIT),
        name="outproj",
    )(seq, *xs, oa, ob, ga, gb, mod, woa, wob, wout, norm_w, wr, br)


def _moe_kernel(seq_ref, h_ref, wexp_ref, x1_ref, mod_ref, wg_ref, wu_ref, wd_ref, *rest, n_first):
    acc_ref = rest[-1]
    i = pl.program_id(0)
    e = pl.program_id(1)

    @pl.when(e == 0)
    def _():
        acc_ref[...] = jnp.zeros_like(acc_ref)

    h = h_ref[...]
    wexp = wexp_ref[...]
    lane = lax.broadcasted_iota(jnp.int32, wexp.shape, 1)
    we = jnp.sum(jnp.where(lane == N_GROUPS + e, wexp, 0.0), axis=-1, keepdims=True)
    hg = _dot(h, wg_ref[...])
    hu = _dot(h, wu_ref[...])
    act = (_silu(hg) * hu * we).astype(BF16)
    acc_ref[...] += _dot(act, wd_ref[...])

    @pl.when(e == pl.num_programs(1) - 1)
    def _():
        tm, d = acc_ref.shape
        nch = tm // CHUNK
        g2 = _chunk_rows(seq_ref, mod_ref, i, nch, 5, d)
        x2 = x1_ref[...] + _per_chunk(acc_ref[...], lambda c, r: g2[c] * r)
        if len(rest) == 2:
            rest[0][...] = x2
        else:
            nw_ref, yp_ref, ys_ref = rest[:3]
            y = x2 * lax.rsqrt(jnp.mean(x2 * x2, axis=-1, keepdims=True) + EPS) * nw_ref[...]

            @pl.when(i < n_first)
            def _():
                yp_ref[...] = y

            @pl.when(i >= n_first)
            def _():
                ys_ref[...] = y


def _moe_call(seq, h2, wexp, x1, mod, wg, wu, wd, tm, final=None):
    t, d = x1.shape
    ne, _, de = wg.shape
    tok = lambda w: pl.BlockSpec((tm, w), lambda i, e, s: (i, 0))
    in_specs = [
        tok(d), tok(LANES), tok(d),
        _const_spec(mod.shape),
        pl.BlockSpec((None, d, de), lambda i, e, s: (e, 0, 0)),
        pl.BlockSpec((None, d, de), lambda i, e, s: (e, 0, 0)),
        pl.BlockSpec((None, de, d), lambda i, e, s: (e, 0, 0)),
    ]
    args = [seq, h2, wexp, x1, mod, wg, wu, wd]
    if final is None:
        n_first = 0
        out_shape = jax.ShapeDtypeStruct((t, d), F32)
        out_specs = tok(d)
    else:
        norm_w, tp = final
        assert tp % tm == 0 and (t - tp) % tm == 0
        n_first = tp // tm
        in_specs.append(_const_spec((1, d)))
        args.append(norm_w)
        out_shape = [jax.ShapeDtypeStruct((tp, d), F32), jax.ShapeDtypeStruct((t - tp, d), F32)]
        out_specs = [pl.BlockSpec((tm, d), lambda i, e, s: (jnp.minimum(i, n_first - 1), 0)),
                     pl.BlockSpec((tm, d), lambda i, e, s: (jnp.maximum(i - n_first, 0), 0))]
    return pl.pallas_call(
        functools.partial(_moe_kernel, n_first=n_first),
        out_shape=out_shape,
        grid_spec=pltpu.PrefetchScalarGridSpec(
            num_scalar_prefetch=1,
            grid=(t // tm, ne),
            in_specs=in_specs,
            out_specs=out_specs,
            scratch_shapes=[pltpu.VMEM((tm, d), F32)],
        ),
        compiler_params=pltpu.CompilerParams(dimension_semantics=("arbitrary", "arbitrary"),
                                             vmem_limit_bytes=VMEM_LIMIT),
        name="moe",
    )(*args)


def _rope_table(pos):
    half = ROPE_DIM // 2
    inv = ROPE_THETA ** (-jnp.arange(half, dtype=F32) / half)
    ang = pos.astype(F32)[:, None] * inv[None, :]
    cos, sin = jnp.cos(ang), jnp.sin(ang)
    t = pos.shape[0]
    ones = jnp.ones((t, HD_A - ROPE_DIM), F32)
    zeros = jnp.zeros((t, HD_A - ROPE_DIM), F32)
    zh = jnp.zeros((t, half), F32)
    c_head = jnp.concatenate([cos, cos, ones], axis=1)
    s_lo = jnp.concatenate([zh, sin, zeros], axis=1)
    s_hi = jnp.concatenate([-sin, zh, zeros], axis=1)
    rep = LANES // HD_A
    return jnp.concatenate([jnp.tile(c_head, (1, rep)), jnp.tile(s_lo, (1, rep)), jnp.tile(s_hi, (1, rep))], axis=1)


def kernel(x_prompt, x_sample, cache_k_a, cache_v_a, state_conv_b, state_ssm_b, c_prompt, c_sample, w_mod, b_mod, norm_mix, w_in, sinks_a, w_o_a, conv_b, a_log_b, dt_bias_b, norm_o_b, w_o_b, w_out, norm_ffn, router_g, router_g_b, router_e, router_e_b, w_gate_e, w_up_e, w_down_e, norm_final):
    bp, seq, d = x_prompt.shape
    bs, dseq, _ = x_sample.shape
    depth = w_mod.shape[0]
    assert seq % CHUNK == 0 and dseq == CHUNK and d % LANES == 0
    assert cache_k_a.shape[2] == WIN_CHUNKS * CHUNK
    npc = seq // CHUNK
    tp, ts = bp * seq, bs * dseq
    t = tp + ts
    n_chunks = t // CHUNK
    nseq = bp + bs
    kvw = N_KV_A * HD_A
    cd = 2 * N_QK_B * DK_B + N_V_B * DV_B
    vw = N_V_B * DV_B
    nq = N_HEADS_A * HD_A

    seq_np = np.concatenate([np.repeat(np.arange(bp), npc), bp + np.arange(bs)]).astype(np.int32)
    local_np = np.concatenate([np.tile(np.arange(npc), bp), np.zeros(bs, np.int64)])
    first_np = (local_np == 0).astype(np.int32)
    last_np = np.concatenate([np.tile(np.arange(npc) == npc - 1, bp), np.ones(bs, bool)]).astype(np.int32)
    nvalid_np = np.concatenate([np.minimum(np.tile(np.arange(npc), bp), WIN_CHUNKS),
                                np.full(bs, WIN_CHUNKS)]).astype(np.int32)
    seq_i, first_i, last_i = jnp.asarray(seq_np), jnp.asarray(first_np), jnp.asarray(last_np)
    nvalid_p, nvalid_s = jnp.asarray(nvalid_np[:bp * npc]), jnp.asarray(nvalid_np[bp * npc:])
    cb = 4 if npc % 4 == 0 else 2
    assert npc % cb == 0

    pos = jnp.concatenate([jnp.tile(jnp.arange(seq), bp), jnp.tile(PAST_LEN + jnp.arange(dseq), bs)])
    cs = _rope_table(pos)

    xs = [x_prompt.reshape(tp, d), x_sample.reshape(ts, d)]
    c_all = jnp.concatenate([c_prompt, c_sample], axis=0)
    c_rows = -(-nseq // 8) * 8
    c_all = jnp.pad(c_all, ((0, c_rows - nseq), (0, 0)))
    mod = _mod_call(c_all, w_mod, b_mod)

    tm = _pick_tile(np.gcd(tp, ts), 256)
    tm_moe = _pick_tile(np.gcd(tp, ts), 1024)
    sizes = np.cumsum([0, nq, kvw, kvw, cd, vw, N_V_B, N_V_B, d, d])
    outs = {k: [] for k in ("k_p", "v_p", "c_p", "s_p", "k_s", "v_s", "c_s", "s_s")}
    for l in range(depth):
        wl = w_in[l]
        seg = lambda a, b: wl[:, sizes[a]:sizes[b]].astype(BF16)
        wqkv, wxb, wz = seg(0, 3), seg(3, 4), seg(4, 5)
        wba = jnp.pad(seg(5, 7), ((0, 0), (0, LANES - 2 * N_V_B)))
        wga, wgb = seg(7, 8), seg(8, 9)
        q, k, v, xb, z, ba, ga, gb = _inproj_call(
            seq_i, xs, mod[l], norm_mix[l].reshape(1, d), cs, wqkv, wxb, wz, wba, wga, wgb, tm)

        oa = _attn_call(nvalid_p, sinks_a[l], q, k, k, v, v, cb, tp // (cb * CHUNK), 0,
                        lambda i: jnp.maximum(i * (cb // WIN_CHUNKS) - 1, 0))
        oa = _attn_call(nvalid_s, sinks_a[l], q, cache_k_a[l].reshape(bs * WIN_CHUNKS * CHUNK, kvw), k,
                        cache_v_a[l].reshape(bs * WIN_CHUNKS * CHUNK, kvw), v, 1, bs, tp // CHUNK, lambda i: i,
                        out_full=oa)

        conv_init = jnp.concatenate([jnp.zeros((bp, CONV_W - 1, cd), F32), state_conv_b[l]], axis=0)
        conv_init = jnp.pad(conv_init, ((0, 0), (8 - (CONV_W - 1), 0), (0, 0)))
        ssm_init = jnp.concatenate([jnp.zeros((bp, N_V_B, DK_B, DV_B), F32), state_ssm_b[l]], axis=0)
        apar = jnp.zeros((2, LANES), F32)
        apar = apar.at[0, N_V_B:2 * N_V_B].set(a_log_b[l]).at[1, N_V_B:2 * N_V_B].set(dt_bias_b[l])
        ob, ssm_out = _gdn_call(seq_i, first_i, last_i, xb, conv_init, ssm_init, conv_b[l], ba, apar, z,
                                norm_o_b[l].reshape(1, DV_B))

        wr = jnp.concatenate([router_g[l], jnp.transpose(router_e[l], (1, 0, 2)).reshape(d, N_EXPERTS)], axis=1)
        wr = jnp.pad(wr, ((0, 0), (0, LANES - wr.shape[1])))
        wr_hi = wr.astype(BF16)
        wr = jnp.concatenate([wr_hi, (wr - wr_hi.astype(F32)).astype(BF16)], axis=1)
        br = jnp.concatenate([router_g_b[l], router_e_b[l].reshape(-1)])
        br = jnp.pad(br, (0, LANES - br.shape[0])).reshape(1, LANES)
        x1, h2, wexp = _outproj_call(seq_i, xs, oa, ob, ga, gb, mod[l], w_o_a[l].astype(BF16),
                                     w_o_b[l].astype(BF16), w_out[l].astype(BF16), norm_ffn[l].reshape(1, d),
                                     wr, br, tm)
        final = (norm_final.reshape(1, d), tp) if l == depth - 1 else None
        x_next = _moe_call(seq_i, h2, wexp, x1, mod[l], w_gate_e[l].astype(BF16), w_up_e[l].astype(BF16),
                           w_down_e[l].astype(BF16), tm_moe, final)
        xs = [x_next] if final is None else x_next

        keep = min(WIN_CHUNKS * CHUNK, seq)
        ends = [(b + 1) * seq for b in range(bp)]
        tail = lambda a, n: jnp.stack([a[e - n:e] for e in ends])
        outs["k_p"].append(tail(k, keep).reshape(bp, keep, N_KV_A, HD_A))
        outs["v_p"].append(tail(v, keep).reshape(bp, keep, N_KV_A, HD_A))
        outs["k_s"].append(k[tp:].reshape(bs, dseq, N_KV_A, HD_A))
        outs["v_s"].append(v[tp:].reshape(bs, dseq, N_KV_A, HD_A))
        outs["c_p"].append(tail(xb, CONV_W - 1))
        outs["c_s"].append(xb[tp:].reshape(bs, dseq, cd)[:, dseq - (CONV_W - 1):])
        outs["s_p"].append(ssm_out[:bp])
        outs["s_s"].append(ssm_out[bp:])

    st = lambda key: jnp.stack(outs[key])
    return (xs[0].reshape(bp, seq, d), xs[1].reshape(bs, dseq, d),
            st("k_p"), st("v_p"), st("c_p"), st("s_p"), st("k_s"), st("v_s"), st("c_s"), st("s_s"))
```

```python
import functools

import numpy as np
import jax
import jax.numpy as jnp
from jax import lax
from jax.experimental import pallas as pl
from jax.experimental.pallas import tpu as pltpu

CHUNK = 64
PAST_LEN = 1024
N_HEADS_A = 8
N_KV_A = 2
HD_A = 64
WIN_CHUNKS = 2
ROPE_DIM = 16
ROPE_THETA = 500000.0
N_QK_B = 4
N_V_B = 8
DK_B = 128
DV_B = 128
CONV_W = 4
N_GROUPS = 4
EXPERTS_PER_GROUP = 4
N_EXPERTS = 16
D_EXPERT = 256
EPS = 1e-6
LANES = 128
VMEM_LIMIT = 56 * 1024 * 1024
MOE_EXPERTS_PER_STEP = 2
GDN_CHUNKS_PER_STEP = 8

F32 = jnp.float32
BF16 = jnp.bfloat16
HIGHEST = lax.Precision.HIGHEST


def _pick_tile(total, pref):
    t = pref
    while total % t:
        t //= 2
    assert t >= CHUNK
    return t


def _const_spec(shape):
    nd = len(shape)
    return pl.BlockSpec(shape, lambda *_: (0,) * nd)


def _silu(x):
    return x * (1.0 / (1.0 + jnp.exp(-x)))


def _sigmoid(x):
    return 1.0 / (1.0 + jnp.exp(-x))


def _dot(a, b):
    return jnp.dot(a, b, preferred_element_type=F32)


def _dot_nt(a, b):
    return lax.dot_general(a, b, (((1,), (1,)), ((), ())), preferred_element_type=F32)


def _dot_tn(a, b):
    return lax.dot_general(a, b, (((0,), (0,)), ((), ())), preferred_element_type=F32)


def _dot_hi(a, b):
    return jnp.dot(a, b, preferred_element_type=F32, precision=HIGHEST)


def _mod_kernel(c_ref, w_ref, b_ref, o_ref):
    o_ref[...] = _dot_hi(_silu(c_ref[...]), w_ref[...]) + b_ref[...]


def _mod_call(c_all, w_mod, b_mod):
    depth, d, d6 = w_mod.shape
    rows = c_all.shape[0]
    tn = 1024
    return pl.pallas_call(
        _mod_kernel,
        out_shape=jax.ShapeDtypeStruct((depth, rows, d6), F32),
        grid=(depth, d6 // tn),
        in_specs=[
            pl.BlockSpec((rows, d), lambda l, j: (0, 0)),
            pl.BlockSpec((None, d, tn), lambda l, j: (l, 0, j)),
            pl.BlockSpec((None, 1, tn), lambda l, j: (l, 0, j)),
        ],
        out_specs=pl.BlockSpec((None, rows, tn), lambda l, j: (l, 0, j)),
        compiler_params=pltpu.CompilerParams(vmem_limit_bytes=VMEM_LIMIT),
        name="mod",
    )(c_all, w_mod, b_mod.reshape(depth, 1, d6))


def _rope(x, cs):
    return (x * cs[:, :LANES]
            + pltpu.roll(x, 8, axis=1) * cs[:, LANES:2 * LANES]
            + pltpu.roll(x, LANES - 8, axis=1) * cs[:, 2 * LANES:])


def _x_specs(xs, tm):
    d = xs[0].shape[1]
    if len(xs) == 1:
        return [pl.BlockSpec((tm, d), lambda i, *_: (i, 0))]
    n_first = xs[0].shape[0] // tm
    assert xs[0].shape[0] % tm == 0 and xs[1].shape[0] % tm == 0
    return [pl.BlockSpec((tm, d), lambda i, *_: (jnp.minimum(i, n_first - 1), 0)),
            pl.BlockSpec((tm, d), lambda i, *_: (jnp.maximum(i - n_first, 0), 0))]


def _load_x(i, x_refs, n_first):
    if len(x_refs) == 1:
        return x_refs[0][...]
    return jnp.where(i < n_first, x_refs[0][...], x_refs[1][...])


def _chunk_rows(seq_ref, mod_ref, i, nch, col, d):
    return [mod_ref[pl.ds(seq_ref[i * nch + c], 1), col * d:(col + 1) * d] for c in range(nch)]


def _per_chunk(x, fn):
    nch = x.shape[0] // CHUNK
    return jnp.concatenate([fn(c, x[c * CHUNK:(c + 1) * CHUNK]) for c in range(nch)], axis=0)


def _inproj_kernel(seq_ref, *refs, n_x, n_first):
    x_refs, (mod_ref, nw_ref, cs_ref, wqkv_ref, wxb_ref, wz_ref, wba_ref, wga_ref, wgb_ref,
             q_ref, k_ref, v_ref, xb_ref, z_ref, ba_ref, ga_ref, gb_ref) = refs[:n_x], refs[n_x:]
    i = pl.program_id(0)
    x = _load_x(i, x_refs, n_first)
    tm, d = x.shape
    nch = tm // CHUNK
    xn = x * lax.rsqrt(jnp.mean(x * x, axis=-1, keepdims=True) + EPS) * nw_ref[...]
    shift = _chunk_rows(seq_ref, mod_ref, i, nch, 0, d)
    scale = _chunk_rows(seq_ref, mod_ref, i, nch, 1, d)
    h = _per_chunk(xn, lambda c, r: r * (1.0 + scale[c]) + shift[c]).astype(BF16)
    cs = cs_ref[...]
    qkv = _dot(h, wqkv_ref[...])
    nq = N_HEADS_A * HD_A
    for g in range(nq // LANES):
        q_ref[:, g * LANES:(g + 1) * LANES] = _rope(qkv[:, g * LANES:(g + 1) * LANES], cs).astype(BF16)
    k_ref[...] = _rope(qkv[:, nq:nq + LANES], cs)
    v_ref[...] = qkv[:, nq + LANES:]
    xb_ref[...] = _dot(h, wxb_ref[...])
    z_ref[...] = _dot(h, wz_ref[...])
    ba_ref[...] = _dot(h, wba_ref[...])
    ga_ref[...] = _dot(h, wga_ref[...])
    gb_ref[...] = _dot(h, wgb_ref[...])


def _inproj_call(seq, xs, mod, norm_w, cs, wqkv, wxb, wz, wba, wga, wgb, tm):
    t = sum(x.shape[0] for x in xs)
    d = xs[0].shape[1]
    widths = [N_HEADS_A * HD_A, LANES, LANES, wxb.shape[1], wz.shape[1], LANES, d, d]
    dtypes = [BF16, F32, F32, F32, F32, F32, F32, F32]
    tok = lambda w: pl.BlockSpec((tm, w), lambda i, s: (i, 0))
    return pl.pallas_call(
        functools.partial(_inproj_kernel, n_x=len(xs), n_first=xs[0].shape[0] // tm),
        out_shape=[jax.ShapeDtypeStruct((t, w), dt) for w, dt in zip(widths, dtypes)],
        grid_spec=pltpu.PrefetchScalarGridSpec(
            num_scalar_prefetch=1,
            grid=(t // tm,),
            in_specs=_x_specs(xs, tm) + [
                _const_spec(mod.shape),
                _const_spec((1, d)),
                tok(3 * LANES),
                _const_spec(wqkv.shape), _const_spec(wxb.shape), _const_spec(wz.shape),
                _const_spec(wba.shape), _const_spec(wga.shape), _const_spec(wgb.shape),
            ],
            out_specs=[tok(w) for w in widths],
        ),
        compiler_params=pltpu.CompilerParams(vmem_limit_bytes=VMEM_LIMIT),
        name="inproj",
    )(seq, *xs, mod, norm_w, cs, wqkv, wxb, wz, wba, wga, wgb)


def _attn_kernel(nvalid_ref, sink_ref, q_ref, kp_ref, kt_ref, vp_ref, vt_ref, *rest, cb):
    o_ref = rest[-1]
    i = pl.program_id(0)
    kcat = jnp.concatenate([kp_ref[...], kt_ref[...]], axis=0).astype(BF16)
    vcat = jnp.concatenate([vp_ref[...], vt_ref[...]], axis=0).astype(BF16)
    nk = (WIN_CHUNKS + 1) * CHUNK
    key_chunk = lax.broadcasted_iota(jnp.int32, (CHUNK, nk), 1) // CHUNK
    group = N_HEADS_A // N_KV_A
    heads = range(N_HEADS_A)

    def scores(c):
        q = q_ref[c * CHUNK:(c + 1) * CHUNK, :]
        return [_dot_nt(q[:, h * HD_A:(h + 1) * HD_A],
                        kcat[c * CHUNK:c * CHUNK + nk, (h // group) * HD_A:(h // group + 1) * HD_A])
                for h in heads]

    s_next = scores(0)
    for c in range(cb):
        s_cur = s_next
        if c + 1 < cb:
            s_next = scores(c + 1)
        valid = key_chunk >= (WIN_CHUNKS - nvalid_ref[i * cb + c])
        p, den = [], []
        for h in heads:
            s = jnp.where(valid, s_cur[h] * (HD_A ** -0.5), -jnp.inf)
            sink = sink_ref[h]
            m = jnp.maximum(jnp.max(s, axis=-1, keepdims=True), sink)
            e = jnp.exp(s - m)
            p.append(e.astype(BF16))
            den.append(jnp.sum(e, axis=-1, keepdims=True) + jnp.exp(sink - m))
        o = [_dot(p[h], vcat[c * CHUNK:c * CHUNK + nk, (h // group) * HD_A:(h // group + 1) * HD_A])
             for h in heads]
        for h in heads:
            o_ref[c * CHUNK:(c + 1) * CHUNK, h * HD_A:(h + 1) * HD_A] = (o[h] / den[h]).astype(BF16)


def _attn_call(nvalid, sinks, q, k_prev, k_new, v_prev, v_new, cb, n_tiles, tile_off, prev_map, out_full=None):
    kvw = N_KV_A * HD_A
    qw = N_HEADS_A * HD_A
    tile = lambda w: pl.BlockSpec((cb * CHUNK, w), lambda i, nv: (i + tile_off, 0))
    prev = pl.BlockSpec((WIN_CHUNKS * CHUNK, kvw), lambda i, nv: (prev_map(i), 0))
    args = [nvalid, sinks, q, k_prev, k_new, v_prev, v_new]
    in_specs = [pl.BlockSpec(memory_space=pltpu.SMEM), tile(qw), prev, tile(kvw), prev, tile(kvw)]
    aliases = {}
    if out_full is not None:
        aliases = {len(args): 0}
        args.append(out_full)
        in_specs.append(pl.BlockSpec(memory_space=pl.ANY))
    return pl.pallas_call(
        functools.partial(_attn_kernel, cb=cb),
        out_shape=jax.ShapeDtypeStruct((q.shape[0], qw), BF16),
        grid_spec=pltpu.PrefetchScalarGridSpec(
            num_scalar_prefetch=1,
            grid=(n_tiles,),
            in_specs=in_specs,
            out_specs=tile(qw),
        ),
        input_output_aliases=aliases,
        compiler_params=pltpu.CompilerParams(vmem_limit_bytes=VMEM_LIMIT),
        name="attn",
    )(*args)


def _l2norm(x):
    return x * lax.rsqrt(jnp.sum(x * x, axis=-1, keepdims=True) + EPS)


def _gdn_kernel(first_ref, last_ref, xb_ref, cinit_ref, sinit_ref, cw_ref, ba_ref, apar_ref, z_ref, nw_ref,
                *rest, cb):
    ob_ref, sout_ref, pad_ref, state_ref = rest[-4:]
    step = pl.program_id(0)

    def chunk_body(ci, carry):
        _gdn_chunk(step * cb + ci, pl.ds(pl.multiple_of(ci * CHUNK, CHUNK), CHUNK), first_ref, last_ref,
                   xb_ref, cinit_ref, sinit_ref, cw_ref, ba_ref, apar_ref, z_ref, nw_ref,
                   ob_ref, sout_ref, pad_ref, state_ref)
        return carry

    lax.fori_loop(0, cb, chunk_body, 0)


def _gdn_chunk(i, rows, first_ref, last_ref, xb_ref, cinit_ref, sinit_ref, cw_ref, ba_ref, apar_ref, z_ref,
               nw_ref, ob_ref, sout_ref, pad_ref, state_ref):
    c = CHUNK

    @pl.when(first_ref[i] == 1)
    def _():
        pad_ref[0:8, :] = cinit_ref[...]
        state_ref[...] = sinit_ref[...]

    pad_ref[8:8 + c, :] = xb_ref[rows, :]
    cw = cw_ref[...]
    conv = pad_ref[5:5 + c, :] * cw[0:1, :]
    for j in range(1, CONV_W):
        conv = conv + pad_ref[5 + j:5 + j + c, :] * cw[j:j + 1, :]
    pad_ref[0:8, :] = pad_ref[c:c + 8, :]
    qkv = _silu(conv)
    nqk = N_QK_B * DK_B

    ba = ba_ref[rows, :]
    apar = apar_ref[...]
    beta = _sigmoid(ba[:, 0:N_V_B])
    sp_in = ba + apar[1:2, :]
    softplus = jnp.maximum(sp_in, 0.0) + jnp.log(1.0 + jnp.exp(-jnp.abs(sp_in)))
    g_all = -jnp.exp(apar[0:1, :]) * softplus
    row = lax.broadcasted_iota(jnp.int32, (c, c), 0)
    col = lax.broadcasted_iota(jnp.int32, (c, c), 1)
    lower = (row >= col).astype(BF16)
    g1 = g_all.astype(BF16)
    r1 = g_all - g1.astype(F32)
    g2 = r1.astype(BF16)
    g3 = (r1 - g2.astype(F32)).astype(BF16)
    gs = _dot(lower, jnp.concatenate([g1, g2, g3], axis=1))
    gc_all = gs[:, :LANES] + gs[:, LANES:2 * LANES] + gs[:, 2 * LANES:]
    gc_t = gc_all.T

    qn, kn, kt, kk, qk = [], [], [], [], []
    for j in range(N_QK_B):
        qj = _l2norm(qkv[:, j * DK_B:(j + 1) * DK_B]) * (DK_B ** -0.5)
        kj = _l2norm(qkv[:, nqk + j * DK_B:nqk + (j + 1) * DK_B])
        kjt = kj.T
        r = _dot(jnp.concatenate([qj, kj], axis=0).astype(BF16), kjt.astype(BF16))
        qn.append(qj)
        kn.append(kj)
        kt.append(kjt)
        qk.append(r[:c])
        kk.append(r[c:])

    rep = N_V_B // N_QK_B
    heads = range(N_V_B)
    gcol = [gc_all[:, N_V_B + h:N_V_B + h + 1] for h in heads]
    grow = [gc_t[N_V_B + h:N_V_B + h + 1, :] for h in heads]
    bcol = [beta[:, h:h + 1] for h in heads]
    decay = [jnp.exp(jnp.where(row >= col, gcol[h] - grow[h], -jnp.inf)) for h in heads]
    a = [jnp.where(row > col, bcol[h] * kk[h // rep] * decay[h], 0.0) for h in heads]
    xm = [-a[h] for h in heads]
    p = [_dot(a[h].astype(BF16), a[h].astype(BF16)) for h in heads]
    for _ in range(4):
        r = [_dot(jnp.concatenate([xm[h], p[h]], axis=0).astype(BF16), p[h].astype(BF16)) for h in heads]
        xm = [xm[h] + p[h] + r[h][:c] for h in heads]
        p = [r[h][c:] for h in heads]
    r = [_dot(xm[h].astype(BF16), p[h].astype(BF16)) for h in heads]
    xm = [xm[h] + p[h] + r[h] for h in heads]
    egc = [jnp.exp(gcol[h]) for h in heads]
    rhs = [jnp.concatenate([qkv[:, 2 * nqk + h * DV_B:2 * nqk + (h + 1) * DV_B] * bcol[h],
                            kn[h // rep] * (bcol[h] * egc[h])], axis=1) for h in heads]
    sol = [rhs[h] + _dot(xm[h].astype(BF16), rhs[h].astype(BF16)) for h in heads]
    s_old = [state_ref[h] for h in heads]
    r = [_dot(jnp.concatenate([sol[h][:, DV_B:], qn[h // rep] * egc[h]], axis=0).astype(BF16),
              s_old[h].astype(BF16)) for h in heads]
    v_new = [(sol[h][:, :DV_B] - r[h][:c]).astype(BF16) for h in heads]
    g_last = [gcol[h][c - 1:c, :] for h in heads]
    k_dec_t = [(kt[h // rep] * jnp.exp(g_last[h] - grow[h])).astype(BF16) for h in heads]
    for h in heads:
        state_ref[h] = s_old[h] * jnp.exp(g_last[h]) + _dot(k_dec_t[h], v_new[h])
    o = [r[h][c:] + _dot((qk[h // rep] * decay[h]).astype(BF16), v_new[h]) for h in heads]
    nw = nw_ref[...]
    for h in heads:
        on = o[h] * lax.rsqrt(jnp.mean(o[h] * o[h], axis=-1, keepdims=True) + EPS) * nw
        zh = z_ref[rows, h * DV_B:(h + 1) * DV_B]
        ob_ref[rows, h * DV_B:(h + 1) * DV_B] = (on * _silu(zh)).astype(BF16)

    @pl.when(last_ref[i] == 1)
    def _():
        sout_ref[...] = state_ref[...]


def _gdn_call(first, last, xb, conv_init, ssm_init, conv_w, ba, apar, z, norm_o, cb, n_steps, tile_off,
              steps_per_seq, out_full=None):
    t, cd = xb.shape
    nseq = conv_init.shape[0]
    vw = N_V_B * DV_B
    tile = lambda w: pl.BlockSpec((cb * CHUNK, w), lambda i, f, l: (i + tile_off, 0))
    const = lambda shape: pl.BlockSpec(shape, lambda i, f, l: (0,) * len(shape))
    state = pl.BlockSpec((None, N_V_B, DK_B, DV_B), lambda i, f, l: (i // steps_per_seq, 0, 0, 0))
    args = [first, last, xb, conv_init, ssm_init, conv_w, ba, apar, z, norm_o]
    in_specs = [
        tile(cd),
        pl.BlockSpec((None, 8, cd), lambda i, f, l: (i // steps_per_seq, 0, 0)),
        state,
        const((CONV_W, cd)),
        tile(LANES),
        const((2, LANES)),
        tile(vw),
        const((1, DV_B)),
    ]
    aliases = {}
    if out_full is not None:
        aliases = {len(args): 0}
        args.append(out_full)
        in_specs.append(pl.BlockSpec(memory_space=pl.ANY))
    return pl.pallas_call(
        functools.partial(_gdn_kernel, cb=cb),
        out_shape=[jax.ShapeDtypeStruct((t, vw), BF16),
                   jax.ShapeDtypeStruct((nseq, N_V_B, DK_B, DV_B), F32)],
        grid_spec=pltpu.PrefetchScalarGridSpec(
            num_scalar_prefetch=2,
            grid=(n_steps,),
            in_specs=in_specs,
            out_specs=[tile(vw), state],
            scratch_shapes=[pltpu.VMEM((CHUNK + 8, cd), F32),
                            pltpu.VMEM((N_V_B, DK_B, DV_B), F32)],
        ),
        input_output_aliases=aliases,
        compiler_params=pltpu.CompilerParams(dimension_semantics=("arbitrary",),
                                             vmem_limit_bytes=VMEM_LIMIT),
        name="gdn",
    )(*args)


def _outproj_kernel(seq_ref, *refs, n_x, n_first):
    x_refs, (oa_ref, ob_ref, ga_ref, gb_ref, mod_ref, woa_ref, wob_ref, wout_ref, nw_ref,
             wr_ref, br_ref, x1_ref, h2_ref, wexp_ref) = refs[:n_x], refs[n_x:]
    i = pl.program_id(0)
    x = _load_x(i, x_refs, n_first)
    tm, d = x.shape
    nch = tm // CHUNK
    ya = _dot(oa_ref[...], woa_ref[...])
    yb = _dot(ob_ref[...], wob_ref[...])
    merged = _sigmoid(ga_ref[...]) * ya + _sigmoid(gb_ref[...]) * yb
    mix = _dot(merged.astype(BF16), wout_ref[...])
    g1 = _chunk_rows(seq_ref, mod_ref, i, nch, 2, d)
    x1 = x + _per_chunk(mix, lambda c, r: g1[c] * r)
    x1_ref[...] = x1
    xn = x1 * lax.rsqrt(jnp.mean(x1 * x1, axis=-1, keepdims=True) + EPS) * nw_ref[...]
    shift = _chunk_rows(seq_ref, mod_ref, i, nch, 3, d)
    scale = _chunk_rows(seq_ref, mod_ref, i, nch, 4, d)
    h2 = _per_chunk(xn, lambda c, r: r * (1.0 + scale[c]) + shift[c])
    h2_hi = h2.astype(BF16)
    h2_ref[...] = h2_hi

    h2_lo = (h2 - h2_hi.astype(F32)).astype(BF16)
    wr = wr_ref[...]
    hi = _dot(h2_hi, wr)
    logits = hi[:, :LANES] + hi[:, LANES:] + _dot(h2_lo, wr[:, :LANES]) + br_ref[...]
    lane = lax.broadcasted_iota(jnp.int32, logits.shape, 1)
    neg = -jnp.inf
    lg = jnp.where(lane < N_GROUPS, logits, neg)
    mx = jnp.max(lg, axis=-1, keepdims=True)
    gi = jnp.min(jnp.where(lg == mx, lane, LANES), axis=-1, keepdims=True)
    p_group = 1.0 / jnp.sum(jnp.exp(lg - mx), axis=-1, keepdims=True)
    lo = N_GROUPS + EXPERTS_PER_GROUP * gi
    le = jnp.where((lane >= lo) & (lane < lo + EXPERTS_PER_GROUP), logits, neg)
    v1 = jnp.max(le, axis=-1, keepdims=True)
    i1 = jnp.min(jnp.where(le == v1, lane, LANES), axis=-1, keepdims=True)
    le2 = jnp.where(lane == i1, neg, le)
    v2 = jnp.max(le2, axis=-1, keepdims=True)
    i2 = jnp.min(jnp.where(le2 == v2, lane, LANES), axis=-1, keepdims=True)
    e2 = jnp.exp(v2 - v1)
    w1 = p_group / (1.0 + e2)
    w2 = p_group * e2 / (1.0 + e2)
    wexp_ref[...] = jnp.where(lane == i1, w1, 0.0) + jnp.where(lane == i2, w2, 0.0)


def _outproj_call(seq, xs, oa, ob, ga, gb, mod, woa, wob, wout, norm_w, wr, br, tm):
    t = sum(x.shape[0] for x in xs)
    d = xs[0].shape[1]
    tok = lambda w: pl.BlockSpec((tm, w), lambda i, s: (i, 0))
    return pl.pallas_call(
        functools.partial(_outproj_kernel, n_x=len(xs), n_first=xs[0].shape[0] // tm),
        out_shape=[jax.ShapeDtypeStruct((t, d), F32), jax.ShapeDtypeStruct((t, d), BF16),
                   jax.ShapeDtypeStruct((t, LANES), F32)],
        grid_spec=pltpu.PrefetchScalarGridSpec(
            num_scalar_prefetch=1,
            grid=(t // tm,),
            in_specs=_x_specs(xs, tm) + [
                tok(oa.shape[1]), tok(ob.shape[1]), tok(d), tok(d),
                _const_spec(mod.shape),
                _const_spec(woa.shape), _const_spec(wob.shape), _const_spec(wout.shape),
                _const_spec((1, d)), _const_spec(wr.shape), _const_spec((1, LANES)),
            ],
            out_specs=[tok(d), tok(d), tok(LANES)],
        ),
        compiler_params=pltpu.CompilerParams(vmem_limit_bytes=VMEM_LIMIT),
        name="outproj",
    )(seq, *xs, oa, ob, ga, gb, mod, woa, wob, wout, norm_w, wr, br)


def _moe_kernel(seq_ref, h_ref, wexp_ref, x1_ref, mod_ref, wg_ref, wu_ref, wd_ref, *rest, n_first):
    acc_ref = rest[-1]
    i = pl.program_id(0)
    e = pl.program_id(1)

    @pl.when(e == 0)
    def _():
        acc_ref[...] = jnp.zeros_like(acc_ref)

    h = h_ref[...]
    wexp = wexp_ref[...]
    lane = lax.broadcasted_iota(jnp.int32, wexp.shape, 1)
    epb = wg_ref.shape[0]
    experts = range(epb)
    hg = [_dot(h, wg_ref[j]) for j in experts]
    hu = [_dot(h, wu_ref[j]) for j in experts]
    we = [jnp.sum(jnp.where(lane == N_GROUPS + e * epb + j, wexp, 0.0), axis=-1, keepdims=True) for j in experts]
    act = [(_silu(hg[j]) * hu[j] * we[j]).astype(BF16) for j in experts]
    out = _dot(act[0], wd_ref[0])
    for j in experts[1:]:
        out = out + _dot(act[j], wd_ref[j])
    acc_ref[...] += out

    @pl.when(e == pl.num_programs(1) - 1)
    def _():
        tm, d = acc_ref.shape
        nch = tm // CHUNK
        g2 = _chunk_rows(seq_ref, mod_ref, i, nch, 5, d)
        x2 = x1_ref[...] + _per_chunk(acc_ref[...], lambda c, r: g2[c] * r)
        if len(rest) == 2:
            rest[0][...] = x2
        else:
            nw_ref, yp_ref, ys_ref = rest[:3]
            y = x2 * lax.rsqrt(jnp.mean(x2 * x2, axis=-1, keepdims=True) + EPS) * nw_ref[...]

            @pl.when(i < n_first)
            def _():
                yp_ref[...] = y

            @pl.when(i >= n_first)
            def _():
                ys_ref[...] = y


def _moe_call(seq, h2, wexp, x1, mod, wg, wu, wd, tm, final=None):
    t, d = x1.shape
    ne, _, de = wg.shape
    epb = MOE_EXPERTS_PER_STEP
    tok = lambda w: pl.BlockSpec((tm, w), lambda i, e, s: (i, 0))
    in_specs = [
        tok(d), tok(LANES), tok(d),
        _const_spec(mod.shape),
        pl.BlockSpec((epb, d, de), lambda i, e, s: (e, 0, 0)),
        pl.BlockSpec((epb, d, de), lambda i, e, s: (e, 0, 0)),
        pl.BlockSpec((epb, de, d), lambda i, e, s: (e, 0, 0)),
    ]
    args = [seq, h2, wexp, x1, mod, wg, wu, wd]
    if final is None:
        n_first = 0
        out_shape = jax.ShapeDtypeStruct((t, d), F32)
        out_specs = tok(d)
    else:
        norm_w, tp = final
        assert tp % tm == 0 and (t - tp) % tm == 0
        n_first = tp // tm
        in_specs.append(_const_spec((1, d)))
        args.append(norm_w)
        out_shape = [jax.ShapeDtypeStruct((tp, d), F32), jax.ShapeDtypeStruct((t - tp, d), F32)]
        out_specs = [pl.BlockSpec((tm, d), lambda i, e, s: (jnp.minimum(i, n_first - 1), 0)),
                     pl.BlockSpec((tm, d), lambda i, e, s: (jnp.maximum(i - n_first, 0), 0))]
    return pl.pallas_call(
        functools.partial(_moe_kernel, n_first=n_first),
        out_shape=out_shape,
        grid_spec=pltpu.PrefetchScalarGridSpec(
            num_scalar_prefetch=1,
            grid=(t // tm, ne // epb),
            in_specs=in_specs,
            out_specs=out_specs,
            scratch_shapes=[pltpu.VMEM((tm, d), F32)],
        ),
        compiler_params=pltpu.CompilerParams(dimension_semantics=("arbitrary", "arbitrary"),
                                             vmem_limit_bytes=VMEM_LIMIT),
        name="moe",
    )(*args)


def _rope_table(pos):
    half = ROPE_DIM // 2
    inv = ROPE_THETA ** (-jnp.arange(half, dtype=F32) / half)
    ang = pos.astype(F32)[:, None] * inv[None, :]
    cos, sin = jnp.cos(ang), jnp.sin(ang)
    t = pos.shape[0]
    ones = jnp.ones((t, HD_A - ROPE_DIM), F32)
    zeros = jnp.zeros((t, HD_A - ROPE_DIM), F32)
    zh = jnp.zeros((t, half), F32)
    c_head = jnp.concatenate([cos, cos, ones], axis=1)
    s_lo = jnp.concatenate([zh, sin, zeros], axis=1)
    s_hi = jnp.concatenate([-sin, zh, zeros], axis=1)
    rep = LANES // HD_A
    return jnp.concatenate([jnp.tile(c_head, (1, rep)), jnp.tile(s_lo, (1, rep)), jnp.tile(s_hi, (1, rep))], axis=1)


def kernel(x_prompt, x_sample, cache_k_a, cache_v_a, state_conv_b, state_ssm_b, c_prompt, c_sample, w_mod, b_mod, norm_mix, w_in, sinks_a, w_o_a, conv_b, a_log_b, dt_bias_b, norm_o_b, w_o_b, w_out, norm_ffn, router_g, router_g_b, router_e, router_e_b, w_gate_e, w_up_e, w_down_e, norm_final):
    bp, seq, d = x_prompt.shape
    bs, dseq, _ = x_sample.shape
    depth = w_mod.shape[0]
    assert seq % CHUNK == 0 and dseq == CHUNK and d % LANES == 0
    assert cache_k_a.shape[2] == WIN_CHUNKS * CHUNK
    npc = seq // CHUNK
    tp, ts = bp * seq, bs * dseq
    t = tp + ts
    n_chunks = t // CHUNK
    nseq = bp + bs
    kvw = N_KV_A * HD_A
    cd = 2 * N_QK_B * DK_B + N_V_B * DV_B
    vw = N_V_B * DV_B
    nq = N_HEADS_A * HD_A

    seq_np = np.concatenate([np.repeat(np.arange(bp), npc), bp + np.arange(bs)]).astype(np.int32)
    local_np = np.concatenate([np.tile(np.arange(npc), bp), np.zeros(bs, np.int64)])
    first_np = (local_np == 0).astype(np.int32)
    last_np = np.concatenate([np.tile(np.arange(npc) == npc - 1, bp), np.ones(bs, bool)]).astype(np.int32)
    nvalid_np = np.concatenate([np.minimum(np.tile(np.arange(npc), bp), WIN_CHUNKS),
                                np.full(bs, WIN_CHUNKS)]).astype(np.int32)
    seq_i = jnp.asarray(seq_np)
    first_p, last_p = jnp.asarray(first_np[:bp * npc]), jnp.asarray(last_np[:bp * npc])
    nvalid_p, nvalid_s = jnp.asarray(nvalid_np[:bp * npc]), jnp.asarray(nvalid_np[bp * npc:])
    cb = 4 if npc % 4 == 0 else 2
    assert npc % cb == 0
    cbg = GDN_CHUNKS_PER_STEP if npc % GDN_CHUNKS_PER_STEP == 0 else cb

    pos = jnp.concatenate([jnp.tile(jnp.arange(seq), bp), jnp.tile(PAST_LEN + jnp.arange(dseq), bs)])
    cs = _rope_table(pos)

    xs = [x_prompt.reshape(tp, d), x_sample.reshape(ts, d)]
    c_all = jnp.concatenate([c_prompt, c_sample], axis=0)
    c_rows = -(-nseq // 8) * 8
    c_all = jnp.pad(c_all, ((0, c_rows - nseq), (0, 0)))
    mod = _mod_call(c_all, w_mod, b_mod)

    tm = _pick_tile(np.gcd(tp, ts), 256)
    tm_out = _pick_tile(np.gcd(tp, ts), 512)
    tm_moe = _pick_tile(np.gcd(tp, ts), 1024)
    sizes = np.cumsum([0, nq, kvw, kvw, cd, vw, N_V_B, N_V_B, d, d])
    outs = {k: [] for k in ("k_p", "v_p", "c_p", "s_p", "k_s", "v_s", "c_s", "s_s")}
    for l in range(depth):
        wl = w_in[l]
        seg = lambda a, b: wl[:, sizes[a]:sizes[b]].astype(BF16)
        wqkv, wxb, wz = seg(0, 3), seg(3, 4), seg(4, 5)
        wba = jnp.pad(seg(5, 7), ((0, 0), (0, LANES - 2 * N_V_B)))
        wga, wgb = seg(7, 8), seg(8, 9)
        q, k, v, xb, z, ba, ga, gb = _inproj_call(
            seq_i, xs, mod[l], norm_mix[l].reshape(1, d), cs, wqkv, wxb, wz, wba, wga, wgb, tm)

        oa = _attn_call(nvalid_p, sinks_a[l], q, k, k, v, v, cb, tp // (cb * CHUNK), 0,
                        lambda i: jnp.maximum(i * (cb // WIN_CHUNKS) - 1, 0))
        oa = _attn_call(nvalid_s, sinks_a[l], q, cache_k_a[l].reshape(bs * WIN_CHUNKS * CHUNK, kvw), k,
                        cache_v_a[l].reshape(bs * WIN_CHUNKS * CHUNK, kvw), v, 1, bs, tp // CHUNK, lambda i: i,
                        out_full=oa)

        apar = jnp.zeros((2, LANES), F32)
        apar = apar.at[0, N_V_B:2 * N_V_B].set(a_log_b[l]).at[1, N_V_B:2 * N_V_B].set(dt_bias_b[l])
        nw_o = norm_o_b[l].reshape(1, DV_B)
        ob, ssm_p = _gdn_call(first_p, last_p, xb, jnp.zeros((bp, 8, cd), F32),
                              jnp.zeros((bp, N_V_B, DK_B, DV_B), F32), conv_b[l], ba, apar, z, nw_o,
                              cbg, bp * npc // cbg, 0, npc // cbg)
        conv_init = jnp.pad(state_conv_b[l], ((0, 0), (8 - (CONV_W - 1), 0), (0, 0)))
        ones_s = jnp.ones((bs,), jnp.int32)
        ob, ssm_s = _gdn_call(ones_s, ones_s, xb, conv_init, state_ssm_b[l], conv_b[l], ba, apar, z, nw_o,
                              1, bs, tp // CHUNK, 1, out_full=ob)

        wr = jnp.concatenate([router_g[l], jnp.transpose(router_e[l], (1, 0, 2)).reshape(d, N_EXPERTS)], axis=1)
        wr = jnp.pad(wr, ((0, 0), (0, LANES - wr.shape[1])))
        wr_hi = wr.astype(BF16)
        wr = jnp.concatenate([wr_hi, (wr - wr_hi.astype(F32)).astype(BF16)], axis=1)
        br = jnp.concatenate([router_g_b[l], router_e_b[l].reshape(-1)])
        br = jnp.pad(br, (0, LANES - br.shape[0])).reshape(1, LANES)
        x1, h2, wexp = _outproj_call(seq_i, xs, oa, ob, ga, gb, mod[l], w_o_a[l].astype(BF16),
                                     w_o_b[l].astype(BF16), w_out[l].astype(BF16), norm_ffn[l].reshape(1, d),
                                     wr, br, tm_out)
        final =(norm_final.reshape(1, d), tp) if l == depth - 1 else None
        x_next = _moe_call(seq_i, h2, wexp, x1, mod[l], w_gate_e[l].astype(BF16), w_up_e[l].astype(BF16),
                           w_down_e[l].astype(BF16), tm_moe, final)
        xs = [x_next] if final is None else x_next

        keep = min(WIN_CHUNKS * CHUNK, seq)
        ends = [(b + 1) * seq for b in range(bp)]
        tail = lambda a, n: jnp.stack([a[e - n:e] for e in ends])
        outs["k_p"].append(tail(k, keep).reshape(bp, keep, N_KV_A, HD_A))
        outs["v_p"].append(tail(v, keep).reshape(bp, keep, N_KV_A, HD_A))
        outs["k_s"].append(k[tp:].reshape(bs, dseq, N_KV_A, HD_A))
        outs["v_s"].append(v[tp:].reshape(bs, dseq, N_KV_A, HD_A))
        outs["c_p"].append(tail(xb, CONV_W - 1))
        outs["c_s"].append(xb[tp:].reshape(bs, dseq, cd)[:, dseq - (CONV_W - 1):])
        outs["s_p"].append(ssm_p)
        outs["s_s"].append(ssm_s)

    st = lambda key: jnp.stack(outs[key])
    return (xs[0].reshape(bp, seq, d), xs[1].reshape(bs, dseq, d),
            st("k_p"), st("v_p"), st("c_p"), st("s_p"), st("k_s"), st("v_s"), st("c_s"), st("s_s"))
```

```python
import functools

import numpy as np
import jax
import jax.numpy as jnp
from jax import lax
from jax.experimental import pallas as pl
from jax.experimental.pallas import tpu as pltpu

CHUNK = 64
PAST_LEN = 1024
N_HEADS_A = 8
N_KV_A = 2
HD_A = 64
WIN_CHUNKS = 2
ROPE_DIM = 16
ROPE_THETA = 500000.0
N_QK_B = 4
N_V_B = 8
DK_B = 128
DV_B = 128
CONV_W = 4
N_GROUPS = 4
EXPERTS_PER_GROUP = 4
N_EXPERTS = 16
D_EXPERT = 256
EPS = 1e-6
LANES = 128
VMEM_LIMIT = 56 * 1024 * 1024
MOE_EXPERTS_PER_STEP = 2
GDN_CHUNKS_PER_STEP = 8
GDN_CHUNKS_PER_ITER = 2

F32 = jnp.float32
BF16 = jnp.bfloat16
HIGHEST = lax.Precision.HIGHEST


def _pick_tile(total, pref):
    t = pref
    while total % t:
        t //= 2
    assert t >= CHUNK
    return t


def _const_spec(shape):
    nd = len(shape)
    return pl.BlockSpec(shape, lambda *_: (0,) * nd)


def _silu(x):
    return x * (1.0 / (1.0 + jnp.exp(-x)))


def _sigmoid(x):
    return 1.0 / (1.0 + jnp.exp(-x))


def _dot(a, b):
    return jnp.dot(a, b, preferred_element_type=F32)


def _dot_nt(a, b):
    return lax.dot_general(a, b, (((1,), (1,)), ((), ())), preferred_element_type=F32)


def _dot_tn(a, b):
    return lax.dot_general(a, b, (((0,), (0,)), ((), ())), preferred_element_type=F32)


def _dot_hi(a, b):
    return jnp.dot(a, b, preferred_element_type=F32, precision=HIGHEST)


def _mod_kernel(c_ref, w_ref, b_ref, o_ref):
    o_ref[...] = _dot_hi(_silu(c_ref[...]), w_ref[...]) + b_ref[...]


def _mod_call(c_all, w_mod, b_mod):
    depth, d, d6 = w_mod.shape
    rows = c_all.shape[0]
    tn = 1024
    return pl.pallas_call(
        _mod_kernel,
        out_shape=jax.ShapeDtypeStruct((depth, rows, d6), F32),
        grid=(depth, d6 // tn),
        in_specs=[
            pl.BlockSpec((rows, d), lambda l, j: (0, 0)),
            pl.BlockSpec((None, d, tn), lambda l, j: (l, 0, j)),
            pl.BlockSpec((None, 1, tn), lambda l, j: (l, 0, j)),
        ],
        out_specs=pl.BlockSpec((None, rows, tn), lambda l, j: (l, 0, j)),
        compiler_params=pltpu.CompilerParams(vmem_limit_bytes=VMEM_LIMIT),
        name="mod",
    )(c_all, w_mod, b_mod.reshape(depth, 1, d6))


def _rope(x, cs):
    return (x * cs[:, :LANES]
            + pltpu.roll(x, 8, axis=1) * cs[:, LANES:2 * LANES]
            + pltpu.roll(x, LANES - 8, axis=1) * cs[:, 2 * LANES:])


def _x_specs(xs, tm):
    d = xs[0].shape[1]
    if len(xs) == 1:
        return [pl.BlockSpec((tm, d), lambda i, *_: (i, 0))]
    n_first = xs[0].shape[0] // tm
    assert xs[0].shape[0] % tm == 0 and xs[1].shape[0] % tm == 0
    return [pl.BlockSpec((tm, d), lambda i, *_: (jnp.minimum(i, n_first - 1), 0)),
            pl.BlockSpec((tm, d), lambda i, *_: (jnp.maximum(i - n_first, 0), 0))]


def _load_x(i, x_refs, n_first):
    if len(x_refs) == 1:
        return x_refs[0][...]
    return jnp.where(i < n_first, x_refs[0][...], x_refs[1][...])


def _chunk_rows(seq_ref, mod_ref, i, nch, col, d):
    return [mod_ref[pl.ds(seq_ref[i * nch + c], 1), col * d:(col + 1) * d] for c in range(nch)]


def _per_chunk(x, fn):
    nch = x.shape[0] // CHUNK
    return jnp.concatenate([fn(c, x[c * CHUNK:(c + 1) * CHUNK]) for c in range(nch)], axis=0)


def _inproj_kernel(seq_ref, *refs, n_x, n_first):
    x_refs, (mod_ref, nw_ref, cs_ref, wqkv_ref, wxb_ref, wz_ref, wba_ref, wga_ref, wgb_ref,
             q_ref, k_ref, v_ref, xb_ref, z_ref, ba_ref, ga_ref, gb_ref) = refs[:n_x], refs[n_x:]
    i = pl.program_id(0)
    x = _load_x(i, x_refs, n_first)
    tm, d = x.shape
    nch = tm // CHUNK
    xn = x * lax.rsqrt(jnp.mean(x * x, axis=-1, keepdims=True) + EPS) * nw_ref[...]
    shift = _chunk_rows(seq_ref, mod_ref, i, nch, 0, d)
    scale = _chunk_rows(seq_ref, mod_ref, i, nch, 1, d)
    h = _per_chunk(xn, lambda c, r: r * (1.0 + scale[c]) + shift[c]).astype(BF16)
    cs = cs_ref[...]
    qkv = _dot(h, wqkv_ref[...])
    nq = N_HEADS_A * HD_A
    for g in range(nq // LANES):
        q_ref[:, g * LANES:(g + 1) * LANES] = _rope(qkv[:, g * LANES:(g + 1) * LANES], cs).astype(BF16)
    k_ref[...] = _rope(qkv[:, nq:nq + LANES], cs)
    v_ref[...] = qkv[:, nq + LANES:]
    xb_ref[...] = _dot(h, wxb_ref[...])
    z_ref[...] = _dot(h, wz_ref[...])
    ba_ref[...] = _dot(h, wba_ref[...])
    ga_ref[...] = _dot(h, wga_ref[...])
    gb_ref[...] = _dot(h, wgb_ref[...])


def _inproj_call(seq, xs, mod, norm_w, cs, wqkv, wxb, wz, wba, wga, wgb, tm):
    t = sum(x.shape[0] for x in xs)
    d = xs[0].shape[1]
    widths = [N_HEADS_A * HD_A, LANES, LANES, wxb.shape[1], wz.shape[1], LANES, d, d]
    dtypes = [BF16, F32, F32, F32, F32, F32, F32, F32]
    tok = lambda w: pl.BlockSpec((tm, w), lambda i, s: (i, 0))
    return pl.pallas_call(
        functools.partial(_inproj_kernel, n_x=len(xs), n_first=xs[0].shape[0] // tm),
        out_shape=[jax.ShapeDtypeStruct((t, w), dt) for w, dt in zip(widths, dtypes)],
        grid_spec=pltpu.PrefetchScalarGridSpec(
            num_scalar_prefetch=1,
            grid=(t // tm,),
            in_specs=_x_specs(xs, tm) + [
                _const_spec(mod.shape),
                _const_spec((1, d)),
                tok(3 * LANES),
                _const_spec(wqkv.shape), _const_spec(wxb.shape), _const_spec(wz.shape),
                _const_spec(wba.shape), _const_spec(wga.shape), _const_spec(wgb.shape),
            ],
            out_specs=[tok(w) for w in widths],
        ),
        compiler_params=pltpu.CompilerParams(vmem_limit_bytes=VMEM_LIMIT),
        name="inproj",
    )(seq, *xs, mod, norm_w, cs, wqkv, wxb, wz, wba, wga, wgb)


def _attn_kernel(nvalid_ref, sink_ref, q_ref, kp_ref, kt_ref, vp_ref, vt_ref, *rest, cb):
    o_ref = rest[-1]
    i = pl.program_id(0)
    kcat = jnp.concatenate([kp_ref[...], kt_ref[...]], axis=0).astype(BF16)
    vcat = jnp.concatenate([vp_ref[...], vt_ref[...]], axis=0).astype(BF16)
    nk = (WIN_CHUNKS + 1) * CHUNK
    key_chunk = lax.broadcasted_iota(jnp.int32, (CHUNK, nk), 1) // CHUNK
    group = N_HEADS_A // N_KV_A
    heads = range(N_HEADS_A)

    def scores(c):
        q = q_ref[c * CHUNK:(c + 1) * CHUNK, :]
        return [_dot_nt(q[:, h * HD_A:(h + 1) * HD_A],
                        kcat[c * CHUNK:c * CHUNK + nk, (h // group) * HD_A:(h // group + 1) * HD_A])
                for h in heads]

    s_next = scores(0)
    for c in range(cb):
        s_cur = s_next
        if c + 1 < cb:
            s_next = scores(c + 1)
        valid = key_chunk >= (WIN_CHUNKS - nvalid_ref[i * cb + c])
        p, den = [], []
        for h in heads:
            s = jnp.where(valid, s_cur[h] * (HD_A ** -0.5), -jnp.inf)
            sink = sink_ref[h]
            m = jnp.maximum(jnp.max(s, axis=-1, keepdims=True), sink)
            e = jnp.exp(s - m)
            p.append(e.astype(BF16))
            den.append(jnp.sum(e, axis=-1, keepdims=True) + jnp.exp(sink - m))
        o = [_dot(p[h], vcat[c * CHUNK:c * CHUNK + nk, (h // group) * HD_A:(h // group + 1) * HD_A])
             for h in heads]
        for h in heads:
            o_ref[c * CHUNK:(c + 1) * CHUNK, h * HD_A:(h + 1) * HD_A] = (o[h] / den[h]).astype(BF16)


def _attn_call(nvalid, sinks, q, k_prev, k_new, v_prev, v_new, cb, n_tiles, tile_off, prev_map, out_full=None):
    kvw = N_KV_A * HD_A
    qw = N_HEADS_A * HD_A
    tile = lambda w: pl.BlockSpec((cb * CHUNK, w), lambda i, nv: (i + tile_off, 0))
    prev = pl.BlockSpec((WIN_CHUNKS * CHUNK, kvw), lambda i, nv: (prev_map(i), 0))
    args = [nvalid, sinks, q, k_prev, k_new, v_prev, v_new]
    in_specs = [pl.BlockSpec(memory_space=pltpu.SMEM), tile(qw), prev, tile(kvw), prev, tile(kvw)]
    aliases = {}
    if out_full is not None:
        aliases = {len(args): 0}
        args.append(out_full)
        in_specs.append(pl.BlockSpec(memory_space=pl.ANY))
    return pl.pallas_call(
        functools.partial(_attn_kernel, cb=cb),
        out_shape=jax.ShapeDtypeStruct((q.shape[0], qw), BF16),
        grid_spec=pltpu.PrefetchScalarGridSpec(
            num_scalar_prefetch=1,
            grid=(n_tiles,),
            in_specs=in_specs,
            out_specs=tile(qw),
        ),
        input_output_aliases=aliases,
        compiler_params=pltpu.CompilerParams(vmem_limit_bytes=VMEM_LIMIT),
        name="attn",
    )(*args)


def _l2norm(x):
    return x * lax.rsqrt(jnp.sum(x * x, axis=-1, keepdims=True) + EPS)


def _gdn_kernel(first_ref, last_ref, xb_ref, cinit_ref, sinit_ref, cw_ref, ba_ref, apar_ref, z_ref, nw_ref,
                *rest, cb):
    ob_ref, sout_ref, pad_ref, state_ref = rest[-4:]
    step = pl.program_id(0)

    nc = GDN_CHUNKS_PER_ITER if cb % GDN_CHUNKS_PER_ITER == 0 else 1

    def body(it, carry):
        _gdn_chunks(step * cb + it * nc, pl.multiple_of(it * (nc * CHUNK), nc * CHUNK), nc, first_ref, last_ref,
                    xb_ref, cinit_ref, sinit_ref, cw_ref, ba_ref, apar_ref, z_ref, nw_ref,
                    ob_ref, sout_ref, pad_ref, state_ref)
        return carry

    lax.fori_loop(0, cb // nc, body, 0)


def _gdn_chunks(i, row0, nc, first_ref, last_ref, xb_ref, cinit_ref, sinit_ref, cw_ref, ba_ref, apar_ref, z_ref,
                nw_ref, ob_ref, sout_ref, pad_ref, state_ref):
    c = CHUNK
    n = nc * c
    rows = pl.ds(row0, n)

    @pl.when(first_ref[i] == 1)
    def _():
        pad_ref[0:8, :] = cinit_ref[...]
        state_ref[...] = sinit_ref[...]

    pad_ref[8:8 + n, :] = xb_ref[rows, :]
    cw = cw_ref[...]
    conv = pad_ref[5:5 + n, :] * cw[0:1, :]
    for j in range(1, CONV_W):
        conv = conv + pad_ref[5 + j:5 + j + n, :] * cw[j:j + 1, :]
    pad_ref[0:8, :] = pad_ref[n:n + 8, :]
    qkv = _silu(conv)
    nqk = N_QK_B * DK_B

    ba = ba_ref[rows, :]
    apar = apar_ref[...]
    beta = _sigmoid(ba[:, 0:N_V_B])
    sp_in = ba + apar[1:2, :]
    softplus = jnp.maximum(sp_in, 0.0) + jnp.log(1.0 + jnp.exp(-jnp.abs(sp_in)))
    g_all = -jnp.exp(apar[0:1, :]) * softplus
    row = lax.broadcasted_iota(jnp.int32, (c, c), 0)
    col = lax.broadcasted_iota(jnp.int32, (c, c), 1)
    lower = (row >= col).astype(BF16)
    g1 = g_all.astype(BF16)
    r1 = g_all - g1.astype(F32)
    g2 = r1.astype(BF16)
    g3 = (r1 - g2.astype(F32)).astype(BF16)
    gsplit = jnp.concatenate([g1, g2, g3], axis=1)
    gc_all, gc_t = [], []
    for k in range(nc):
        gs = _dot(lower, gsplit[k * c:(k + 1) * c])
        gck = gs[:, :LANES] + gs[:, LANES:2 * LANES] + gs[:, 2 * LANES:]
        gc_all.append(gck)
        gc_t.append(gck.T)

    qn_all = [_l2norm(qkv[:, j * DK_B:(j + 1) * DK_B]) * (DK_B ** -0.5) for j in range(N_QK_B)]
    kn_all = [_l2norm(qkv[:, nqk + j * DK_B:nqk + (j + 1) * DK_B]) for j in range(N_QK_B)]
    pairs = [(k, j) for k in range(nc) for j in range(N_QK_B)]
    qn = {kj: qn_all[kj[1]][kj[0] * c:(kj[0] + 1) * c] for kj in pairs}
    kn = {kj: kn_all[kj[1]][kj[0] * c:(kj[0] + 1) * c] for kj in pairs}
    kt = {kj: kn[kj].T for kj in pairs}
    qkk = {kj: _dot(jnp.concatenate([qn[kj], kn[kj]], axis=0).astype(BF16), kt[kj].astype(BF16))
           for kj in pairs}

    rep = N_V_B // N_QK_B
    items = [(k, h) for k in range(nc) for h in range(N_V_B)]
    qk_of = lambda kh: (kh[0], kh[1] // rep)
    chunk_rows = lambda kh, x: x[kh[0] * c:(kh[0] + 1) * c]
    gcol = {kh: gc_all[kh[0]][:, N_V_B + kh[1]:N_V_B + kh[1] + 1] for kh in items}
    grow = {kh: gc_t[kh[0]][N_V_B + kh[1]:N_V_B + kh[1] + 1, :] for kh in items}
    bcol = {kh: chunk_rows(kh, beta)[:, kh[1]:kh[1] + 1] for kh in items}
    decay = {kh: jnp.exp(jnp.where(row >= col, gcol[kh] - grow[kh], -jnp.inf)) for kh in items}
    a = {kh: jnp.where(row > col, bcol[kh] * qkk[qk_of(kh)][c:] * decay[kh], 0.0) for kh in items}
    xm = {kh: -a[kh] for kh in items}
    p = {kh: _dot(a[kh].astype(BF16), a[kh].astype(BF16)) for kh in items}
    for _ in range(4):
        r = {kh: _dot(jnp.concatenate([xm[kh], p[kh]], axis=0).astype(BF16), p[kh].astype(BF16)) for kh in items}
        xm = {kh: xm[kh] + p[kh] + r[kh][:c] for kh in items}
        p = {kh: r[kh][c:] for kh in items}
    r = {kh: _dot(xm[kh].astype(BF16), p[kh].astype(BF16)) for kh in items}
    xm = {kh: xm[kh] + p[kh] + r[kh] for kh in items}
    egc = {kh: jnp.exp(gcol[kh]) for kh in items}
    rhs = {kh: jnp.concatenate(
        [chunk_rows(kh, qkv[:, 2 * nqk + kh[1] * DV_B:2 * nqk + (kh[1] + 1) * DV_B]) * bcol[kh],
         kn[qk_of(kh)] * (bcol[kh] * egc[kh])], axis=1) for kh in items}
    sol = {kh: rhs[kh] + _dot(xm[kh].astype(BF16), rhs[kh].astype(BF16)) for kh in items}
    wq = {kh: jnp.concatenate([sol[kh][:, DV_B:], qn[qk_of(kh)] * egc[kh]], axis=0).astype(BF16) for kh in items}
    a_qk = {kh: (qkk[qk_of(kh)][:c] * decay[kh]).astype(BF16) for kh in items}
    g_last = {kh: gcol[kh][c - 1:c, :] for kh in items}
    k_dec_t = {kh: (kt[qk_of(kh)] * jnp.exp(g_last[kh] - grow[kh])).astype(BF16) for kh in items}

    heads = range(N_V_B)
    state = [state_ref[h] for h in heads]
    nw = nw_ref[...]
    for k in range(nc):
        r = [_dot(wq[k, h], state[h].astype(BF16)) for h in heads]
        v_new = [(sol[k, h][:, :DV_B] - r[h][:c]).astype(BF16) for h in heads]
        state = [state[h] * jnp.exp(g_last[k, h]) + _dot(k_dec_t[k, h], v_new[h]) for h in heads]
        o = [r[h][c:] + _dot(a_qk[k, h], v_new[h]) for h in heads]
        out_rows = pl.ds(pl.multiple_of(row0 + k * c, c), c)
        for h in heads:
            on = o[h] * lax.rsqrt(jnp.mean(o[h] * o[h], axis=-1, keepdims=True) + EPS) * nw
            zh = z_ref[out_rows, h * DV_B:(h + 1) * DV_B]
            ob_ref[out_rows, h * DV_B:(h + 1) * DV_B] = (on * _silu(zh)).astype(BF16)
    for h in heads:
        state_ref[h] = state[h]

    @pl.when(last_ref[i + nc - 1] == 1)
    def _():
        sout_ref[...] = state_ref[...]


def _gdn_call(first, last, xb, conv_init, ssm_init, conv_w, ba, apar, z, norm_o, cb, n_steps, tile_off,
              steps_per_seq, out_full=None):
    t, cd = xb.shape
    nseq = conv_init.shape[0]
    vw = N_V_B * DV_B
    tile = lambda w: pl.BlockSpec((cb * CHUNK, w), lambda i, f, l: (i + tile_off, 0))
    const = lambda shape: pl.BlockSpec(shape, lambda i, f, l: (0,) * len(shape))
    state = pl.BlockSpec((None, N_V_B, DK_B, DV_B), lambda i, f, l: (i // steps_per_seq, 0, 0, 0))
    args = [first, last, xb, conv_init, ssm_init, conv_w, ba, apar, z, norm_o]
    in_specs = [
        tile(cd),
        pl.BlockSpec((None, 8, cd), lambda i, f, l: (i // steps_per_seq, 0, 0)),
        state,
        const((CONV_W, cd)),
        tile(LANES),
        const((2, LANES)),
        tile(vw),
        const((1, DV_B)),
    ]
    aliases = {}
    if out_full is not None:
        aliases = {len(args): 0}
        args.append(out_full)
        in_specs.append(pl.BlockSpec(memory_space=pl.ANY))
    return pl.pallas_call(
        functools.partial(_gdn_kernel, cb=cb),
        out_shape=[jax.ShapeDtypeStruct((t, vw), BF16),
                   jax.ShapeDtypeStruct((nseq, N_V_B, DK_B, DV_B), F32)],
        grid_spec=pltpu.PrefetchScalarGridSpec(
            num_scalar_prefetch=2,
            grid=(n_steps,),
            in_specs=in_specs,
            out_specs=[tile(vw), state],
            scratch_shapes=[pltpu.VMEM((GDN_CHUNKS_PER_ITER * CHUNK + 8, cd), F32),
                            pltpu.VMEM((N_V_B, DK_B, DV_B), F32)],
        ),
        input_output_aliases=aliases,
        compiler_params=pltpu.CompilerParams(dimension_semantics=("arbitrary",),
                                             vmem_limit_bytes=VMEM_LIMIT),
        name="gdn",
    )(*args)


def _outproj_kernel(seq_ref, *refs, n_x, n_first):
    x_refs, (oa_ref, ob_ref, ga_ref, gb_ref, mod_ref, woa_ref, wob_ref, wout_ref, nw_ref,
             wr_ref, br_ref, x1_ref, h2_ref, wexp_ref) = refs[:n_x], refs[n_x:]
    i = pl.program_id(0)
    x = _load_x(i, x_refs, n_first)
    tm, d = x.shape
    nch = tm // CHUNK
    ya = _dot(oa_ref[...], woa_ref[...])
    yb = _dot(ob_ref[...], wob_ref[...])
    merged = _sigmoid(ga_ref[...]) * ya + _sigmoid(gb_ref[...]) * yb
    mix = _dot(merged.astype(BF16), wout_ref[...])
    g1 = _chunk_rows(seq_ref, mod_ref, i, nch, 2, d)
    x1 = x + _per_chunk(mix, lambda c, r: g1[c] * r)
    x1_ref[...] = x1
    xn = x1 * lax.rsqrt(jnp.mean(x1 * x1, axis=-1, keepdims=True) + EPS) * nw_ref[...]
    shift = _chunk_rows(seq_ref, mod_ref, i, nch, 3, d)
    scale = _chunk_rows(seq_ref, mod_ref, i, nch, 4, d)
    h2 = _per_chunk(xn, lambda c, r: r * (1.0 + scale[c]) + shift[c])
    h2_hi = h2.astype(BF16)
    h2_ref[...] = h2_hi

    h2_lo = (h2 - h2_hi.astype(F32)).astype(BF16)
    wr = wr_ref[...]
    hi = _dot(h2_hi, wr)
    logits = hi[:, :LANES] + hi[:, LANES:] + _dot(h2_lo, wr[:, :LANES]) + br_ref[...]
    lane = lax.broadcasted_iota(jnp.int32, logits.shape, 1)
    neg = -jnp.inf
    lg = jnp.where(lane < N_GROUPS, logits, neg)
    mx = jnp.max(lg, axis=-1, keepdims=True)
    gi = jnp.min(jnp.where(lg == mx, lane, LANES), axis=-1, keepdims=True)
    p_group = 1.0 / jnp.sum(jnp.exp(lg - mx), axis=-1, keepdims=True)
    lo = N_GROUPS + EXPERTS_PER_GROUP * gi
    le = jnp.where((lane >= lo) & (lane < lo + EXPERTS_PER_GROUP), logits, neg)
    v1 = jnp.max(le, axis=-1, keepdims=True)
    i1 = jnp.min(jnp.where(le == v1, lane, LANES), axis=-1, keepdims=True)
    le2 = jnp.where(lane == i1, neg, le)
    v2 = jnp.max(le2, axis=-1, keepdims=True)
    i2 = jnp.min(jnp.where(le2 == v2, lane, LANES), axis=-1, keepdims=True)
    e2 = jnp.exp(v2 - v1)
    w1 = p_group / (1.0 + e2)
    w2 = p_group * e2 / (1.0 + e2)
    wexp_ref[...] = jnp.where(lane == i1, w1, 0.0) + jnp.where(lane == i2, w2, 0.0)


def _outproj_call(seq, xs, oa, ob, ga, gb, mod, woa, wob, wout, norm_w, wr, br, tm):
    t = sum(x.shape[0] for x in xs)
    d = xs[0].shape[1]
    tok = lambda w: pl.BlockSpec((tm, w), lambda i, s: (i, 0))
    return pl.pallas_call(
        functools.partial(_outproj_kernel, n_x=len(xs), n_first=xs[0].shape[0] // tm),
        out_shape=[jax.ShapeDtypeStruct((t, d), F32), jax.ShapeDtypeStruct((t, d), BF16),
                   jax.ShapeDtypeStruct((t, LANES), F32)],
        grid_spec=pltpu.PrefetchScalarGridSpec(
            num_scalar_prefetch=1,
            grid=(t // tm,),
            in_specs=_x_specs(xs, tm) + [
                tok(oa.shape[1]), tok(ob.shape[1]), tok(d), tok(d),
                _const_spec(mod.shape),
                _const_spec(woa.shape), _const_spec(wob.shape), _const_spec(wout.shape),
                _const_spec((1, d)), _const_spec(wr.shape), _const_spec((1, LANES)),
            ],
            out_specs=[tok(d), tok(d), tok(LANES)],
        ),
        compiler_params=pltpu.CompilerParams(vmem_limit_bytes=VMEM_LIMIT),
        name="outproj",
    )(seq, *xs, oa, ob, ga, gb, mod, woa, wob, wout, norm_w, wr, br)


def _moe_kernel(seq_ref, h_ref, wexp_ref, x1_ref, mod_ref, wg_ref, wu_ref, wd_ref, *rest, n_first):
    acc_ref = rest[-1]
    i = pl.program_id(0)
    e = pl.program_id(1)

    @pl.when(e == 0)
    def _():
        acc_ref[...] = jnp.zeros_like(acc_ref)

    h = h_ref[...]
    wexp = wexp_ref[...]
    lane = lax.broadcasted_iota(jnp.int32, wexp.shape, 1)
    epb = wg_ref.shape[0]
    experts = range(epb)
    hg = [_dot(h, wg_ref[j]) for j in experts]
    hu = [_dot(h, wu_ref[j]) for j in experts]
    we = [jnp.sum(jnp.where(lane == N_GROUPS + e * epb + j, wexp, 0.0), axis=-1, keepdims=True) for j in experts]
    act = [(_silu(hg[j]) * hu[j] * we[j]).astype(BF16) for j in experts]
    out = _dot(act[0], wd_ref[0])
    for j in experts[1:]:
        out = out + _dot(act[j], wd_ref[j])
    acc_ref[...] += out

    @pl.when(e == pl.num_programs(1) - 1)
    def _():
        tm, d = acc_ref.shape
        nch = tm // CHUNK
        g2 = _chunk_rows(seq_ref, mod_ref, i, nch, 5, d)
        x2 = x1_ref[...] + _per_chunk(acc_ref[...], lambda c, r: g2[c] * r)
        if len(rest) == 2:
            rest[0][...] = x2
        else:
            nw_ref, yp_ref, ys_ref = rest[:3]
            y = x2 * lax.rsqrt(jnp.mean(x2 * x2, axis=-1, keepdims=True) + EPS) * nw_ref[...]

            @pl.when(i < n_first)
            def _():
                yp_ref[...] = y

            @pl.when(i >= n_first)
            def _():
                ys_ref[...] = y


def _moe_call(seq, h2, wexp, x1, mod, wg, wu, wd, tm, final=None):
    t, d = x1.shape
    ne, _, de = wg.shape
    epb = MOE_EXPERTS_PER_STEP
    tok = lambda w: pl.BlockSpec((tm, w), lambda i, e, s: (i, 0))
    in_specs = [
        tok(d), tok(LANES), tok(d),
        _const_spec(mod.shape),
        pl.BlockSpec((epb, d, de), lambda i, e, s: (e, 0, 0)),
        pl.BlockSpec((epb, d, de), lambda i, e, s: (e, 0, 0)),
        pl.BlockSpec((epb, de, d), lambda i, e, s: (e, 0, 0)),
    ]
    args = [seq, h2, wexp, x1, mod, wg, wu, wd]
    if final is None:
        n_first = 0
        out_shape = jax.ShapeDtypeStruct((t, d), F32)
        out_specs = tok(d)
    else:
        norm_w, tp = final
        assert tp % tm == 0 and (t - tp) % tm == 0
        n_first = tp // tm
        in_specs.append(_const_spec((1, d)))
        args.append(norm_w)
        out_shape = [jax.ShapeDtypeStruct((tp, d), F32), jax.ShapeDtypeStruct((t - tp, d), F32)]
        out_specs = [pl.BlockSpec((tm, d), lambda i, e, s: (jnp.minimum(i, n_first - 1), 0)),
                     pl.BlockSpec((tm, d), lambda i, e, s: (jnp.maximum(i - n_first, 0), 0))]
    return pl.pallas_call(
        functools.partial(_moe_kernel, n_first=n_first),
        out_shape=out_shape,
        grid_spec=pltpu.PrefetchScalarGridSpec(
            num_scalar_prefetch=1,
            grid=(t // tm, ne // epb),
            in_specs=in_specs,
            out_specs=out_specs,
            scratch_shapes=[pltpu.VMEM((tm, d), F32)],
        ),
        compiler_params=pltpu.CompilerParams(dimension_semantics=("arbitrary", "arbitrary"),
                                             vmem_limit_bytes=VMEM_LIMIT),
        name="moe",
    )(*args)


def _rope_table(pos):
    half = ROPE_DIM // 2
    inv = ROPE_THETA ** (-jnp.arange(half, dtype=F32) / half)
    ang = pos.astype(F32)[:, None] * inv[None, :]
    cos, sin = jnp.cos(ang), jnp.sin(ang)
    t = pos.shape[0]
    ones = jnp.ones((t, HD_A - ROPE_DIM), F32)
    zeros = jnp.zeros((t, HD_A - ROPE_DIM), F32)
    zh = jnp.zeros((t, half), F32)
    c_head = jnp.concatenate([cos, cos, ones], axis=1)
    s_lo = jnp.concatenate([zh, sin, zeros], axis=1)
    s_hi = jnp.concatenate([-sin, zh, zeros], axis=1)
    rep = LANES // HD_A
    return jnp.concatenate([jnp.tile(c_head, (1, rep)), jnp.tile(s_lo, (1, rep)), jnp.tile(s_hi, (1, rep))], axis=1)


def kernel(x_prompt, x_sample, cache_k_a, cache_v_a, state_conv_b, state_ssm_b, c_prompt, c_sample, w_mod, b_mod, norm_mix, w_in, sinks_a, w_o_a, conv_b, a_log_b, dt_bias_b, norm_o_b, w_o_b, w_out, norm_ffn, router_g, router_g_b, router_e, router_e_b, w_gate_e, w_up_e, w_down_e, norm_final):
    bp, seq, d = x_prompt.shape
    bs, dseq, _ = x_sample.shape
    depth = w_mod.shape[0]
    assert seq % CHUNK == 0 and dseq == CHUNK and d % LANES == 0
    assert cache_k_a.shape[2] == WIN_CHUNKS * CHUNK
    npc = seq // CHUNK
    tp, ts = bp * seq, bs * dseq
    t = tp + ts
    n_chunks = t // CHUNK
    nseq = bp + bs
    kvw = N_KV_A * HD_A
    cd = 2 * N_QK_B * DK_B + N_V_B * DV_B
    vw = N_V_B * DV_B
    nq = N_HEADS_A * HD_A

    seq_np = np.concatenate([np.repeat(np.arange(bp), npc), bp + np.arange(bs)]).astype(np.int32)
    local_np = np.concatenate([np.tile(np.arange(npc), bp), np.zeros(bs, np.int64)])
    first_np = (local_np == 0).astype(np.int32)
    last_np = np.concatenate([np.tile(np.arange(npc) == npc - 1, bp), np.ones(bs, bool)]).astype(np.int32)
    nvalid_np = np.concatenate([np.minimum(np.tile(np.arange(npc), bp), WIN_CHUNKS),
                                np.full(bs, WIN_CHUNKS)]).astype(np.int32)
    seq_i = jnp.asarray(seq_np)
    first_p, last_p = jnp.asarray(first_np[:bp * npc]), jnp.asarray(last_np[:bp * npc])
    nvalid_p, nvalid_s = jnp.asarray(nvalid_np[:bp * npc]), jnp.asarray(nvalid_np[bp * npc:])
    cb = 4 if npc % 4 == 0 else 2
    assert npc % cb == 0
    cbg = GDN_CHUNKS_PER_STEP if npc % GDN_CHUNKS_PER_STEP == 0 else cb

    pos = jnp.concatenate([jnp.tile(jnp.arange(seq), bp), jnp.tile(PAST_LEN + jnp.arange(dseq), bs)])
    cs = _rope_table(pos)

    xs = [x_prompt.reshape(tp, d), x_sample.reshape(ts, d)]
    c_all = jnp.concatenate([c_prompt, c_sample], axis=0)
    c_rows = -(-nseq // 8) * 8
    c_all = jnp.pad(c_all, ((0, c_rows - nseq), (0, 0)))
    mod = _mod_call(c_all, w_mod, b_mod)

    tm = _pick_tile(np.gcd(tp, ts), 256)
    tm_out = _pick_tile(np.gcd(tp, ts), 512)
    tm_moe = _pick_tile(np.gcd(tp, ts), 1024)
    sizes = np.cumsum([0, nq, kvw, kvw, cd, vw, N_V_B, N_V_B, d, d])
    outs = {k: [] for k in ("k_p", "v_p", "c_p", "s_p", "k_s", "v_s", "c_s", "s_s")}
    for l in range(depth):
        wl = w_in[l]
        seg = lambda a, b: wl[:, sizes[a]:sizes[b]].astype(BF16)
        wqkv, wxb, wz = seg(0, 3), seg(3, 4), seg(4, 5)
        wba = jnp.pad(seg(5, 7), ((0, 0), (0, LANES - 2 * N_V_B)))
        wga, wgb = seg(7, 8), seg(8, 9)
        q, k, v, xb, z, ba, ga, gb = _inproj_call(
            seq_i, xs, mod[l], norm_mix[l].reshape(1, d), cs, wqkv, wxb, wz, wba, wga, wgb, tm)

        oa = _attn_call(nvalid_p, sinks_a[l], q, k, k, v, v, cb, tp // (cb * CHUNK), 0,
                        lambda i: jnp.maximum(i * (cb // WIN_CHUNKS) - 1, 0))
        oa = _attn_call(nvalid_s, sinks_a[l], q, cache_k_a[l].reshape(bs * WIN_CHUNKS * CHUNK, kvw), k,
                        cache_v_a[l].reshape(bs * WIN_CHUNKS * CHUNK, kvw), v, 1, bs, tp // CHUNK, lambda i: i,
                        out_full=oa)

        apar = jnp.zeros((2, LANES), F32)
        apar = apar.at[0, N_V_B:2 * N_V_B].set(a_log_b[l]).at[1, N_V_B:2 * N_V_B].set(dt_bias_b[l])
        nw_o = norm_o_b[l].reshape(1, DV_B)
        ob, ssm_p = _gdn_call(first_p, last_p, xb, jnp.zeros((bp, 8, cd), F32),
                              jnp.zeros((bp, N_V_B, DK_B, DV_B), F32), conv_b[l], ba, apar, z, nw_o,
                              cbg, bp * npc // cbg, 0, npc // cbg)
        conv_init = jnp.pad(state_conv_b[l], ((0, 0), (8 - (CONV_W - 1), 0), (0, 0)))
        ones_s = jnp.ones((bs,), jnp.int32)
        ob, ssm_s = _gdn_call(ones_s, ones_s, xb, conv_init, state_ssm_b[l], conv_b[l], ba, apar, z, nw_o,
                              1, bs, tp // CHUNK, 1, out_full=ob)

        wr = jnp.concatenate([router_g[l], jnp.transpose(router_e[l], (1, 0, 2)).reshape(d, N_EXPERTS)], axis=1)
        wr = jnp.pad(wr, ((0, 0), (0, LANES - wr.shape[1])))
        wr_hi = wr.astype(BF16)
        wr = jnp.concatenate([wr_hi, (wr - wr_hi.astype(F32)).astype(BF16)], axis=1)
        br = jnp.concatenate([router_g_b[l], router_e_b[l].reshape(-1)])
        br = jnp.pad(br, (0, LANES - br.shape[0])).reshape(1, LANES)
        x1, h2, wexp = _outproj_call(seq_i, xs, oa, ob, ga, gb, mod[l], w_o_a[l].astype(BF16),
                                     w_o_b[l].astype(BF16), w_out[l].astype(BF16), norm_ffn[l].reshape(1, d),
                                     wr, br, tm_out)
        final =(norm_final.reshape(1, d), tp) if l == depth - 1 else None
        x_next = _moe_call(seq_i, h2, wexp, x1, mod[l], w_gate_e[l].astype(BF16), w_up_e[l].astype(BF16),
                           w_down_e[l].astype(BF16), tm_moe, final)
        xs = [x_next] if final is None else x_next

        keep = min(WIN_CHUNKS * CHUNK, seq)
        ends = [(b + 1) * seq for b in range(bp)]
        tail = lambda a, n: jnp.stack([a[e - n:e] for e in ends])
        outs["k_p"].append(tail(k, keep).reshape(bp, keep, N_KV_A, HD_A))
        outs["v_p"].append(tail(v, keep).reshape(bp, keep, N_KV_A, HD_A))
        outs["k_s"].append(k[tp:].reshape(bs, dseq, N_KV_A, HD_A))
        outs["v_s"].append(v[tp:].reshape(bs, dseq, N_KV_A, HD_A))
        outs["c_p"].append(tail(xb, CONV_W - 1))
        outs["c_s"].append(xb[tp:].reshape(bs, dseq, cd)[:, dseq - (CONV_W - 1):])
        outs["s_p"].append(ssm_p)
        outs["s_s"].append(ssm_s)

    st = lambda key: jnp.stack(outs[key])
    return (xs[0].reshape(bp, seq, d), xs[1].reshape(bs, dseq, d),
            st("k_p"), st("v_p"), st("c_p"), st("s_p"), st("k_s"), st("v_s"), st("c_s"), st("s_s"))
```

```python
import functools

import numpy as np
import jax
import jax.numpy as jnp
from jax import lax
from jax.experimental import pallas as pl
from jax.experimental.pallas import tpu as pltpu

CHUNK = 64
PAST_LEN = 1024
N_HEADS_A = 8
N_KV_A = 2
HD_A = 64
WIN_CHUNKS = 2
ROPE_DIM = 16
ROPE_THETA = 500000.0
N_QK_B = 4
N_V_B = 8
DK_B = 128
DV_B = 128
CONV_W = 4
N_GROUPS = 4
EXPERTS_PER_GROUP = 4
N_EXPERTS = 16
D_EXPERT = 256
EPS = 1e-6
LANES = 128
VMEM_LIMIT = 56 * 1024 * 1024
MOE_ROW_BLOCK = 128
GDN_CHUNKS_PER_STEP = 8
GDN_CHUNKS_PER_ITER = 2

F32 = jnp.float32
BF16 = jnp.bfloat16
HIGHEST = lax.Precision.HIGHEST


def _pick_tile(total, pref):
    t = pref
    while total % t:
        t //= 2
    assert t >= CHUNK
    return t


def _const_spec(shape):
    nd = len(shape)
    return pl.BlockSpec(shape, lambda *_: (0,) * nd)


def _silu(x):
    return x * (1.0 / (1.0 + jnp.exp(-x)))


def _sigmoid(x):
    return 1.0 / (1.0 + jnp.exp(-x))


def _dot(a, b):
    return jnp.dot(a, b, preferred_element_type=F32)


def _dot_nt(a, b):
    return lax.dot_general(a, b, (((1,), (1,)), ((), ())), preferred_element_type=F32)


def _dot_tn(a, b):
    return lax.dot_general(a, b, (((0,), (0,)), ((), ())), preferred_element_type=F32)


def _dot_hi(a, b):
    return jnp.dot(a, b, preferred_element_type=F32, precision=HIGHEST)


def _mod_kernel(c_ref, w_ref, b_ref, o_ref):
    o_ref[...] = _dot_hi(_silu(c_ref[...]), w_ref[...]) + b_ref[...]


def _mod_call(c_all, w_mod, b_mod):
    depth, d, d6 = w_mod.shape
    rows = c_all.shape[0]
    tn = 1024
    return pl.pallas_call(
        _mod_kernel,
        out_shape=jax.ShapeDtypeStruct((depth, rows, d6), F32),
        grid=(depth, d6 // tn),
        in_specs=[
            pl.BlockSpec((rows, d), lambda l, j: (0, 0)),
            pl.BlockSpec((None, d, tn), lambda l, j: (l, 0, j)),
            pl.BlockSpec((None, 1, tn), lambda l, j: (l, 0, j)),
        ],
        out_specs=pl.BlockSpec((None, rows, tn), lambda l, j: (l, 0, j)),
        compiler_params=pltpu.CompilerParams(vmem_limit_bytes=VMEM_LIMIT),
        name="mod",
    )(c_all, w_mod, b_mod.reshape(depth, 1, d6))


def _rope(x, cs):
    return (x * cs[:, :LANES]
            + pltpu.roll(x, 8, axis=1) * cs[:, LANES:2 * LANES]
            + pltpu.roll(x, LANES - 8, axis=1) * cs[:, 2 * LANES:])


def _x_specs(xs, tm):
    d = xs[0].shape[1]
    if len(xs) == 1:
        return [pl.BlockSpec((tm, d), lambda i, *_: (i, 0))]
    n_first = xs[0].shape[0] // tm
    assert xs[0].shape[0] % tm == 0 and xs[1].shape[0] % tm == 0
    return [pl.BlockSpec((tm, d), lambda i, *_: (jnp.minimum(i, n_first - 1), 0)),
            pl.BlockSpec((tm, d), lambda i, *_: (jnp.maximum(i - n_first, 0), 0))]


def _load_x(i, x_refs, n_first):
    if len(x_refs) == 1:
        return x_refs[0][...]
    return jnp.where(i < n_first, x_refs[0][...], x_refs[1][...])


def _chunk_rows(seq_ref, mod_ref, i, nch, col, d):
    return [mod_ref[pl.ds(seq_ref[i * nch + c], 1), col * d:(col + 1) * d] for c in range(nch)]


def _per_chunk(x, fn):
    nch = x.shape[0] // CHUNK
    return jnp.concatenate([fn(c, x[c * CHUNK:(c + 1) * CHUNK]) for c in range(nch)], axis=0)


def _inproj_kernel(seq_ref, *refs, n_x, n_first):
    x_refs, (mod_ref, nw_ref, cs_ref, wqkv_ref, wxb_ref, wz_ref, wba_ref, wga_ref, wgb_ref,
             q_ref, k_ref, v_ref, xb_ref, z_ref, ba_ref, ga_ref, gb_ref) = refs[:n_x], refs[n_x:]
    i = pl.program_id(0)
    x = _load_x(i, x_refs, n_first)
    tm, d = x.shape
    nch = tm // CHUNK
    xn = x * lax.rsqrt(jnp.mean(x * x, axis=-1, keepdims=True) + EPS) * nw_ref[...]
    shift = _chunk_rows(seq_ref, mod_ref, i, nch, 0, d)
    scale = _chunk_rows(seq_ref, mod_ref, i, nch, 1, d)
    h = _per_chunk(xn, lambda c, r: r * (1.0 + scale[c]) + shift[c]).astype(BF16)
    cs = cs_ref[...]
    qkv = _dot(h, wqkv_ref[...])
    nq = N_HEADS_A * HD_A
    for g in range(nq // LANES):
        q_ref[:, g * LANES:(g + 1) * LANES] = _rope(qkv[:, g * LANES:(g + 1) * LANES], cs).astype(BF16)
    k_ref[...] = _rope(qkv[:, nq:nq + LANES], cs)
    v_ref[...] = qkv[:, nq + LANES:]
    xb_ref[...] = _dot(h, wxb_ref[...])
    z_ref[...] = _dot(h, wz_ref[...])
    ba_ref[...] = _dot(h, wba_ref[...])
    ga_ref[...] = _dot(h, wga_ref[...])
    gb_ref[...] = _dot(h, wgb_ref[...])


def _inproj_call(seq, xs, mod, norm_w, cs, wqkv, wxb, wz, wba, wga, wgb, tm):
    t = sum(x.shape[0] for x in xs)
    d = xs[0].shape[1]
    widths = [N_HEADS_A * HD_A, LANES, LANES, wxb.shape[1], wz.shape[1], LANES, d, d]
    dtypes = [BF16, F32, F32, F32, F32, F32, F32, F32]
    tok = lambda w: pl.BlockSpec((tm, w), lambda i, s: (i, 0))
    return pl.pallas_call(
        functools.partial(_inproj_kernel, n_x=len(xs), n_first=xs[0].shape[0] // tm),
        out_shape=[jax.ShapeDtypeStruct((t, w), dt) for w, dt in zip(widths, dtypes)],
        grid_spec=pltpu.PrefetchScalarGridSpec(
            num_scalar_prefetch=1,
            grid=(t // tm,),
            in_specs=_x_specs(xs, tm) + [
                _const_spec(mod.shape),
                _const_spec((1, d)),
                tok(3 * LANES),
                _const_spec(wqkv.shape), _const_spec(wxb.shape), _const_spec(wz.shape),
                _const_spec(wba.shape), _const_spec(wga.shape), _const_spec(wgb.shape),
            ],
            out_specs=[tok(w) for w in widths],
        ),
        compiler_params=pltpu.CompilerParams(vmem_limit_bytes=VMEM_LIMIT),
        name="inproj",
    )(seq, *xs, mod, norm_w, cs, wqkv, wxb, wz, wba, wga, wgb)


def _attn_kernel(nvalid_ref, sink_ref, q_ref, kp_ref, kt_ref, vp_ref, vt_ref, *rest, cb):
    o_ref = rest[-1]
    i = pl.program_id(0)
    kcat = jnp.concatenate([kp_ref[...], kt_ref[...]], axis=0).astype(BF16)
    vcat = jnp.concatenate([vp_ref[...], vt_ref[...]], axis=0).astype(BF16)
    nk = (WIN_CHUNKS + 1) * CHUNK
    key_chunk = lax.broadcasted_iota(jnp.int32, (CHUNK, nk), 1) // CHUNK
    group = N_HEADS_A // N_KV_A
    heads = range(N_HEADS_A)

    def scores(c):
        q = q_ref[c * CHUNK:(c + 1) * CHUNK, :]
        return [_dot_nt(q[:, h * HD_A:(h + 1) * HD_A],
                        kcat[c * CHUNK:c * CHUNK + nk, (h // group) * HD_A:(h // group + 1) * HD_A])
                for h in heads]

    s_next = scores(0)
    for c in range(cb):
        s_cur = s_next
        if c + 1 < cb:
            s_next = scores(c + 1)
        valid = key_chunk >= (WIN_CHUNKS - nvalid_ref[i * cb + c])
        p, den = [], []
        for h in heads:
            s = jnp.where(valid, s_cur[h] * (HD_A ** -0.5), -jnp.inf)
            sink = sink_ref[h]
            m = jnp.maximum(jnp.max(s, axis=-1, keepdims=True), sink)
            e = jnp.exp(s - m)
            p.append(e.astype(BF16))
            den.append(jnp.sum(e, axis=-1, keepdims=True) + jnp.exp(sink - m))
        o = [_dot(p[h], vcat[c * CHUNK:c * CHUNK + nk, (h // group) * HD_A:(h // group + 1) * HD_A])
             for h in heads]
        for h in heads:
            o_ref[c * CHUNK:(c + 1) * CHUNK, h * HD_A:(h + 1) * HD_A] = (o[h] / den[h]).astype(BF16)


def _attn_call(nvalid, sinks, q, k_prev, k_new, v_prev, v_new, cb, n_tiles, tile_off, prev_map, out_full=None):
    kvw = N_KV_A * HD_A
    qw = N_HEADS_A * HD_A
    tile = lambda w: pl.BlockSpec((cb * CHUNK, w), lambda i, nv: (i + tile_off, 0))
    prev = pl.BlockSpec((WIN_CHUNKS * CHUNK, kvw), lambda i, nv: (prev_map(i), 0))
    args = [nvalid, sinks, q, k_prev, k_new, v_prev, v_new]
    in_specs = [pl.BlockSpec(memory_space=pltpu.SMEM), tile(qw), prev, tile(kvw), prev, tile(kvw)]
    aliases = {}
    if out_full is not None:
        aliases = {len(args): 0}
        args.append(out_full)
        in_specs.append(pl.BlockSpec(memory_space=pl.ANY))
    return pl.pallas_call(
        functools.partial(_attn_kernel, cb=cb),
        out_shape=jax.ShapeDtypeStruct((q.shape[0], qw), BF16),
        grid_spec=pltpu.PrefetchScalarGridSpec(
            num_scalar_prefetch=1,
            grid=(n_tiles,),
            in_specs=in_specs,
            out_specs=tile(qw),
        ),
        input_output_aliases=aliases,
        compiler_params=pltpu.CompilerParams(vmem_limit_bytes=VMEM_LIMIT),
        name="attn",
    )(*args)


def _l2norm(x):
    return x * lax.rsqrt(jnp.sum(x * x, axis=-1, keepdims=True) + EPS)


def _gdn_kernel(first_ref, last_ref, xb_ref, cinit_ref, sinit_ref, cw_ref, ba_ref, apar_ref, z_ref, nw_ref,
                *rest, cb):
    ob_ref, sout_ref, pad_ref, state_ref = rest[-4:]
    step = pl.program_id(0)

    nc = GDN_CHUNKS_PER_ITER if cb % GDN_CHUNKS_PER_ITER == 0 else 1

    def body(it, carry):
        _gdn_chunks(step * cb + it * nc, pl.multiple_of(it * (nc * CHUNK), nc * CHUNK), nc, first_ref, last_ref,
                    xb_ref, cinit_ref, sinit_ref, cw_ref, ba_ref, apar_ref, z_ref, nw_ref,
                    ob_ref, sout_ref, pad_ref, state_ref)
        return carry

    lax.fori_loop(0, cb // nc, body, 0)


def _gdn_chunks(i, row0, nc, first_ref, last_ref, xb_ref, cinit_ref, sinit_ref, cw_ref, ba_ref, apar_ref, z_ref,
                nw_ref, ob_ref, sout_ref, pad_ref, state_ref):
    c = CHUNK
    n = nc * c
    rows = pl.ds(row0, n)

    @pl.when(first_ref[i] == 1)
    def _():
        pad_ref[0:8, :] = cinit_ref[...]
        state_ref[...] = sinit_ref[...]

    pad_ref[8:8 + n, :] = xb_ref[rows, :]
    cw = cw_ref[...]
    conv = pad_ref[5:5 + n, :] * cw[0:1, :]
    for j in range(1, CONV_W):
        conv = conv + pad_ref[5 + j:5 + j + n, :] * cw[j:j + 1, :]
    pad_ref[0:8, :] = pad_ref[n:n + 8, :]
    qkv = _silu(conv)
    nqk = N_QK_B * DK_B

    ba = ba_ref[rows, :]
    apar = apar_ref[...]
    beta = _sigmoid(ba[:, 0:N_V_B])
    sp_in = ba + apar[1:2, :]
    softplus = jnp.maximum(sp_in, 0.0) + jnp.log(1.0 + jnp.exp(-jnp.abs(sp_in)))
    g_all = -jnp.exp(apar[0:1, :]) * softplus
    row = lax.broadcasted_iota(jnp.int32, (c, c), 0)
    col = lax.broadcasted_iota(jnp.int32, (c, c), 1)
    lower = (row >= col).astype(BF16)
    g1 = g_all.astype(BF16)
    r1 = g_all - g1.astype(F32)
    g2 = r1.astype(BF16)
    g3 = (r1 - g2.astype(F32)).astype(BF16)
    gsplit = jnp.concatenate([g1, g2, g3], axis=1)
    gc_all, gc_t = [], []
    for k in range(nc):
        gs = _dot(lower, gsplit[k * c:(k + 1) * c])
        gck = gs[:, :LANES] + gs[:, LANES:2 * LANES] + gs[:, 2 * LANES:]
        gc_all.append(gck)
        gc_t.append(gck.T)

    qn_all = [_l2norm(qkv[:, j * DK_B:(j + 1) * DK_B]) * (DK_B ** -0.5) for j in range(N_QK_B)]
    kn_all = [_l2norm(qkv[:, nqk + j * DK_B:nqk + (j + 1) * DK_B]) for j in range(N_QK_B)]
    pairs = [(k, j) for k in range(nc) for j in range(N_QK_B)]
    qn = {kj: qn_all[kj[1]][kj[0] * c:(kj[0] + 1) * c] for kj in pairs}
    kn = {kj: kn_all[kj[1]][kj[0] * c:(kj[0] + 1) * c] for kj in pairs}
    kt = {kj: kn[kj].T for kj in pairs}
    qkk = {kj: _dot(jnp.concatenate([qn[kj], kn[kj]], axis=0).astype(BF16), kt[kj].astype(BF16))
           for kj in pairs}

    rep = N_V_B // N_QK_B
    items = [(k, h) for k in range(nc) for h in range(N_V_B)]
    qk_of = lambda kh: (kh[0], kh[1] // rep)
    chunk_rows = lambda kh, x: x[kh[0] * c:(kh[0] + 1) * c]
    gcol = {kh: gc_all[kh[0]][:, N_V_B + kh[1]:N_V_B + kh[1] + 1] for kh in items}
    grow = {kh: gc_t[kh[0]][N_V_B + kh[1]:N_V_B + kh[1] + 1, :] for kh in items}
    bcol = {kh: chunk_rows(kh, beta)[:, kh[1]:kh[1] + 1] for kh in items}
    decay = {kh: jnp.exp(jnp.where(row >= col, gcol[kh] - grow[kh], -jnp.inf)) for kh in items}
    a = {kh: jnp.where(row > col, bcol[kh] * qkk[qk_of(kh)][c:] * decay[kh], 0.0) for kh in items}
    xm = {kh: -a[kh] for kh in items}
    p = {kh: _dot(a[kh].astype(BF16), a[kh].astype(BF16)) for kh in items}
    for _ in range(4):
        r = {kh: _dot(jnp.concatenate([xm[kh], p[kh]], axis=0).astype(BF16), p[kh].astype(BF16)) for kh in items}
        xm = {kh: xm[kh] + p[kh] + r[kh][:c] for kh in items}
        p = {kh: r[kh][c:] for kh in items}
    r = {kh: _dot(xm[kh].astype(BF16), p[kh].astype(BF16)) for kh in items}
    xm = {kh: xm[kh] + p[kh] + r[kh] for kh in items}
    egc = {kh: jnp.exp(gcol[kh]) for kh in items}
    rhs = {kh: jnp.concatenate(
        [chunk_rows(kh, qkv[:, 2 * nqk + kh[1] * DV_B:2 * nqk + (kh[1] + 1) * DV_B]) * bcol[kh],
         kn[qk_of(kh)] * (bcol[kh] * egc[kh])], axis=1) for kh in items}
    sol = {kh: rhs[kh] + _dot(xm[kh].astype(BF16), rhs[kh].astype(BF16)) for kh in items}
    wq = {kh: jnp.concatenate([sol[kh][:, DV_B:], qn[qk_of(kh)] * egc[kh]], axis=0).astype(BF16) for kh in items}
    a_qk = {kh: (qkk[qk_of(kh)][:c] * decay[kh]).astype(BF16) for kh in items}
    g_last = {kh: gcol[kh][c - 1:c, :] for kh in items}
    k_dec_t = {kh: (kt[qk_of(kh)] * jnp.exp(g_last[kh] - grow[kh])).astype(BF16) for kh in items}

    heads = range(N_V_B)
    state = [state_ref[h] for h in heads]
    nw = nw_ref[...]
    for k in range(nc):
        r = [_dot(wq[k, h], state[h].astype(BF16)) for h in heads]
        v_new = [(sol[k, h][:, :DV_B] - r[h][:c]).astype(BF16) for h in heads]
        state = [state[h] * jnp.exp(g_last[k, h]) + _dot(k_dec_t[k, h], v_new[h]) for h in heads]
        o = [r[h][c:] + _dot(a_qk[k, h], v_new[h]) for h in heads]
        out_rows = pl.ds(pl.multiple_of(row0 + k * c, c), c)
        for h in heads:
            on = o[h] * lax.rsqrt(jnp.mean(o[h] * o[h], axis=-1, keepdims=True) + EPS) * nw
            zh = z_ref[out_rows, h * DV_B:(h + 1) * DV_B]
            ob_ref[out_rows, h * DV_B:(h + 1) * DV_B] = (on * _silu(zh)).astype(BF16)
    for h in heads:
        state_ref[h] = state[h]

    @pl.when(last_ref[i + nc - 1] == 1)
    def _():
        sout_ref[...] = state_ref[...]


def _gdn_call(first, last, xb, conv_init, ssm_init, conv_w, ba, apar, z, norm_o, cb, n_steps, tile_off,
              steps_per_seq, out_full=None):
    t, cd = xb.shape
    nseq = conv_init.shape[0]
    vw = N_V_B * DV_B
    tile = lambda w: pl.BlockSpec((cb * CHUNK, w), lambda i, f, l: (i + tile_off, 0))
    const = lambda shape: pl.BlockSpec(shape, lambda i, f, l: (0,) * len(shape))
    state = pl.BlockSpec((None, N_V_B, DK_B, DV_B), lambda i, f, l: (i // steps_per_seq, 0, 0, 0))
    args = [first, last, xb, conv_init, ssm_init, conv_w, ba, apar, z, norm_o]
    in_specs = [
        tile(cd),
        pl.BlockSpec((None, 8, cd), lambda i, f, l: (i // steps_per_seq, 0, 0)),
        state,
        const((CONV_W, cd)),
        tile(LANES),
        const((2, LANES)),
        tile(vw),
        const((1, DV_B)),
    ]
    aliases = {}
    if out_full is not None:
        aliases = {len(args): 0}
        args.append(out_full)
        in_specs.append(pl.BlockSpec(memory_space=pl.ANY))
    return pl.pallas_call(
        functools.partial(_gdn_kernel, cb=cb),
        out_shape=[jax.ShapeDtypeStruct((t, vw), BF16),
                   jax.ShapeDtypeStruct((nseq, N_V_B, DK_B, DV_B), F32)],
        grid_spec=pltpu.PrefetchScalarGridSpec(
            num_scalar_prefetch=2,
            grid=(n_steps,),
            in_specs=in_specs,
            out_specs=[tile(vw), state],
            scratch_shapes=[pltpu.VMEM((GDN_CHUNKS_PER_ITER * CHUNK + 8, cd), F32),
                            pltpu.VMEM((N_V_B, DK_B, DV_B), F32)],
        ),
        input_output_aliases=aliases,
        compiler_params=pltpu.CompilerParams(dimension_semantics=("arbitrary",),
                                             vmem_limit_bytes=VMEM_LIMIT),
        name="gdn",
    )(*args)


def _outproj_kernel(seq_ref, *refs, n_x, n_first):
    x_refs, (oa_ref, ob_ref, ga_ref, gb_ref, mod_ref, woa_ref, wob_ref, wout_ref, nw_ref,
             wr_ref, br_ref, x1_ref, h2_ref, wexp_ref) = refs[:n_x], refs[n_x:]
    i = pl.program_id(0)
    x = _load_x(i, x_refs, n_first)
    tm, d = x.shape
    nch = tm // CHUNK
    ya = _dot(oa_ref[...], woa_ref[...])
    yb = _dot(ob_ref[...], wob_ref[...])
    merged = _sigmoid(ga_ref[...]) * ya + _sigmoid(gb_ref[...]) * yb
    mix = _dot(merged.astype(BF16), wout_ref[...])
    g1 = _chunk_rows(seq_ref, mod_ref, i, nch, 2, d)
    x1 = x + _per_chunk(mix, lambda c, r: g1[c] * r)
    x1_ref[...] = x1
    xn = x1 * lax.rsqrt(jnp.mean(x1 * x1, axis=-1, keepdims=True) + EPS) * nw_ref[...]
    shift = _chunk_rows(seq_ref, mod_ref, i, nch, 3, d)
    scale = _chunk_rows(seq_ref, mod_ref, i, nch, 4, d)
    h2 = _per_chunk(xn, lambda c, r: r * (1.0 + scale[c]) + shift[c])
    h2_hi = h2.astype(BF16)
    h2_ref[...] = h2_hi

    h2_lo = (h2 - h2_hi.astype(F32)).astype(BF16)
    wr = wr_ref[...]
    hi = _dot(h2_hi, wr)
    logits = hi[:, :LANES] + hi[:, LANES:] + _dot(h2_lo, wr[:, :LANES]) + br_ref[...]
    lane = lax.broadcasted_iota(jnp.int32, logits.shape, 1)
    neg = -jnp.inf
    lg = jnp.where(lane < N_GROUPS, logits, neg)
    mx = jnp.max(lg, axis=-1, keepdims=True)
    gi = jnp.min(jnp.where(lg == mx, lane, LANES), axis=-1, keepdims=True)
    p_group = 1.0 / jnp.sum(jnp.exp(lg - mx), axis=-1, keepdims=True)
    lo = N_GROUPS + EXPERTS_PER_GROUP * gi
    le = jnp.where((lane >= lo) & (lane < lo + EXPERTS_PER_GROUP), logits, neg)
    v1 = jnp.max(le, axis=-1, keepdims=True)
    i1 = jnp.min(jnp.where(le == v1, lane, LANES), axis=-1, keepdims=True)
    le2 = jnp.where(lane == i1, neg, le)
    v2 = jnp.max(le2, axis=-1, keepdims=True)
    i2 = jnp.min(jnp.where(le2 == v2, lane, LANES), axis=-1, keepdims=True)
    e2 = jnp.exp(v2 - v1)
    w1 = p_group / (1.0 + e2)
    w2 = p_group * e2 / (1.0 + e2)
    wexp_ref[...] = jnp.where(lane == i1, w1, 0.0) + jnp.where(lane == i2, w2, 0.0)


def _outproj_call(seq, xs, oa, ob, ga, gb, mod, woa, wob, wout, norm_w, wr, br, tm):
    t = sum(x.shape[0] for x in xs)
    d = xs[0].shape[1]
    tok = lambda w: pl.BlockSpec((tm, w), lambda i, s: (i, 0))
    return pl.pallas_call(
        functools.partial(_outproj_kernel, n_x=len(xs), n_first=xs[0].shape[0] // tm),
        out_shape=[jax.ShapeDtypeStruct((t, d), F32), jax.ShapeDtypeStruct((t, d), BF16),
                   jax.ShapeDtypeStruct((t, LANES), F32)],
        grid_spec=pltpu.PrefetchScalarGridSpec(
            num_scalar_prefetch=1,
            grid=(t // tm,),
            in_specs=_x_specs(xs, tm) + [
                tok(oa.shape[1]), tok(ob.shape[1]), tok(d), tok(d),
                _const_spec(mod.shape),
                _const_spec(woa.shape), _const_spec(wob.shape), _const_spec(wout.shape),
                _const_spec((1, d)), _const_spec(wr.shape), _const_spec((1, LANES)),
            ],
            out_specs=[tok(d), tok(d), tok(LANES)],
        ),
        compiler_params=pltpu.CompilerParams(vmem_limit_bytes=VMEM_LIMIT),
        name="outproj",
    )(seq, *xs, oa, ob, ga, gb, mod, woa, wob, wout, norm_w, wr, br)


def _split3(x):
    x1 = x.astype(BF16)
    r1 = x - x1.astype(F32)
    x2 = r1.astype(BF16)
    return jnp.concatenate([x1, x2, (r1 - x2.astype(F32)).astype(BF16)], axis=1)


def _moe_kernel(seq_ref, h_ref, wexp_ref, x1_ref, mod_ref, wg_ref, wu_ref, wd_ref, *rest, n_first):
    xs_ref, ws_ref, ys_ref, pmt_ref, meta_ref = rest[-5:]
    outs = rest[:-5]
    i = pl.program_id(0)
    g = pl.program_id(1)
    tm, d = x1_ref.shape
    npos = xs_ref.shape[0]
    rb = MOE_ROW_BLOCK
    epg = wg_ref.shape[0]

    @pl.when(g == 0)
    def _():
        wexp = wexp_ref[...]
        lane = lax.broadcasted_iota(jnp.int32, (tm, LANES), 1)
        sel = jnp.where(wexp > 0.0, 1.0, 0.0).astype(BF16)
        wl = lax.broadcasted_iota(jnp.int32, (LANES, LANES), 0) - N_GROUPS
        gl = lax.broadcasted_iota(jnp.int32, (LANES, LANES), 1)
        in_group = (wl >= gl * epg) & (wl < (gl + 1) * epg) & (gl < N_GROUPS)
        onehot = jnp.where(_dot(sel, jnp.where(in_group, 1.0, 0.0).astype(BF16)) > 0.0, 1.0, 0.0)
        before = lax.broadcasted_iota(jnp.int32, (tm, tm), 0) > lax.broadcasted_iota(jnp.int32, (tm, tm), 1)
        rank = _dot(jnp.where(before, 1.0, 0.0).astype(BF16), onehot.astype(BF16))
        cnt = rank[tm - 1:tm, :] + onehot[tm - 1:tm, :]
        nblk = jnp.floor((cnt + (rb - 1)) * (1.0 / rb))
        base_row = jnp.zeros((1, LANES), F32)
        start = jnp.zeros((), F32)
        for grp in range(N_GROUPS):
            nb = nblk[0, grp]
            meta_ref[grp] = start.astype(jnp.int32)
            meta_ref[N_GROUPS + grp] = nb.astype(jnp.int32)
            base_row = base_row + jnp.where(lane[0:1, :] == grp, start, 0.0)
            start = start + nb * rb
        pos_col = jnp.sum(onehot * (rank + base_row), axis=-1, keepdims=True)
        pmt_ref[...] = jnp.where(pos_col.astype(jnp.int32) == lax.broadcasted_iota(jnp.int32, (tm, npos), 1),
                                 1.0, 0.0).astype(BF16)
        pos_row = jnp.transpose(jnp.broadcast_to(pos_col, (tm, LANES)))[0:1, :].astype(jnp.int32)
        q = LANES // 4
        hw = jnp.concatenate([h_ref[...], _split3(wexp[:, :q]), jnp.zeros((tm, q), BF16)], axis=1)
        used = start.astype(jnp.int32)
        pblk = 256
        for pb in range(npos // pblk):
            @pl.when(pb * pblk < used)
            def _():
                slot = lax.broadcasted_iota(jnp.int32, (pblk, tm), 0) + pb * pblk
                pm = jnp.where(slot == pos_row, 1.0, 0.0).astype(BF16)
                r = _dot(pm, hw)
                xs_ref[pb * pblk:(pb + 1) * pblk, :] = r[:, :d].astype(BF16)
                ws = r[:, d:d + q] + r[:, d + q:d + 2 * q] + r[:, d + 2 * q:d + 3 * q]
                ws_ref[pb * pblk:(pb + 1) * pblk, :] = jnp.concatenate(
                    [ws, jnp.zeros((pblk, LANES - q), F32)], axis=1)
        ys_ref[...] = jnp.zeros_like(ys_ref)

    start = meta_ref[g]

    def block(b, carry):
        r0 = pl.multiple_of(start + b * rb, rb)
        x = xs_ref[pl.ds(r0, rb), :]
        w = ws_ref[pl.ds(r0, rb), :]
        lane = lax.broadcasted_iota(jnp.int32, (rb, LANES), 1)
        experts = range(epg)
        hg = [_dot(x, wg_ref[j]) for j in experts]
        hu = [_dot(x, wu_ref[j]) for j in experts]
        we = [jnp.sum(jnp.where(lane == N_GROUPS + g * epg + j, w, 0.0), axis=-1, keepdims=True) for j in experts]
        act = [(_silu(hg[j]) * hu[j] * we[j]).astype(BF16) for j in experts]
        out = _dot(act[0], wd_ref[0])
        for j in experts[1:]:
            out = out + _dot(act[j], wd_ref[j])
        ys_ref[pl.ds(r0, rb), :] = out.astype(BF16)
        return carry

    lax.fori_loop(0, meta_ref[N_GROUPS + g], block, 0)

    @pl.when(g == pl.num_programs(1) - 1)
    def _():
        nch = tm // CHUNK
        moe = _dot(pmt_ref[...], ys_ref[...])
        g2 = _chunk_rows(seq_ref, mod_ref, i, nch, 5, d)
        x2 = x1_ref[...] + _per_chunk(moe, lambda c, r: g2[c] * r)
        if len(outs) == 1:
            outs[0][...] = x2
        else:
            nw_ref, y_prompt_ref, y_sample_ref = outs
            y = x2 * lax.rsqrt(jnp.mean(x2 * x2, axis=-1, keepdims=True) + EPS) * nw_ref[...]

            @pl.when(i < n_first)
            def _():
                y_prompt_ref[...] = y

            @pl.when(i >= n_first)
            def _():
                y_sample_ref[...] = y


def _moe_call(seq, h2, wexp, x1, mod, wg, wu, wd, tm, final=None):
    t, d = x1.shape
    ne, _, de = wg.shape
    epb = EXPERTS_PER_GROUP
    npos = -(-(tm + N_GROUPS * MOE_ROW_BLOCK) // 256) * 256
    tok =lambda w: pl.BlockSpec((tm, w), lambda i, e, s: (i, 0))
    in_specs = [
        tok(d), tok(LANES), tok(d),
        _const_spec(mod.shape),
        pl.BlockSpec((epb, d, de), lambda i, e, s: (e, 0, 0)),
        pl.BlockSpec((epb, d, de), lambda i, e, s: (e, 0, 0)),
        pl.BlockSpec((epb, de, d), lambda i, e, s: (e, 0, 0)),
    ]
    args = [seq, h2, wexp, x1, mod, wg, wu, wd]
    if final is None:
        n_first = 0
        out_shape = jax.ShapeDtypeStruct((t, d), F32)
        out_specs = tok(d)
    else:
        norm_w, tp = final
        assert tp % tm == 0 and (t - tp) % tm == 0
        n_first = tp // tm
        in_specs.append(_const_spec((1, d)))
        args.append(norm_w)
        out_shape = [jax.ShapeDtypeStruct((tp, d), F32), jax.ShapeDtypeStruct((t - tp, d), F32)]
        out_specs = [pl.BlockSpec((tm, d), lambda i, e, s: (jnp.minimum(i, n_first - 1), 0)),
                     pl.BlockSpec((tm, d), lambda i, e, s: (jnp.maximum(i - n_first, 0), 0))]
    return pl.pallas_call(
        functools.partial(_moe_kernel, n_first=n_first),
        out_shape=out_shape,
        grid_spec=pltpu.PrefetchScalarGridSpec(
            num_scalar_prefetch=1,
            grid=(t // tm, ne // epb),
            in_specs=in_specs,
            out_specs=out_specs,
            scratch_shapes=[pltpu.VMEM((npos, d), BF16), pltpu.VMEM((npos, LANES), F32),
                            pltpu.VMEM((npos, d), BF16), pltpu.VMEM((tm, npos), BF16),
                            pltpu.SMEM((2 * N_GROUPS,), jnp.int32)],
        ),
        compiler_params=pltpu.CompilerParams(dimension_semantics=("arbitrary", "arbitrary"),
                                             vmem_limit_bytes=VMEM_LIMIT),
        name="moe",
    )(*args)


def _rope_table(pos):
    half = ROPE_DIM // 2
    inv = ROPE_THETA ** (-jnp.arange(half, dtype=F32) / half)
    ang = pos.astype(F32)[:, None] * inv[None, :]
    cos, sin = jnp.cos(ang), jnp.sin(ang)
    t = pos.shape[0]
    ones = jnp.ones((t, HD_A - ROPE_DIM), F32)
    zeros = jnp.zeros((t, HD_A - ROPE_DIM), F32)
    zh = jnp.zeros((t, half), F32)
    c_head = jnp.concatenate([cos, cos, ones], axis=1)
    s_lo = jnp.concatenate([zh, sin, zeros], axis=1)
    s_hi = jnp.concatenate([-sin, zh, zeros], axis=1)
    rep = LANES // HD_A
    return jnp.concatenate([jnp.tile(c_head, (1, rep)), jnp.tile(s_lo, (1, rep)), jnp.tile(s_hi, (1, rep))], axis=1)


def kernel(x_prompt, x_sample, cache_k_a, cache_v_a, state_conv_b, state_ssm_b, c_prompt, c_sample, w_mod, b_mod, norm_mix, w_in, sinks_a, w_o_a, conv_b, a_log_b, dt_bias_b, norm_o_b, w_o_b, w_out, norm_ffn, router_g, router_g_b, router_e, router_e_b, w_gate_e, w_up_e, w_down_e, norm_final):
    bp, seq, d = x_prompt.shape
    bs, dseq, _ = x_sample.shape
    depth = w_mod.shape[0]
    assert seq % CHUNK == 0 and dseq == CHUNK and d % LANES == 0
    assert cache_k_a.shape[2] == WIN_CHUNKS * CHUNK
    npc = seq // CHUNK
    tp, ts = bp * seq, bs * dseq
    t = tp + ts
    n_chunks = t // CHUNK
    nseq = bp + bs
    kvw = N_KV_A * HD_A
    cd = 2 * N_QK_B * DK_B + N_V_B * DV_B
    vw = N_V_B * DV_B
    nq = N_HEADS_A * HD_A

    seq_np = np.concatenate([np.repeat(np.arange(bp), npc), bp + np.arange(bs)]).astype(np.int32)
    local_np = np.concatenate([np.tile(np.arange(npc), bp), np.zeros(bs, np.int64)])
    first_np = (local_np == 0).astype(np.int32)
    last_np = np.concatenate([np.tile(np.arange(npc) == npc - 1, bp), np.ones(bs, bool)]).astype(np.int32)
    nvalid_np = np.concatenate([np.minimum(np.tile(np.arange(npc), bp), WIN_CHUNKS),
                                np.full(bs, WIN_CHUNKS)]).astype(np.int32)
    seq_i = jnp.asarray(seq_np)
    first_p, last_p = jnp.asarray(first_np[:bp * npc]), jnp.asarray(last_np[:bp * npc])
    nvalid_p, nvalid_s = jnp.asarray(nvalid_np[:bp * npc]), jnp.asarray(nvalid_np[bp * npc:])
    cb = 4 if npc % 4 == 0 else 2
    assert npc % cb == 0
    cbg = GDN_CHUNKS_PER_STEP if npc % GDN_CHUNKS_PER_STEP == 0 else cb

    pos = jnp.concatenate([jnp.tile(jnp.arange(seq), bp), jnp.tile(PAST_LEN + jnp.arange(dseq), bs)])
    cs = _rope_table(pos)

    xs = [x_prompt.reshape(tp, d), x_sample.reshape(ts, d)]
    c_all = jnp.concatenate([c_prompt, c_sample], axis=0)
    c_rows = -(-nseq // 8) * 8
    c_all = jnp.pad(c_all, ((0, c_rows - nseq), (0, 0)))
    mod = _mod_call(c_all, w_mod, b_mod)

    tm = _pick_tile(np.gcd(tp, ts), 256)
    tm_out = _pick_tile(np.gcd(tp, ts), 512)
    tm_moe = _pick_tile(np.gcd(tp, ts), 1024)
    sizes = np.cumsum([0, nq, kvw, kvw, cd, vw, N_V_B, N_V_B, d, d])
    outs = {k: [] for k in ("k_p", "v_p", "c_p", "s_p", "k_s", "v_s", "c_s", "s_s")}
    for l in range(depth):
        wl = w_in[l]
        seg = lambda a, b: wl[:, sizes[a]:sizes[b]].astype(BF16)
        wqkv, wxb, wz = seg(0, 3), seg(3, 4), seg(4, 5)
        wba = jnp.pad(seg(5, 7), ((0, 0), (0, LANES - 2 * N_V_B)))
        wga, wgb = seg(7, 8), seg(8, 9)
        q, k, v, xb, z, ba, ga, gb = _inproj_call(
            seq_i, xs, mod[l], norm_mix[l].reshape(1, d), cs, wqkv, wxb, wz, wba, wga, wgb, tm)

        oa = _attn_call(nvalid_p, sinks_a[l], q, k, k, v, v, cb, tp // (cb * CHUNK), 0,
                        lambda i: jnp.maximum(i * (cb // WIN_CHUNKS) - 1, 0))
        oa = _attn_call(nvalid_s, sinks_a[l], q, cache_k_a[l].reshape(bs * WIN_CHUNKS * CHUNK, kvw), k,
                        cache_v_a[l].reshape(bs * WIN_CHUNKS * CHUNK, kvw), v, 1, bs, tp // CHUNK, lambda i: i,
                        out_full=oa)

        apar = jnp.zeros((2, LANES), F32)
        apar = apar.at[0, N_V_B:2 * N_V_B].set(a_log_b[l]).at[1, N_V_B:2 * N_V_B].set(dt_bias_b[l])
        nw_o = norm_o_b[l].reshape(1, DV_B)
        ob, ssm_p = _gdn_call(first_p, last_p, xb, jnp.zeros((bp, 8, cd), F32),
                              jnp.zeros((bp, N_V_B, DK_B, DV_B), F32), conv_b[l], ba, apar, z, nw_o,
                              cbg, bp * npc // cbg, 0, npc // cbg)
        conv_init = jnp.pad(state_conv_b[l], ((0, 0), (8 - (CONV_W - 1), 0), (0, 0)))
        ones_s = jnp.ones((bs,), jnp.int32)
        ob, ssm_s = _gdn_call(ones_s, ones_s, xb, conv_init, state_ssm_b[l], conv_b[l], ba, apar, z, nw_o,
                              1, bs, tp // CHUNK, 1, out_full=ob)

        wr = jnp.concatenate([router_g[l], jnp.transpose(router_e[l], (1, 0, 2)).reshape(d, N_EXPERTS)], axis=1)
        wr = jnp.pad(wr, ((0, 0), (0, LANES - wr.shape[1])))
        wr_hi = wr.astype(BF16)
        wr = jnp.concatenate([wr_hi, (wr - wr_hi.astype(F32)).astype(BF16)], axis=1)
        br = jnp.concatenate([router_g_b[l], router_e_b[l].reshape(-1)])
        br = jnp.pad(br, (0, LANES - br.shape[0])).reshape(1, LANES)
        x1, h2, wexp = _outproj_call(seq_i, xs, oa, ob, ga, gb, mod[l], w_o_a[l].astype(BF16),
                                     w_o_b[l].astype(BF16), w_out[l].astype(BF16), norm_ffn[l].reshape(1, d),
                                     wr, br, tm_out)
        final =(norm_final.reshape(1, d), tp) if l == depth - 1 else None
        x_next = _moe_call(seq_i, h2, wexp, x1, mod[l], w_gate_e[l].astype(BF16), w_up_e[l].astype(BF16),
                           w_down_e[l].astype(BF16), tm_moe, final)
        xs = [x_next] if final is None else x_next

        keep = min(WIN_CHUNKS * CHUNK, seq)
        ends = [(b + 1) * seq for b in range(bp)]
        tail = lambda a, n: jnp.stack([a[e - n:e] for e in ends])
        outs["k_p"].append(tail(k, keep).reshape(bp, keep, N_KV_A, HD_A))
        outs["v_p"].append(tail(v, keep).reshape(bp, keep, N_KV_A, HD_A))
        outs["k_s"].append(k[tp:].reshape(bs, dseq, N_KV_A, HD_A))
        outs["v_s"].append(v[tp:].reshape(bs, dseq, N_KV_A, HD_A))
        outs["c_p"].append(tail(xb, CONV_W - 1))
        outs["c_s"].append(xb[tp:].reshape(bs, dseq, cd)[:, dseq - (CONV_W - 1):])
        outs["s_p"].append(ssm_p)
        outs["s_s"].append(ssm_s)

    st = lambda key: jnp.stack(outs[key])
    return (xs[0].reshape(bp, seq, d), xs[1].reshape(bs, dseq, d),
            st("k_p"), st("v_p"), st("c_p"), st("s_p"), st("k_s"), st("v_s"), st("c_s"), st("s_s"))
```

```python
import functools

import numpy as np
import jax
import jax.numpy as jnp
from jax import lax
from jax.experimental import pallas as pl
from jax.experimental.pallas import tpu as pltpu

CHUNK = 64
PAST_LEN = 1024
N_HEADS_A = 8
N_KV_A = 2
HD_A = 64
WIN_CHUNKS = 2
ROPE_DIM = 16
ROPE_THETA = 500000.0
N_QK_B = 4
N_V_B = 8
DK_B = 128
DV_B = 128
CONV_W = 4
N_GROUPS = 4
EXPERTS_PER_GROUP = 4
N_EXPERTS = 16
D_EXPERT = 256
EPS = 1e-6
LANES = 128
VMEM_LIMIT = 56 * 1024 * 1024
MOE_ROW_BLOCK = 128
GDN_CHUNKS_PER_STEP = 8
GDN_CHUNKS_PER_ITER = 2

F32 = jnp.float32
BF16 = jnp.bfloat16
HIGHEST = lax.Precision.HIGHEST


def _pick_tile(total, pref):
    t = pref
    while total % t:
        t //= 2
    assert t >= CHUNK
    return t


def _const_spec(shape):
    nd = len(shape)
    return pl.BlockSpec(shape, lambda *_: (0,) * nd)


def _silu(x):
    return x * (1.0 / (1.0 + jnp.exp(-x)))


def _sigmoid(x):
    return 1.0 / (1.0 + jnp.exp(-x))


def _dot(a, b):
    return jnp.dot(a, b, preferred_element_type=F32)


def _dot_nt(a, b):
    return lax.dot_general(a, b, (((1,), (1,)), ((), ())), preferred_element_type=F32)


def _dot_tn(a, b):
    return lax.dot_general(a, b, (((0,), (0,)), ((), ())), preferred_element_type=F32)


def _dot_hi(a, b):
    return jnp.dot(a, b, preferred_element_type=F32, precision=HIGHEST)


def _mod_kernel(c_ref, w_ref, b_ref, o_ref):
    o_ref[...] = _dot_hi(_silu(c_ref[...]), w_ref[...]) + b_ref[...]


def _mod_call(c_all, w_mod, b_mod):
    depth, d, d6 = w_mod.shape
    rows = c_all.shape[0]
    tn = 1024
    return pl.pallas_call(
        _mod_kernel,
        out_shape=jax.ShapeDtypeStruct((depth, rows, d6), F32),
        grid=(depth, d6 // tn),
        in_specs=[
            pl.BlockSpec((rows, d), lambda l, j: (0, 0)),
            pl.BlockSpec((None, d, tn), lambda l, j: (l, 0, j)),
            pl.BlockSpec((None, 1, tn), lambda l, j: (l, 0, j)),
        ],
        out_specs=pl.BlockSpec((None, rows, tn), lambda l, j: (l, 0, j)),
        compiler_params=pltpu.CompilerParams(vmem_limit_bytes=VMEM_LIMIT),
        name="mod",
    )(c_all, w_mod, b_mod.reshape(depth, 1, d6))


def _rope(x, cs):
    return (x * cs[:, :LANES]
            + pltpu.roll(x, 8, axis=1) * cs[:, LANES:2 * LANES]
            + pltpu.roll(x, LANES - 8, axis=1) * cs[:, 2 * LANES:])


def _x_specs(xs, tm):
    d = xs[0].shape[1]
    if len(xs) == 1:
        return [pl.BlockSpec((tm, d), lambda i, *_: (i, 0))]
    n_first = xs[0].shape[0] // tm
    assert xs[0].shape[0] % tm == 0 and xs[1].shape[0] % tm == 0
    return [pl.BlockSpec((tm, d), lambda i, *_: (jnp.minimum(i, n_first - 1), 0)),
            pl.BlockSpec((tm, d), lambda i, *_: (jnp.maximum(i - n_first, 0), 0))]


def _load_x(i, x_refs, n_first):
    if len(x_refs) == 1:
        return x_refs[0][...]
    return jnp.where(i < n_first, x_refs[0][...], x_refs[1][...])


def _chunk_rows(seq_ref, mod_ref, i, nch, col, d):
    return [mod_ref[pl.ds(seq_ref[i * nch + c], 1), col * d:(col + 1) * d] for c in range(nch)]


def _per_chunk(x, fn):
    nch = x.shape[0] // CHUNK
    return jnp.concatenate([fn(c, x[c * CHUNK:(c + 1) * CHUNK]) for c in range(nch)], axis=0)


def _inproj_kernel(seq_ref, *refs, n_x, n_first):
    x_refs, (mod_ref, nw_ref, cs_ref, rope_e_ref, rope_c_ref, wqkv_ref, wxb_ref, wz_ref, wba_ref, wga_ref, wgb_ref,
             q_ref, k_ref, v_ref, xb_ref, z_ref, ba_ref, ga_ref, gb_ref) = refs[:n_x], refs[n_x:]
    i = pl.program_id(0)
    x = _load_x(i, x_refs, n_first)
    tm, d = x.shape
    nch = tm // CHUNK
    xn = x * lax.rsqrt(jnp.mean(x * x, axis=-1, keepdims=True) + EPS) * nw_ref[...]
    shift = _chunk_rows(seq_ref, mod_ref, i, nch, 0, d)
    scale = _chunk_rows(seq_ref, mod_ref, i, nch, 1, d)
    h = _per_chunk(xn, lambda c, r: r * (1.0 + scale[c]) + shift[c]).astype(BF16)
    cs = _dot(_split3(cs_ref[...]), rope_e_ref[...]) + rope_c_ref[...]
    qkv = _dot(h, wqkv_ref[...])
    nq = N_HEADS_A * HD_A
    for g in range(nq // LANES):
        q_ref[:, g * LANES:(g + 1) * LANES] = _rope(qkv[:, g * LANES:(g + 1) * LANES], cs).astype(BF16)
    k_ref[...] = _rope(qkv[:, nq:nq + LANES], cs)
    v_ref[...] = qkv[:, nq + LANES:]
    xb_ref[...] = _dot(h, wxb_ref[...])
    z_ref[...] = _dot(h, wz_ref[...]).astype(BF16)
    ba_ref[...] = _dot(h, wba_ref[...])
    ga_ref[...] = _dot(h, wga_ref[...]).astype(BF16)
    gb_ref[...] = _dot(h, wgb_ref[...]).astype(BF16)


def _inproj_call(seq, xs, mod, norm_w, cs, rope_e, rope_c, wqkv, wxb, wz, wba, wga, wgb, tm):
    t = sum(x.shape[0] for x in xs)
    d = xs[0].shape[1]
    widths = [N_HEADS_A * HD_A, LANES, LANES, wxb.shape[1], wz.shape[1], LANES, d, d]
    dtypes = [BF16, F32, F32, F32, BF16, F32, BF16, BF16]
    tok = lambda w: pl.BlockSpec((tm, w), lambda i, s: (i, 0))
    return pl.pallas_call(
        functools.partial(_inproj_kernel, n_x=len(xs), n_first=xs[0].shape[0] // tm),
        out_shape=[jax.ShapeDtypeStruct((t, w), dt) for w, dt in zip(widths, dtypes)],
        grid_spec=pltpu.PrefetchScalarGridSpec(
            num_scalar_prefetch=1,
            grid=(t // tm,),
            in_specs=_x_specs(xs, tm) + [
                _const_spec(mod.shape),
                _const_spec((1, d)),
                tok(cs.shape[1]), _const_spec(rope_e.shape), _const_spec(rope_c.shape),
                _const_spec(wqkv.shape), _const_spec(wxb.shape), _const_spec(wz.shape),
                _const_spec(wba.shape), _const_spec(wga.shape), _const_spec(wgb.shape),
            ],
            out_specs=[tok(w) for w in widths],
        ),
        compiler_params=pltpu.CompilerParams(vmem_limit_bytes=VMEM_LIMIT),
        name="inproj",
    )(seq, *xs, mod, norm_w, cs, rope_e, rope_c, wqkv, wxb, wz, wba, wga, wgb)


def _attn_kernel(nvalid_ref, sink_ref, q_ref, kp_ref, kt_ref, vp_ref, vt_ref, *rest, cb):
    o_ref = rest[-1]
    i = pl.program_id(0)
    kcat = jnp.concatenate([kp_ref[...], kt_ref[...]], axis=0).astype(BF16)
    vcat = jnp.concatenate([vp_ref[...], vt_ref[...]], axis=0).astype(BF16)
    nk = (WIN_CHUNKS + 1) * CHUNK
    key_chunk = lax.broadcasted_iota(jnp.int32, (CHUNK, nk), 1) // CHUNK
    group = N_HEADS_A // N_KV_A
    heads = range(N_HEADS_A)

    def scores(c):
        q = q_ref[c * CHUNK:(c + 1) * CHUNK, :]
        return [_dot_nt(q[:, h * HD_A:(h + 1) * HD_A],
                        kcat[c * CHUNK:c * CHUNK + nk, (h // group) * HD_A:(h // group + 1) * HD_A])
                for h in heads]

    s_next = scores(0)
    for c in range(cb):
        s_cur = s_next
        if c + 1 < cb:
            s_next = scores(c + 1)
        valid = key_chunk >= (WIN_CHUNKS - nvalid_ref[i * cb + c])
        p, den = [], []
        for h in heads:
            s = jnp.where(valid, s_cur[h] * (HD_A ** -0.5), -jnp.inf)
            sink = sink_ref[h]
            m = jnp.maximum(jnp.max(s, axis=-1, keepdims=True), sink)
            e = jnp.exp(s - m)
            p.append(e.astype(BF16))
            den.append(jnp.sum(e, axis=-1, keepdims=True) + jnp.exp(sink - m))
        o = [_dot(p[h], vcat[c * CHUNK:c * CHUNK + nk, (h // group) * HD_A:(h // group + 1) * HD_A])
             for h in heads]
        for h in heads:
            o_ref[c * CHUNK:(c + 1) * CHUNK, h * HD_A:(h + 1) * HD_A] = (o[h] / den[h]).astype(BF16)


def _attn_call(nvalid, sinks, q, k_prev, k_new, v_prev, v_new, cb, n_tiles, tile_off, prev_map, out_full=None):
    kvw = N_KV_A * HD_A
    qw = N_HEADS_A * HD_A
    tile = lambda w: pl.BlockSpec((cb * CHUNK, w), lambda i, nv: (i + tile_off, 0))
    prev = pl.BlockSpec((WIN_CHUNKS * CHUNK, kvw), lambda i, nv: (prev_map(i), 0))
    args = [nvalid, sinks, q, k_prev, k_new, v_prev, v_new]
    in_specs = [pl.BlockSpec(memory_space=pltpu.SMEM), tile(qw), prev, tile(kvw), prev, tile(kvw)]
    aliases = {}
    if out_full is not None:
        aliases = {len(args): 0}
        args.append(out_full)
        in_specs.append(pl.BlockSpec(memory_space=pl.ANY))
    return pl.pallas_call(
        functools.partial(_attn_kernel, cb=cb),
        out_shape=jax.ShapeDtypeStruct((q.shape[0], qw), BF16),
        grid_spec=pltpu.PrefetchScalarGridSpec(
            num_scalar_prefetch=1,
            grid=(n_tiles,),
            in_specs=in_specs,
            out_specs=tile(qw),
        ),
        input_output_aliases=aliases,
        compiler_params=pltpu.CompilerParams(vmem_limit_bytes=VMEM_LIMIT),
        name="attn",
    )(*args)


def _l2norm(x):
    return x * lax.rsqrt(jnp.sum(x * x, axis=-1, keepdims=True) + EPS)


def _gdn_kernel(first_ref, last_ref, xb_ref, cinit_ref, sinit_ref, cw_ref, ba_ref, apar_ref, z_ref, nw_ref,
                *rest, cb):
    ob_ref, sout_ref, pad_ref, state_ref = rest[-4:]
    step = pl.program_id(0)

    nc = GDN_CHUNKS_PER_ITER if cb % GDN_CHUNKS_PER_ITER == 0 else 1

    def body(it, carry):
        _gdn_chunks(step * cb + it * nc, pl.multiple_of(it * (nc * CHUNK), nc * CHUNK), nc, first_ref, last_ref,
                    xb_ref, cinit_ref, sinit_ref, cw_ref, ba_ref, apar_ref, z_ref, nw_ref,
                    ob_ref, sout_ref, pad_ref, state_ref)
        return carry

    lax.fori_loop(0, cb // nc, body, 0)


def _gdn_chunks(i, row0, nc, first_ref, last_ref, xb_ref, cinit_ref, sinit_ref, cw_ref, ba_ref, apar_ref, z_ref,
                nw_ref, ob_ref, sout_ref, pad_ref, state_ref):
    c = CHUNK
    n = nc * c
    rows = pl.ds(row0, n)

    @pl.when(first_ref[i] == 1)
    def _():
        pad_ref[0:8, :] = cinit_ref[...]
        state_ref[...] = sinit_ref[...]

    pad_ref[8:8 + n, :] = xb_ref[rows, :]
    cw = cw_ref[...]
    conv = pad_ref[5:5 + n, :] * cw[0:1, :]
    for j in range(1, CONV_W):
        conv = conv + pad_ref[5 + j:5 + j + n, :] * cw[j:j + 1, :]
    pad_ref[0:8, :] = pad_ref[n:n + 8, :]
    qkv = _silu(conv)
    nqk = N_QK_B * DK_B

    ba = ba_ref[rows, :]
    apar = apar_ref[...]
    beta = _sigmoid(ba[:, 0:N_V_B])
    sp_in = ba + apar[1:2, :]
    softplus = jnp.maximum(sp_in, 0.0) + jnp.log(1.0 + jnp.exp(-jnp.abs(sp_in)))
    g_all = -jnp.exp(apar[0:1, :]) * softplus
    row = lax.broadcasted_iota(jnp.int32, (c, c), 0)
    col = lax.broadcasted_iota(jnp.int32, (c, c), 1)
    lower = (row >= col).astype(BF16)
    g1 = g_all.astype(BF16)
    r1 = g_all - g1.astype(F32)
    g2 = r1.astype(BF16)
    g3 = (r1 - g2.astype(F32)).astype(BF16)
    gsplit = jnp.concatenate([g1, g2, g3], axis=1)
    gc_all, gc_t = [], []
    for k in range(nc):
        gs = _dot(lower, gsplit[k * c:(k + 1) * c])
        gck = gs[:, :LANES] + gs[:, LANES:2 * LANES] + gs[:, 2 * LANES:]
        gc_all.append(gck)
        gc_t.append(gck.T)

    qn_all = [_l2norm(qkv[:, j * DK_B:(j + 1) * DK_B]) * (DK_B ** -0.5) for j in range(N_QK_B)]
    kn_all = [_l2norm(qkv[:, nqk + j * DK_B:nqk + (j + 1) * DK_B]) for j in range(N_QK_B)]
    pairs = [(k, j) for k in range(nc) for j in range(N_QK_B)]
    qn = {kj: qn_all[kj[1]][kj[0] * c:(kj[0] + 1) * c] for kj in pairs}
    kn = {kj: kn_all[kj[1]][kj[0] * c:(kj[0] + 1) * c] for kj in pairs}
    kt = {kj: kn[kj].T for kj in pairs}
    qkk = {kj: _dot(jnp.concatenate([qn[kj], kn[kj]], axis=0).astype(BF16), kt[kj].astype(BF16))
           for kj in pairs}

    rep = N_V_B // N_QK_B
    items = [(k, h) for k in range(nc) for h in range(N_V_B)]
    qk_of = lambda kh: (kh[0], kh[1] // rep)
    chunk_rows = lambda kh, x: x[kh[0] * c:(kh[0] + 1) * c]
    gcol = {kh: gc_all[kh[0]][:, N_V_B + kh[1]:N_V_B + kh[1] + 1] for kh in items}
    grow = {kh: gc_t[kh[0]][N_V_B + kh[1]:N_V_B + kh[1] + 1, :] for kh in items}
    bcol = {kh: chunk_rows(kh, beta)[:, kh[1]:kh[1] + 1] for kh in items}
    decay = {kh: jnp.exp(jnp.where(row >= col, gcol[kh] - grow[kh], -jnp.inf)) for kh in items}
    a = {kh: jnp.where(row > col, bcol[kh] * qkk[qk_of(kh)][c:] * decay[kh], 0.0) for kh in items}
    xm = {kh: -a[kh] for kh in items}
    p = {kh: _dot(a[kh].astype(BF16), a[kh].astype(BF16)) for kh in items}
    for _ in range(4):
        r = {kh: _dot(jnp.concatenate([xm[kh], p[kh]], axis=0).astype(BF16), p[kh].astype(BF16)) for kh in items}
        xm = {kh: xm[kh] + p[kh] + r[kh][:c] for kh in items}
        p = {kh: r[kh][c:] for kh in items}
    r = {kh: _dot(xm[kh].astype(BF16), p[kh].astype(BF16)) for kh in items}
    xm = {kh: xm[kh] + p[kh] + r[kh] for kh in items}
    egc = {kh: jnp.exp(gcol[kh]) for kh in items}
    rhs = {kh: jnp.concatenate(
        [chunk_rows(kh, qkv[:, 2 * nqk + kh[1] * DV_B:2 * nqk + (kh[1] + 1) * DV_B]) * bcol[kh],
         kn[qk_of(kh)] * (bcol[kh] * egc[kh])], axis=1) for kh in items}
    sol = {kh: rhs[kh] + _dot(xm[kh].astype(BF16), rhs[kh].astype(BF16)) for kh in items}
    wq = {kh: jnp.concatenate([sol[kh][:, DV_B:], qn[qk_of(kh)] * egc[kh]], axis=0).astype(BF16) for kh in items}
    a_qk = {kh: (qkk[qk_of(kh)][:c] * decay[kh]).astype(BF16) for kh in items}
    g_last = {kh: gcol[kh][c - 1:c, :] for kh in items}
    k_dec_t = {kh: (kt[qk_of(kh)] * jnp.exp(g_last[kh] - grow[kh])).astype(BF16) for kh in items}

    heads = range(N_V_B)
    state = [state_ref[h] for h in heads]
    nw = nw_ref[...]
    for k in range(nc):
        r = [_dot(wq[k, h], state[h].astype(BF16)) for h in heads]
        v_new = [(sol[k, h][:, :DV_B] - r[h][:c]).astype(BF16) for h in heads]
        state = [state[h] * jnp.exp(g_last[k, h]) + _dot(k_dec_t[k, h], v_new[h]) for h in heads]
        o = [r[h][c:] + _dot(a_qk[k, h], v_new[h]) for h in heads]
        out_rows = pl.ds(pl.multiple_of(row0 + k * c, c), c)
        for h in heads:
            on = o[h] * lax.rsqrt(jnp.mean(o[h] * o[h], axis=-1, keepdims=True) + EPS) * nw
            zh = z_ref[out_rows, h * DV_B:(h + 1) * DV_B].astype(F32)
            ob_ref[out_rows, h * DV_B:(h + 1) * DV_B] = (on * _silu(zh)).astype(BF16)
    for h in heads:
        state_ref[h] = state[h]

    @pl.when(last_ref[i + nc - 1] == 1)
    def _():
        sout_ref[...] = state_ref[...]


def _gdn_call(first, last, xb, conv_init, ssm_init, conv_w, ba, apar, z, norm_o, cb, n_steps, tile_off,
              steps_per_seq, out_full=None):
    t, cd = xb.shape
    nseq = conv_init.shape[0]
    vw = N_V_B * DV_B
    tile = lambda w: pl.BlockSpec((cb * CHUNK, w), lambda i, f, l: (i + tile_off, 0))
    const = lambda shape: pl.BlockSpec(shape, lambda i, f, l: (0,) * len(shape))
    state = pl.BlockSpec((None, N_V_B, DK_B, DV_B), lambda i, f, l: (i // steps_per_seq, 0, 0, 0))
    args = [first, last, xb, conv_init, ssm_init, conv_w, ba, apar, z, norm_o]
    in_specs = [
        tile(cd),
        pl.BlockSpec((None, 8, cd), lambda i, f, l: (i // steps_per_seq, 0, 0)),
        state,
        const((CONV_W, cd)),
        tile(LANES),
        const((2, LANES)),
        tile(vw),
        const((1, DV_B)),
    ]
    aliases = {}
    if out_full is not None:
        aliases = {len(args): 0}
        args.append(out_full)
        in_specs.append(pl.BlockSpec(memory_space=pl.ANY))
    return pl.pallas_call(
        functools.partial(_gdn_kernel, cb=cb),
        out_shape=[jax.ShapeDtypeStruct((t, vw), BF16),
                   jax.ShapeDtypeStruct((nseq, N_V_B, DK_B, DV_B), F32)],
        grid_spec=pltpu.PrefetchScalarGridSpec(
            num_scalar_prefetch=2,
            grid=(n_steps,),
            in_specs=in_specs,
            out_specs=[tile(vw), state],
            scratch_shapes=[pltpu.VMEM((GDN_CHUNKS_PER_ITER * CHUNK + 8, cd), F32),
                            pltpu.VMEM((N_V_B, DK_B, DV_B), F32)],
        ),
        input_output_aliases=aliases,
        compiler_params=pltpu.CompilerParams(dimension_semantics=("arbitrary",),
                                             vmem_limit_bytes=VMEM_LIMIT),
        name="gdn",
    )(*args)


def _outproj_kernel(seq_ref, *refs, n_x, n_first):
    x_refs, (oa_ref, ob_ref, ga_ref, gb_ref, mod_ref, woa_ref, wob_ref, wout_ref, nw_ref,
             wr_ref, br_ref, x1_ref, h2_ref, wexp_ref) = refs[:n_x], refs[n_x:]
    i = pl.program_id(0)
    x = _load_x(i, x_refs, n_first)
    tm, d = x.shape
    nch = tm // CHUNK
    ya = _dot(oa_ref[...], woa_ref[...])
    yb = _dot(ob_ref[...], wob_ref[...])
    merged = _sigmoid(ga_ref[...].astype(F32)) * ya + _sigmoid(gb_ref[...].astype(F32)) * yb
    mix = _dot(merged.astype(BF16), wout_ref[...])
    g1 = _chunk_rows(seq_ref, mod_ref, i, nch, 2, d)
    x1 = x + _per_chunk(mix, lambda c, r: g1[c] * r)
    x1_ref[...] = x1
    xn = x1 * lax.rsqrt(jnp.mean(x1 * x1, axis=-1, keepdims=True) + EPS) * nw_ref[...]
    shift = _chunk_rows(seq_ref, mod_ref, i, nch, 3, d)
    scale = _chunk_rows(seq_ref, mod_ref, i, nch, 4, d)
    h2 = _per_chunk(xn, lambda c, r: r * (1.0 + scale[c]) + shift[c])
    h2_hi = h2.astype(BF16)
    h2_ref[...] = h2_hi

    h2_lo = (h2 - h2_hi.astype(F32)).astype(BF16)
    wr = wr_ref[...]
    hi = _dot(h2_hi, wr)
    logits = hi[:, :LANES] + hi[:, LANES:] + _dot(h2_lo, wr[:, :LANES]) + br_ref[...]
    lane = lax.broadcasted_iota(jnp.int32, logits.shape, 1)
    neg = -jnp.inf
    lg = jnp.where(lane < N_GROUPS, logits, neg)
    mx = jnp.max(lg, axis=-1, keepdims=True)
    gi = jnp.min(jnp.where(lg == mx, lane, LANES), axis=-1, keepdims=True)
    p_group = 1.0 / jnp.sum(jnp.exp(lg - mx), axis=-1, keepdims=True)
    lo = N_GROUPS + EXPERTS_PER_GROUP * gi
    le = jnp.where((lane >= lo) & (lane < lo + EXPERTS_PER_GROUP), logits, neg)
    v1 = jnp.max(le, axis=-1, keepdims=True)
    i1 = jnp.min(jnp.where(le == v1, lane, LANES), axis=-1, keepdims=True)
    le2 = jnp.where(lane == i1, neg, le)
    v2 = jnp.max(le2, axis=-1, keepdims=True)
    i2 = jnp.min(jnp.where(le2 == v2, lane, LANES), axis=-1, keepdims=True)
    e2 = jnp.exp(v2 - v1)
    w1 = p_group / (1.0 + e2)
    w2 = p_group * e2 / (1.0 + e2)
    wexp_ref[...] = jnp.where(lane == i1, w1, 0.0) + jnp.where(lane == i2, w2, 0.0)


def _outproj_call(seq, xs, oa, ob, ga, gb, mod, woa, wob, wout, norm_w, wr, br, tm):
    t = sum(x.shape[0] for x in xs)
    d = xs[0].shape[1]
    tok = lambda w: pl.BlockSpec((tm, w), lambda i, s: (i, 0))
    return pl.pallas_call(
        functools.partial(_outproj_kernel, n_x=len(xs), n_first=xs[0].shape[0] // tm),
        out_shape=[jax.ShapeDtypeStruct((t, d), F32), jax.ShapeDtypeStruct((t, d), BF16),
                   jax.ShapeDtypeStruct((t, LANES), F32)],
        grid_spec=pltpu.PrefetchScalarGridSpec(
            num_scalar_prefetch=1,
            grid=(t // tm,),
            in_specs=_x_specs(xs, tm) + [
                tok(oa.shape[1]), tok(ob.shape[1]), tok(d), tok(d),
                _const_spec(mod.shape),
                _const_spec(woa.shape), _const_spec(wob.shape), _const_spec(wout.shape),
                _const_spec((1, d)), _const_spec(wr.shape), _const_spec((1, LANES)),
            ],
            out_specs=[tok(d), tok(d), tok(LANES)],
        ),
        compiler_params=pltpu.CompilerParams(vmem_limit_bytes=VMEM_LIMIT),
        name="outproj",
    )(seq, *xs, oa, ob, ga, gb, mod, woa, wob, wout, norm_w, wr, br)


def _split3(x):
    x1 = x.astype(BF16)
    r1 = x - x1.astype(F32)
    x2 = r1.astype(BF16)
    return jnp.concatenate([x1, x2, (r1 - x2.astype(F32)).astype(BF16)], axis=1)


def _moe_kernel(seq_ref, h_ref, wexp_ref, x1_ref, mod_ref, wg_ref, wu_ref, wd_ref, *rest, n_first):
    xs_ref, ws_ref, ys_ref, pmt_ref, meta_ref = rest[-5:]
    outs = rest[:-5]
    i = pl.program_id(0)
    g = pl.program_id(1)
    tm, d = x1_ref.shape
    npos = xs_ref.shape[0]
    rb = MOE_ROW_BLOCK
    epg = wg_ref.shape[0]

    @pl.when(g == 0)
    def _():
        wexp = wexp_ref[...]
        lane = lax.broadcasted_iota(jnp.int32, (tm, LANES), 1)
        sel = jnp.where(wexp > 0.0, 1.0, 0.0).astype(BF16)
        wl = lax.broadcasted_iota(jnp.int32, (LANES, LANES), 0) - N_GROUPS
        gl = lax.broadcasted_iota(jnp.int32, (LANES, LANES), 1)
        in_group = (wl >= gl * epg) & (wl < (gl + 1) * epg) & (gl < N_GROUPS)
        onehot = jnp.where(_dot(sel, jnp.where(in_group, 1.0, 0.0).astype(BF16)) > 0.0, 1.0, 0.0)
        before = lax.broadcasted_iota(jnp.int32, (tm, tm), 0) > lax.broadcasted_iota(jnp.int32, (tm, tm), 1)
        rank = _dot(jnp.where(before, 1.0, 0.0).astype(BF16), onehot.astype(BF16))
        cnt = rank[tm - 1:tm, :] + onehot[tm - 1:tm, :]
        nblk = jnp.floor((cnt + (rb - 1)) * (1.0 / rb))
        base_row = jnp.zeros((1, LANES), F32)
        start = jnp.zeros((), F32)
        for grp in range(N_GROUPS):
            nb = nblk[0, grp]
            meta_ref[grp] = start.astype(jnp.int32)
            meta_ref[N_GROUPS + grp] = nb.astype(jnp.int32)
            base_row = base_row + jnp.where(lane[0:1, :] == grp, start, 0.0)
            start = start + nb * rb
        pos_col = jnp.sum(onehot * (rank + base_row), axis=-1, keepdims=True)
        pmt_ref[...] = jnp.where(pos_col.astype(jnp.int32) == lax.broadcasted_iota(jnp.int32, (tm, npos), 1),
                                 1.0, 0.0).astype(BF16)
        pos_row = jnp.transpose(jnp.broadcast_to(pos_col, (tm, LANES)))[0:1, :].astype(jnp.int32)
        q = LANES // 4
        hw = jnp.concatenate([h_ref[...], _split3(wexp[:, :q]), jnp.zeros((tm, q), BF16)], axis=1)
        used = start.astype(jnp.int32)
        pblk = 256
        for pb in range(npos // pblk):
            @pl.when(pb * pblk < used)
            def _():
                slot = lax.broadcasted_iota(jnp.int32, (pblk, tm), 0) + pb * pblk
                pm = jnp.where(slot == pos_row, 1.0, 0.0).astype(BF16)
                r = _dot(pm, hw)
                xs_ref[pb * pblk:(pb + 1) * pblk, :] = r[:, :d].astype(BF16)
                ws = r[:, d:d + q] + r[:, d + q:d + 2 * q] + r[:, d + 2 * q:d + 3 * q]
                ws_ref[pb * pblk:(pb + 1) * pblk, :] = jnp.concatenate(
                    [ws, jnp.zeros((pblk, LANES - q), F32)], axis=1)
        ys_ref[...] = jnp.zeros_like(ys_ref)

    start = meta_ref[g]

    def block(b, carry):
        r0 = pl.multiple_of(start + b * rb, rb)
        x = xs_ref[pl.ds(r0, rb), :]
        w = ws_ref[pl.ds(r0, rb), :]
        lane = lax.broadcasted_iota(jnp.int32, (rb, LANES), 1)
        experts = range(epg)
        hg = [_dot(x, wg_ref[j]) for j in experts]
        hu = [_dot(x, wu_ref[j]) for j in experts]
        we = [jnp.sum(jnp.where(lane == N_GROUPS + g * epg + j, w, 0.0), axis=-1, keepdims=True) for j in experts]
        act = [(_silu(hg[j]) * hu[j] * we[j]).astype(BF16) for j in experts]
        out = _dot(act[0], wd_ref[0])
        for j in experts[1:]:
            out = out + _dot(act[j], wd_ref[j])
        ys_ref[pl.ds(r0, rb), :] = out.astype(BF16)
        return carry

    lax.fori_loop(0, meta_ref[N_GROUPS + g], block, 0)

    @pl.when(g == pl.num_programs(1) - 1)
    def _():
        nch = tm // CHUNK
        moe = _dot(pmt_ref[...], ys_ref[...])
        g2 = _chunk_rows(seq_ref, mod_ref, i, nch, 5, d)
        x2 = x1_ref[...] + _per_chunk(moe, lambda c, r: g2[c] * r)
        if len(outs) == 1:
            outs[0][...] = x2
        else:
            nw_ref, y_prompt_ref, y_sample_ref = outs
            y = x2 * lax.rsqrt(jnp.mean(x2 * x2, axis=-1, keepdims=True) + EPS) * nw_ref[...]

            @pl.when(i < n_first)
            def _():
                y_prompt_ref[...] = y

            @pl.when(i >= n_first)
            def _():
                y_sample_ref[...] = y


def _moe_call(seq, h2, wexp, x1, mod, wg, wu, wd, tm, final=None):
    t, d = x1.shape
    ne, _, de = wg.shape
    epb = EXPERTS_PER_GROUP
    npos = -(-(tm + N_GROUPS * MOE_ROW_BLOCK) // 256) * 256
    tok =lambda w: pl.BlockSpec((tm, w), lambda i, e, s: (i, 0))
    in_specs = [
        tok(d), tok(LANES), tok(d),
        _const_spec(mod.shape),
        pl.BlockSpec((epb, d, de), lambda i, e, s: (e, 0, 0)),
        pl.BlockSpec((epb, d, de), lambda i, e, s: (e, 0, 0)),
        pl.BlockSpec((epb, de, d), lambda i, e, s: (e, 0, 0)),
    ]
    args = [seq, h2, wexp, x1, mod, wg, wu, wd]
    if final is None:
        n_first = 0
        out_shape = jax.ShapeDtypeStruct((t, d), F32)
        out_specs = tok(d)
    else:
        norm_w, tp = final
        assert tp % tm == 0 and (t - tp) % tm == 0
        n_first = tp // tm
        in_specs.append(_const_spec((1, d)))
        args.append(norm_w)
        out_shape = [jax.ShapeDtypeStruct((tp, d), F32), jax.ShapeDtypeStruct((t - tp, d), F32)]
        out_specs = [pl.BlockSpec((tm, d), lambda i, e, s: (jnp.minimum(i, n_first - 1), 0)),
                     pl.BlockSpec((tm, d), lambda i, e, s: (jnp.maximum(i - n_first, 0), 0))]
    return pl.pallas_call(
        functools.partial(_moe_kernel, n_first=n_first),
        out_shape=out_shape,
        grid_spec=pltpu.PrefetchScalarGridSpec(
            num_scalar_prefetch=1,
            grid=(t // tm, ne // epb),
            in_specs=in_specs,
            out_specs=out_specs,
            scratch_shapes=[pltpu.VMEM((npos, d), BF16), pltpu.VMEM((npos, LANES), F32),
                            pltpu.VMEM((npos, d), BF16), pltpu.VMEM((tm, npos), BF16),
                            pltpu.SMEM((2 * N_GROUPS,), jnp.int32)],
        ),
        compiler_params=pltpu.CompilerParams(dimension_semantics=("arbitrary", "arbitrary"),
                                             vmem_limit_bytes=VMEM_LIMIT),
        name="moe",
    )(*args)


def _rope_table(pos):
    half = ROPE_DIM // 2
    inv = ROPE_THETA ** (-jnp.arange(half, dtype=F32) / half)
    ang = pos.astype(F32)[:, None] * inv[None, :]
    return jnp.concatenate([jnp.cos(ang), jnp.sin(ang)], axis=1)


def _rope_expansion():
    half = ROPE_DIM // 2
    e = np.zeros((2 * half, 3 * LANES), np.float32)
    c = np.zeros((1, 3 * LANES), np.float32)
    for lane in range(LANES):
        dim = lane % HD_A
        if dim < half:
            e[dim, lane] = 1.0
            e[half + dim, 2 * LANES + lane] = -1.0
        elif dim < ROPE_DIM:
            e[dim - half, lane] = 1.0
            e[dim, LANES + lane] = 1.0
        else:
            c[0, lane] = 1.0
    return jnp.asarray(np.tile(e, (3, 1)), BF16), jnp.asarray(c)


def kernel(x_prompt, x_sample, cache_k_a, cache_v_a, state_conv_b, state_ssm_b, c_prompt, c_sample, w_mod, b_mod, norm_mix, w_in, sinks_a, w_o_a, conv_b, a_log_b, dt_bias_b, norm_o_b, w_o_b, w_out, norm_ffn, router_g, router_g_b, router_e, router_e_b, w_gate_e, w_up_e, w_down_e, norm_final):
    bp, seq, d = x_prompt.shape
    bs, dseq, _ = x_sample.shape
    depth = w_mod.shape[0]
    assert seq % CHUNK == 0 and dseq == CHUNK and d % LANES == 0
    assert cache_k_a.shape[2] == WIN_CHUNKS * CHUNK
    npc = seq // CHUNK
    tp, ts = bp * seq, bs * dseq
    t = tp + ts
    n_chunks = t // CHUNK
    nseq = bp + bs
    kvw = N_KV_A * HD_A
    cd = 2 * N_QK_B * DK_B + N_V_B * DV_B
    vw = N_V_B * DV_B
    nq = N_HEADS_A * HD_A

    seq_np = np.concatenate([np.repeat(np.arange(bp), npc), bp + np.arange(bs)]).astype(np.int32)
    local_np = np.concatenate([np.tile(np.arange(npc), bp), np.zeros(bs, np.int64)])
    first_np = (local_np == 0).astype(np.int32)
    last_np = np.concatenate([np.tile(np.arange(npc) == npc - 1, bp), np.ones(bs, bool)]).astype(np.int32)
    nvalid_np = np.concatenate([np.minimum(np.tile(np.arange(npc), bp), WIN_CHUNKS),
                                np.full(bs, WIN_CHUNKS)]).astype(np.int32)
    seq_i = jnp.asarray(seq_np)
    first_p, last_p = jnp.asarray(first_np[:bp * npc]), jnp.asarray(last_np[:bp * npc])
    nvalid_p, nvalid_s = jnp.asarray(nvalid_np[:bp * npc]), jnp.asarray(nvalid_np[bp * npc:])
    cb = 4 if npc % 4 == 0 else 2
    assert npc % cb == 0
    cbg = GDN_CHUNKS_PER_STEP if npc % GDN_CHUNKS_PER_STEP == 0 else cb

    pos = jnp.concatenate([jnp.tile(jnp.arange(seq), bp), jnp.tile(PAST_LEN + jnp.arange(dseq), bs)])
    cs = _rope_table(pos)
    rope_e, rope_c = _rope_expansion()

    xs = [x_prompt.reshape(tp, d), x_sample.reshape(ts, d)]
    c_all = jnp.concatenate([c_prompt, c_sample], axis=0)
    c_rows = -(-nseq // 8) * 8
    c_all = jnp.pad(c_all, ((0, c_rows - nseq), (0, 0)))
    mod = _mod_call(c_all, w_mod, b_mod)

    tm = _pick_tile(np.gcd(tp, ts), 256)
    tm_out = _pick_tile(np.gcd(tp, ts), 512)
    tm_moe = _pick_tile(np.gcd(tp, ts), 1024)
    sizes = np.cumsum([0, nq, kvw, kvw, cd, vw, N_V_B, N_V_B, d, d])
    outs = {k: [] for k in ("k_p", "v_p", "c_p", "s_p", "k_s", "v_s", "c_s", "s_s")}
    for l in range(depth):
        wl = w_in[l]
        seg = lambda a, b: wl[:, sizes[a]:sizes[b]].astype(BF16)
        wqkv, wxb, wz = seg(0, 3), seg(3, 4), seg(4, 5)
        wba = jnp.pad(seg(5, 7), ((0, 0), (0, LANES - 2 * N_V_B)))
        wga, wgb = seg(7, 8), seg(8, 9)
        q, k, v, xb, z, ba, ga, gb = _inproj_call(
            seq_i, xs, mod[l], norm_mix[l].reshape(1, d), cs, rope_e, rope_c, wqkv, wxb, wz, wba, wga, wgb, tm)

        oa = _attn_call(nvalid_p, sinks_a[l], q, k, k, v, v, cb, tp // (cb * CHUNK), 0,
                        lambda i: jnp.maximum(i * (cb // WIN_CHUNKS) - 1, 0))
        oa = _attn_call(nvalid_s, sinks_a[l], q, cache_k_a[l].reshape(bs * WIN_CHUNKS * CHUNK, kvw), k,
                        cache_v_a[l].reshape(bs * WIN_CHUNKS * CHUNK, kvw), v, 1, bs, tp // CHUNK, lambda i: i,
                        out_full=oa)

        apar = jnp.zeros((2, LANES), F32)
        apar = apar.at[0, N_V_B:2 * N_V_B].set(a_log_b[l]).at[1, N_V_B:2 * N_V_B].set(dt_bias_b[l])
        nw_o = norm_o_b[l].reshape(1, DV_B)
        ob, ssm_p = _gdn_call(first_p, last_p, xb, jnp.zeros((bp, 8, cd), F32),
                              jnp.zeros((bp, N_V_B, DK_B, DV_B), F32), conv_b[l], ba, apar, z, nw_o,
                              cbg, bp * npc // cbg, 0, npc // cbg)
        conv_init = jnp.pad(state_conv_b[l], ((0, 0), (8 - (CONV_W - 1), 0), (0, 0)))
        ones_s = jnp.ones((bs,), jnp.int32)
        ob, ssm_s = _gdn_call(ones_s, ones_s, xb, conv_init, state_ssm_b[l], conv_b[l], ba, apar, z, nw_o,
                              1, bs, tp // CHUNK, 1, out_full=ob)

        wr = jnp.concatenate([router_g[l], jnp.transpose(router_e[l], (1, 0, 2)).reshape(d, N_EXPERTS)], axis=1)
        wr = jnp.pad(wr, ((0, 0), (0, LANES - wr.shape[1])))
        wr_hi = wr.astype(BF16)
        wr = jnp.concatenate([wr_hi, (wr - wr_hi.astype(F32)).astype(BF16)], axis=1)
        br = jnp.concatenate([router_g_b[l], router_e_b[l].reshape(-1)])
        br = jnp.pad(br, (0, LANES - br.shape[0])).reshape(1, LANES)
        x1, h2, wexp = _outproj_call(seq_i, xs, oa, ob, ga, gb, mod[l], w_o_a[l].astype(BF16),
                                     w_o_b[l].astype(BF16), w_out[l].astype(BF16), norm_ffn[l].reshape(1, d),
                                     wr, br, tm_out)
        final =(norm_final.reshape(1, d), tp) if l == depth - 1 else None
        x_next = _moe_call(seq_i, h2, wexp, x1, mod[l], w_gate_e[l].astype(BF16), w_up_e[l].astype(BF16),
                           w_down_e[l].astype(BF16), tm_moe, final)
        xs = [x_next] if final is None else x_next

        keep = min(WIN_CHUNKS * CHUNK, seq)
        ends = [(b + 1) * seq for b in range(bp)]
        tail = lambda a, n: jnp.stack([a[e - n:e] for e in ends])
        outs["k_p"].append(tail(k, keep).reshape(bp, keep, N_KV_A, HD_A))
        outs["v_p"].append(tail(v, keep).reshape(bp, keep, N_KV_A, HD_A))
        outs["k_s"].append(k[tp:].reshape(bs, dseq, N_KV_A, HD_A))
        outs["v_s"].append(v[tp:].reshape(bs, dseq, N_KV_A, HD_A))
        outs["c_p"].append(tail(xb, CONV_W - 1))
        outs["c_s"].append(xb[tp:].reshape(bs, dseq, cd)[:, dseq - (CONV_W - 1):])
        outs["s_p"].append(ssm_p)
        outs["s_s"].append(ssm_s)

    st = lambda key: jnp.stack(outs[key])
    return (xs[0].reshape(bp, seq, d), xs[1].reshape(bs, dseq, d),
            st("k_p"), st("v_p"), st("c_p"), st("s_p"), st("k_s"), st("v_s"), st("c_s"), st("s_s"))
```

```python
import functools

import numpy as np
import jax
import jax.numpy as jnp
from jax import lax
from jax.experimental import pallas as pl
from jax.experimental.pallas import tpu as pltpu

CHUNK = 64
PAST_LEN = 1024
N_HEADS_A = 8
N_KV_A = 2
HD_A = 64
WIN_CHUNKS = 2
ROPE_DIM = 16
ROPE_THETA = 500000.0
N_QK_B = 4
N_V_B = 8
DK_B = 128
DV_B = 128
CONV_W = 4
N_GROUPS = 4
EXPERTS_PER_GROUP = 4
N_EXPERTS = 16
D_EXPERT = 256
EPS = 1e-6
LANES = 128
VMEM_LIMIT = 56 * 1024 * 1024
MOE_ROW_BLOCK = 128
GDN_CHUNKS_PER_STEP = 8
GDN_CHUNKS_PER_ITER = 2

F32 = jnp.float32
BF16 = jnp.bfloat16
HIGHEST = lax.Precision.HIGHEST


def _pick_tile(total, pref):
    t = pref
    while total % t:
        t //= 2
    assert t >= CHUNK
    return t


def _const_spec(shape):
    nd = len(shape)
    return pl.BlockSpec(shape, lambda *_: (0,) * nd)


def _silu(x):
    return x * (1.0 / (1.0 + jnp.exp(-x)))


def _sigmoid(x):
    return 1.0 / (1.0 + jnp.exp(-x))


def _dot(a, b):
    return jnp.dot(a, b, preferred_element_type=F32)


def _dot_nt(a, b):
    return lax.dot_general(a, b, (((1,), (1,)), ((), ())), preferred_element_type=F32)


def _dot_tn(a, b):
    return lax.dot_general(a, b, (((0,), (0,)), ((), ())), preferred_element_type=F32)


def _dot_hi(a, b):
    return jnp.dot(a, b, preferred_element_type=F32, precision=HIGHEST)


def _mod_kernel(c_ref, w_ref, b_ref, o_ref):
    o_ref[...] = _dot_hi(_silu(c_ref[...]), w_ref[...]) + b_ref[...]


def _mod_call(c_all, w_mod, b_mod):
    depth, d, d6 = w_mod.shape
    rows = c_all.shape[0]
    tn = 1024
    return pl.pallas_call(
        _mod_kernel,
        out_shape=jax.ShapeDtypeStruct((depth, rows, d6), F32),
        grid=(depth, d6 // tn),
        in_specs=[
            pl.BlockSpec((rows, d), lambda l, j: (0, 0)),
            pl.BlockSpec((None, d, tn), lambda l, j: (l, 0, j)),
            pl.BlockSpec((None, 1, tn), lambda l, j: (l, 0, j)),
        ],
        out_specs=pl.BlockSpec((None, rows, tn), lambda l, j: (l, 0, j)),
        compiler_params=pltpu.CompilerParams(vmem_limit_bytes=VMEM_LIMIT),
        name="mod",
    )(c_all, w_mod, b_mod.reshape(depth, 1, d6))


def _rope(x, cs):
    return (x * cs[:, :LANES]
            + pltpu.roll(x, 8, axis=1) * cs[:, LANES:2 * LANES]
            + pltpu.roll(x, LANES - 8, axis=1) * cs[:, 2 * LANES:])


def _x_specs(xs, tm):
    d = xs[0].shape[1]
    if len(xs) == 1:
        return [pl.BlockSpec((tm, d), lambda i, *_: (i, 0))]
    n_first = xs[0].shape[0] // tm
    assert xs[0].shape[0] % tm == 0 and xs[1].shape[0] % tm == 0
    return [pl.BlockSpec((tm, d), lambda i, *_: (jnp.minimum(i, n_first - 1), 0)),
            pl.BlockSpec((tm, d), lambda i, *_: (jnp.maximum(i - n_first, 0), 0))]


def _load_x(i, x_refs, n_first):
    if len(x_refs) == 1:
        return x_refs[0][...]
    return jnp.where(i < n_first, x_refs[0][...], x_refs[1][...])


def _chunk_rows(seq_ref, mod_ref, i, nch, col, d):
    return [mod_ref[pl.ds(seq_ref[i * nch + c], 1), col * d:(col + 1) * d] for c in range(nch)]


def _per_chunk(x, fn):
    nch = x.shape[0] // CHUNK
    return jnp.concatenate([fn(c, x[c * CHUNK:(c + 1) * CHUNK]) for c in range(nch)], axis=0)


def _inproj_kernel(seq_ref, *refs, n_x, n_first):
    x_refs, (mod_ref, nw_ref, cs_ref, rope_e_ref, rope_c_ref, wqkv_ref, wxb_ref, wz_ref, wba_ref, wga_ref, wgb_ref,
             q_ref, k_ref, v_ref, xb_ref, z_ref, ba_ref, ga_ref, gb_ref) = refs[:n_x], refs[n_x:]
    i = pl.program_id(0)
    x = _load_x(i, x_refs, n_first)
    tm, d = x.shape
    nch = tm // CHUNK
    xn = x * lax.rsqrt(jnp.mean(x * x, axis=-1, keepdims=True) + EPS) * nw_ref[...]
    shift = _chunk_rows(seq_ref, mod_ref, i, nch, 0, d)
    scale = _chunk_rows(seq_ref, mod_ref, i, nch, 1, d)
    h = _per_chunk(xn, lambda c, r: r * (1.0 + scale[c]) + shift[c]).astype(BF16)
    cs = _dot(_split3(cs_ref[...]), rope_e_ref[...]) + rope_c_ref[...]
    qkv = _dot(h, wqkv_ref[...])
    nq = N_HEADS_A * HD_A
    for g in range(nq // LANES):
        q_ref[:, g * LANES:(g + 1) * LANES] = _rope(qkv[:, g * LANES:(g + 1) * LANES], cs).astype(BF16)
    k_ref[...] = _rope(qkv[:, nq:nq + LANES], cs)
    v_ref[...] = qkv[:, nq + LANES:]
    xb_ref[...] = _dot(h, wxb_ref[...])
    z_ref[...] = _dot(h, wz_ref[...]).astype(BF16)
    ba_ref[...] = _dot(h, wba_ref[...])
    ga_ref[...] = _dot(h, wga_ref[...]).astype(BF16)
    gb_ref[...] = _dot(h, wgb_ref[...]).astype(BF16)


def _inproj_call(seq, xs, mod, norm_w, cs, rope_e, rope_c, wqkv, wxb, wz, wba, wga, wgb, tm):
    t = sum(x.shape[0] for x in xs)
    d = xs[0].shape[1]
    widths = [N_HEADS_A * HD_A, LANES, LANES, wxb.shape[1], wz.shape[1], LANES, d, d]
    dtypes = [BF16, F32, F32, F32, BF16, F32, BF16, BF16]
    tok = lambda w: pl.BlockSpec((tm, w), lambda i, s: (i, 0))
    return pl.pallas_call(
        functools.partial(_inproj_kernel, n_x=len(xs), n_first=xs[0].shape[0] // tm),
        out_shape=[jax.ShapeDtypeStruct((t, w), dt) for w, dt in zip(widths, dtypes)],
        grid_spec=pltpu.PrefetchScalarGridSpec(
            num_scalar_prefetch=1,
            grid=(t // tm,),
            in_specs=_x_specs(xs, tm) + [
                _const_spec(mod.shape),
                _const_spec((1, d)),
                tok(cs.shape[1]), _const_spec(rope_e.shape), _const_spec(rope_c.shape),
                _const_spec(wqkv.shape), _const_spec(wxb.shape), _const_spec(wz.shape),
                _const_spec(wba.shape), _const_spec(wga.shape), _const_spec(wgb.shape),
            ],
            out_specs=[tok(w) for w in widths],
        ),
        compiler_params=pltpu.CompilerParams(vmem_limit_bytes=VMEM_LIMIT),
        name="inproj",
    )(seq, *xs, mod, norm_w, cs, rope_e, rope_c, wqkv, wxb, wz, wba, wga, wgb)


def _attn_kernel(nvalid_ref, sink_ref, q_ref, kp_ref, kt_ref, vp_ref, vt_ref, *rest, cb):
    o_ref = rest[-1]
    i = pl.program_id(0)
    kcat = jnp.concatenate([kp_ref[...], kt_ref[...]], axis=0).astype(BF16)
    vcat = jnp.concatenate([vp_ref[...], vt_ref[...]], axis=0).astype(BF16)
    nk = (WIN_CHUNKS + 1) * CHUNK
    key_chunk = lax.broadcasted_iota(jnp.int32, (CHUNK, nk), 1) // CHUNK
    group = N_HEADS_A // N_KV_A
    heads = range(N_HEADS_A)

    def scores(c):
        q = q_ref[c * CHUNK:(c + 1) * CHUNK, :]
        return [_dot_nt(q[:, h * HD_A:(h + 1) * HD_A],
                        kcat[c * CHUNK:c * CHUNK + nk, (h // group) * HD_A:(h // group + 1) * HD_A])
                for h in heads]

    s_next = scores(0)
    for c in range(cb):
        s_cur = s_next
        if c + 1 < cb:
            s_next = scores(c + 1)
        valid = key_chunk >= (WIN_CHUNKS - nvalid_ref[i * cb + c])
        p, den = [], []
        for h in heads:
            s = jnp.where(valid, s_cur[h] * (HD_A ** -0.5), -jnp.inf)
            sink = sink_ref[h]
            m = jnp.maximum(jnp.max(s, axis=-1, keepdims=True), sink)
            e = jnp.exp(s - m)
            p.append(e.astype(BF16))
            den.append(jnp.sum(e, axis=-1, keepdims=True) + jnp.exp(sink - m))
        o = [_dot(p[h], vcat[c * CHUNK:c * CHUNK + nk, (h // group) * HD_A:(h // group + 1) * HD_A])
             for h in heads]
        for h in heads:
            o_ref[c * CHUNK:(c + 1) * CHUNK, h * HD_A:(h + 1) * HD_A] = (o[h] / den[h]).astype(BF16)


def _attn_call(nvalid, sinks, q, k_prev, k_new, v_prev, v_new, cb, n_tiles, tile_off, prev_map, out_full=None):
    kvw = N_KV_A * HD_A
    qw = N_HEADS_A * HD_A
    tile = lambda w: pl.BlockSpec((cb * CHUNK, w), lambda i, nv: (i + tile_off, 0))
    prev = pl.BlockSpec((WIN_CHUNKS * CHUNK, kvw), lambda i, nv: (prev_map(i), 0))
    args = [nvalid, sinks, q, k_prev, k_new, v_prev, v_new]
    in_specs = [pl.BlockSpec(memory_space=pltpu.SMEM), tile(qw), prev, tile(kvw), prev, tile(kvw)]
    aliases = {}
    if out_full is not None:
        aliases = {len(args): 0}
        args.append(out_full)
        in_specs.append(pl.BlockSpec(memory_space=pl.ANY))
    return pl.pallas_call(
        functools.partial(_attn_kernel, cb=cb),
        out_shape=jax.ShapeDtypeStruct((q.shape[0], qw), BF16),
        grid_spec=pltpu.PrefetchScalarGridSpec(
            num_scalar_prefetch=1,
            grid=(n_tiles,),
            in_specs=in_specs,
            out_specs=tile(qw),
        ),
        input_output_aliases=aliases,
        compiler_params=pltpu.CompilerParams(vmem_limit_bytes=VMEM_LIMIT),
        name="attn",
    )(*args)


def _l2norm(x):
    return x * lax.rsqrt(jnp.sum(x * x, axis=-1, keepdims=True) + EPS)


def _gdn_kernel(first_ref, last_ref, xb_ref, cinit_ref, sinit_ref, cw_ref, ba_ref, apar_ref, z_ref, nw_ref,
                *rest, cb):
    ob_ref, sout_ref, pad_ref, state_ref = rest[-4:]
    step = pl.program_id(0)

    nc = GDN_CHUNKS_PER_ITER if cb % GDN_CHUNKS_PER_ITER == 0 else 1

    def body(it, carry):
        _gdn_chunks(step * cb + it * nc, pl.multiple_of(it * (nc * CHUNK), nc * CHUNK), nc, first_ref, last_ref,
                    xb_ref, cinit_ref, sinit_ref, cw_ref, ba_ref, apar_ref, z_ref, nw_ref,
                    ob_ref, sout_ref, pad_ref, state_ref)
        return carry

    lax.fori_loop(0, cb // nc, body, 0)


def _gdn_chunks(i, row0, nc, first_ref, last_ref, xb_ref, cinit_ref, sinit_ref, cw_ref, ba_ref, apar_ref, z_ref,
                nw_ref, ob_ref, sout_ref, pad_ref, state_ref):
    c = CHUNK
    n = nc * c
    rows = pl.ds(row0, n)

    @pl.when(first_ref[i] == 1)
    def _():
        pad_ref[0:8, :] = cinit_ref[...]
        state_ref[...] = sinit_ref[...]

    pad_ref[8:8 + n, :] = xb_ref[rows, :]
    cw = cw_ref[...]
    conv = pad_ref[5:5 + n, :] * cw[0:1, :]
    for j in range(1, CONV_W):
        conv = conv + pad_ref[5 + j:5 + j + n, :] * cw[j:j + 1, :]
    pad_ref[0:8, :] = pad_ref[n:n + 8, :]
    qkv = _silu(conv)
    nqk = N_QK_B * DK_B

    ba = ba_ref[rows, :]
    apar = apar_ref[...]
    beta = _sigmoid(ba[:, 0:N_V_B])
    sp_in = ba + apar[1:2, :]
    softplus = jnp.maximum(sp_in, 0.0) + jnp.log(1.0 + jnp.exp(-jnp.abs(sp_in)))
    g_all = -jnp.exp(apar[0:1, :]) * softplus
    row = lax.broadcasted_iota(jnp.int32, (c, c), 0)
    col = lax.broadcasted_iota(jnp.int32, (c, c), 1)
    lower = (row >= col).astype(BF16)
    g1 = g_all.astype(BF16)
    r1 = g_all - g1.astype(F32)
    g2 = r1.astype(BF16)
    g3 = (r1 - g2.astype(F32)).astype(BF16)
    gsplit = jnp.concatenate([g1, g2, g3], axis=1)
    gc_all, gc_t = [], []
    for k in range(nc):
        gs = _dot(lower, gsplit[k * c:(k + 1) * c])
        gck = gs[:, :LANES] + gs[:, LANES:2 * LANES] + gs[:, 2 * LANES:]
        gc_all.append(gck)
        gc_t.append(gck.T)

    qn_all = [_l2norm(qkv[:, j * DK_B:(j + 1) * DK_B]) * (DK_B ** -0.5) for j in range(N_QK_B)]
    kn_all = [_l2norm(qkv[:, nqk + j * DK_B:nqk + (j + 1) * DK_B]) for j in range(N_QK_B)]
    pairs = [(k, j) for k in range(nc) for j in range(N_QK_B)]
    qn = {kj: qn_all[kj[1]][kj[0] * c:(kj[0] + 1) * c] for kj in pairs}
    kn = {kj: kn_all[kj[1]][kj[0] * c:(kj[0] + 1) * c] for kj in pairs}
    kt = {kj: kn[kj].T for kj in pairs}
    qkk = {kj: _dot(jnp.concatenate([qn[kj], kn[kj]], axis=0).astype(BF16), kt[kj].astype(BF16))
           for kj in pairs}

    rep = N_V_B // N_QK_B
    items = [(k, h) for k in range(nc) for h in range(N_V_B)]
    qk_of = lambda kh: (kh[0], kh[1] // rep)
    chunk_rows = lambda kh, x: x[kh[0] * c:(kh[0] + 1) * c]
    gcol = {kh: gc_all[kh[0]][:, N_V_B + kh[1]:N_V_B + kh[1] + 1] for kh in items}
    grow = {kh: gc_t[kh[0]][N_V_B + kh[1]:N_V_B + kh[1] + 1, :] for kh in items}
    bcol = {kh: chunk_rows(kh, beta)[:, kh[1]:kh[1] + 1] for kh in items}
    decay = {kh: jnp.exp(jnp.where(row >= col, gcol[kh] - grow[kh], -jnp.inf)) for kh in items}
    a = {kh: jnp.where(row > col, bcol[kh] * qkk[qk_of(kh)][c:] * decay[kh], 0.0) for kh in items}
    xm = {kh: -a[kh] for kh in items}
    p = {kh: _dot(a[kh].astype(BF16), a[kh].astype(BF16)) for kh in items}
    for _ in range(4):
        pb = {kh: p[kh].astype(BF16) for kh in items}
        r = {kh: _dot(jnp.concatenate([xm[kh].astype(BF16), pb[kh]], axis=0), pb[kh]) for kh in items}
        xm = {kh: xm[kh] + p[kh] + r[kh][:c] for kh in items}
        p = {kh: r[kh][c:] for kh in items}
    r = {kh: _dot(xm[kh].astype(BF16), p[kh].astype(BF16)) for kh in items}
    xm = {kh: xm[kh] + p[kh] + r[kh] for kh in items}
    egc = {kh: jnp.exp(gcol[kh]) for kh in items}
    rhs = {kh: jnp.concatenate(
        [chunk_rows(kh, qkv[:, 2 * nqk + kh[1] * DV_B:2 * nqk + (kh[1] + 1) * DV_B]) * bcol[kh],
         kn[qk_of(kh)] * (bcol[kh] * egc[kh])], axis=1) for kh in items}
    sol = {kh: rhs[kh] + _dot(xm[kh].astype(BF16), rhs[kh].astype(BF16)) for kh in items}
    wq = {kh: jnp.concatenate([sol[kh][:, DV_B:], qn[qk_of(kh)] * egc[kh]], axis=0).astype(BF16) for kh in items}
    a_qk = {kh: (qkk[qk_of(kh)][:c] * decay[kh]).astype(BF16) for kh in items}
    g_last = {kh: gcol[kh][c - 1:c, :] for kh in items}
    k_dec_t = {kh: (kt[qk_of(kh)] * jnp.exp(g_last[kh] - grow[kh])).astype(BF16) for kh in items}

    heads = range(N_V_B)
    state = [state_ref[h] for h in heads]
    nw = nw_ref[...]
    for k in range(nc):
        r = [_dot(wq[k, h], state[h].astype(BF16)) for h in heads]
        v_new = [(sol[k, h][:, :DV_B] - r[h][:c]).astype(BF16) for h in heads]
        state = [state[h] * jnp.exp(g_last[k, h]) + _dot(k_dec_t[k, h], v_new[h]) for h in heads]
        o = [r[h][c:] + _dot(a_qk[k, h], v_new[h]) for h in heads]
        out_rows = pl.ds(pl.multiple_of(row0 + k * c, c), c)
        for h in heads:
            on = o[h] * lax.rsqrt(jnp.mean(o[h] * o[h], axis=-1, keepdims=True) + EPS) * nw
            zh = z_ref[out_rows, h * DV_B:(h + 1) * DV_B].astype(F32)
            ob_ref[out_rows, h * DV_B:(h + 1) * DV_B] = (on * _silu(zh)).astype(BF16)
    for h in heads:
        state_ref[h] = state[h]

    @pl.when(last_ref[i + nc - 1] == 1)
    def _():
        sout_ref[...] = state_ref[...]


def _gdn_call(first, last, xb, conv_init, ssm_init, conv_w, ba, apar, z, norm_o, cb, n_steps, tile_off,
              steps_per_seq, out_full=None, ssm_off=0):
    t, cd = xb.shape
    nseq = conv_init.shape[0]
    vw = N_V_B * DV_B
    tile = lambda w: pl.BlockSpec((cb * CHUNK, w), lambda i, f, l: (i + tile_off, 0))
    const = lambda shape: pl.BlockSpec(shape, lambda i, f, l: (0,) * len(shape))
    state = pl.BlockSpec((None, N_V_B, DK_B, DV_B), lambda i, f, l: (i // steps_per_seq, 0, 0, 0))
    args = [first, last, xb, conv_init, ssm_init, conv_w, ba, apar, z, norm_o]
    in_specs = [
        tile(cd),
        pl.BlockSpec((None, 8, cd), lambda i, f, l: (i // steps_per_seq, 0, 0)),
        pl.BlockSpec((None, N_V_B, DK_B, DV_B), lambda i, f, l: (i // steps_per_seq + ssm_off, 0, 0, 0)),
        const((CONV_W, cd)),
        tile(LANES),
        const((2, LANES)),
        tile(vw),
        const((1, DV_B)),
    ]
    aliases = {}
    if out_full is not None:
        aliases = {len(args): 0}
        args.append(out_full)
        in_specs.append(pl.BlockSpec(memory_space=pl.ANY))
    return pl.pallas_call(
        functools.partial(_gdn_kernel, cb=cb),
        out_shape=[jax.ShapeDtypeStruct((t, vw), BF16),
                   jax.ShapeDtypeStruct((nseq, N_V_B, DK_B, DV_B), F32)],
        grid_spec=pltpu.PrefetchScalarGridSpec(
            num_scalar_prefetch=2,
            grid=(n_steps,),
            in_specs=in_specs,
            out_specs=[tile(vw), state],
            scratch_shapes=[pltpu.VMEM((GDN_CHUNKS_PER_ITER * CHUNK + 8, cd), F32),
                            pltpu.VMEM((N_V_B, DK_B, DV_B), F32)],
        ),
        input_output_aliases=aliases,
        compiler_params=pltpu.CompilerParams(dimension_semantics=("arbitrary",),
                                             vmem_limit_bytes=VMEM_LIMIT),
        name="gdn",
    )(*args)


def _outproj_kernel(seq_ref, *refs, n_x, n_first):
    x_refs, (oa_ref, ob_ref, ga_ref, gb_ref, mod_ref, woa_ref, wob_ref, wout_ref, nw_ref,
             wr_ref, br_ref, x1_ref, h2_ref, wexp_ref) = refs[:n_x], refs[n_x:]
    i = pl.program_id(0)
    x = _load_x(i, x_refs, n_first)
    tm, d = x.shape
    nch = tm // CHUNK
    ya = _dot(oa_ref[...], woa_ref[...])
    yb = _dot(ob_ref[...], wob_ref[...])
    merged = _sigmoid(ga_ref[...].astype(F32)) * ya + _sigmoid(gb_ref[...].astype(F32)) * yb
    mix = _dot(merged.astype(BF16), wout_ref[...])
    g1 = _chunk_rows(seq_ref, mod_ref, i, nch, 2, d)
    x1 = x + _per_chunk(mix, lambda c, r: g1[c] * r)
    x1_ref[...] = x1
    xn = x1 * lax.rsqrt(jnp.mean(x1 * x1, axis=-1, keepdims=True) + EPS) * nw_ref[...]
    shift = _chunk_rows(seq_ref, mod_ref, i, nch, 3, d)
    scale = _chunk_rows(seq_ref, mod_ref, i, nch, 4, d)
    h2 = _per_chunk(xn, lambda c, r: r * (1.0 + scale[c]) + shift[c])
    h2_hi = h2.astype(BF16)
    h2_ref[...] = h2_hi

    h2_lo = (h2 - h2_hi.astype(F32)).astype(BF16)
    wr = wr_ref[...]
    hi = _dot(h2_hi, wr)
    logits = hi[:, :LANES] + hi[:, LANES:] + _dot(h2_lo, wr[:, :LANES]) + br_ref[...]
    lane = lax.broadcasted_iota(jnp.int32, logits.shape, 1)
    neg = -jnp.inf
    lg = jnp.where(lane < N_GROUPS, logits, neg)
    mx = jnp.max(lg, axis=-1, keepdims=True)
    gi = jnp.min(jnp.where(lg == mx, lane, LANES), axis=-1, keepdims=True)
    p_group = 1.0 / jnp.sum(jnp.exp(lg - mx), axis=-1, keepdims=True)
    lo = N_GROUPS + EXPERTS_PER_GROUP * gi
    le = jnp.where((lane >= lo) & (lane < lo + EXPERTS_PER_GROUP), logits, neg)
    v1 = jnp.max(le, axis=-1, keepdims=True)
    i1 = jnp.min(jnp.where(le == v1, lane, LANES), axis=-1, keepdims=True)
    le2 = jnp.where(lane == i1, neg, le)
    v2 = jnp.max(le2, axis=-1, keepdims=True)
    i2 = jnp.min(jnp.where(le2 == v2, lane, LANES), axis=-1, keepdims=True)
    e2 = jnp.exp(v2 - v1)
    w1 = p_group / (1.0 + e2)
    w2 = p_group * e2 / (1.0 + e2)
    wexp_ref[...] = jnp.where(lane == i1, w1, 0.0) + jnp.where(lane == i2, w2, 0.0)


def _outproj_call(seq, xs, oa, ob, ga, gb, mod, woa, wob, wout, norm_w, wr, br, tm):
    t = sum(x.shape[0] for x in xs)
    d = xs[0].shape[1]
    tok = lambda w: pl.BlockSpec((tm, w), lambda i, s: (i, 0))
    return pl.pallas_call(
        functools.partial(_outproj_kernel, n_x=len(xs), n_first=xs[0].shape[0] // tm),
        out_shape=[jax.ShapeDtypeStruct((t, d), F32), jax.ShapeDtypeStruct((t, d), BF16),
                   jax.ShapeDtypeStruct((t, LANES), F32)],
        grid_spec=pltpu.PrefetchScalarGridSpec(
            num_scalar_prefetch=1,
            grid=(t // tm,),
            in_specs=_x_specs(xs, tm) + [
                tok(oa.shape[1]), tok(ob.shape[1]), tok(d), tok(d),
                _const_spec(mod.shape),
                _const_spec(woa.shape), _const_spec(wob.shape), _const_spec(wout.shape),
                _const_spec((1, d)), _const_spec(wr.shape), _const_spec((1, LANES)),
            ],
            out_specs=[tok(d), tok(d), tok(LANES)],
        ),
        compiler_params=pltpu.CompilerParams(vmem_limit_bytes=VMEM_LIMIT),
        name="outproj",
    )(seq, *xs, oa, ob, ga, gb, mod, woa, wob, wout, norm_w, wr, br)


def _split3(x):
    x1 = x.astype(BF16)
    r1 = x - x1.astype(F32)
    x2 = r1.astype(BF16)
    return jnp.concatenate([x1, x2, (r1 - x2.astype(F32)).astype(BF16)], axis=1)


def _moe_kernel(seq_ref, h_ref, wexp_ref, x1_ref, mod_ref, wg_ref, wu_ref, wd_ref, *rest, n_first):
    xs_ref, ws_ref, ys_ref, pmt_ref, meta_ref = rest[-5:]
    outs = rest[:-5]
    i = pl.program_id(0)
    g = pl.program_id(1)
    tm, d = x1_ref.shape
    npos = xs_ref.shape[0]
    rb = MOE_ROW_BLOCK
    epg = wg_ref.shape[0]

    @pl.when(g == 0)
    def _():
        wexp = wexp_ref[...]
        lane = lax.broadcasted_iota(jnp.int32, (tm, LANES), 1)
        sel = jnp.where(wexp > 0.0, 1.0, 0.0).astype(BF16)
        wl = lax.broadcasted_iota(jnp.int32, (LANES, LANES), 0) - N_GROUPS
        gl = lax.broadcasted_iota(jnp.int32, (LANES, LANES), 1)
        in_group = (wl >= gl * epg) & (wl < (gl + 1) * epg) & (gl < N_GROUPS)
        onehot = jnp.where(_dot(sel, jnp.where(in_group, 1.0, 0.0).astype(BF16)) > 0.0, 1.0, 0.0)
        before = lax.broadcasted_iota(jnp.int32, (tm, tm), 0) > lax.broadcasted_iota(jnp.int32, (tm, tm), 1)
        rank = _dot(jnp.where(before, 1.0, 0.0).astype(BF16), onehot.astype(BF16))
        cnt = rank[tm - 1:tm, :] + onehot[tm - 1:tm, :]
        nblk = jnp.floor((cnt + (rb - 1)) * (1.0 / rb))
        base_row = jnp.zeros((1, LANES), F32)
        start = jnp.zeros((), F32)
        for grp in range(N_GROUPS):
            nb = nblk[0, grp]
            meta_ref[grp] = start.astype(jnp.int32)
            meta_ref[N_GROUPS + grp] = nb.astype(jnp.int32)
            base_row = base_row + jnp.where(lane[0:1, :] == grp, start, 0.0)
            start = start + nb * rb
        pos_col = jnp.sum(onehot * (rank + base_row), axis=-1, keepdims=True)
        pmt_ref[...] = jnp.where(pos_col.astype(jnp.int32) == lax.broadcasted_iota(jnp.int32, (tm, npos), 1),
                                 1.0, 0.0).astype(BF16)
        pos_row = jnp.transpose(jnp.broadcast_to(pos_col, (tm, LANES)))[0:1, :].astype(jnp.int32)
        q = LANES // 4
        hw = jnp.concatenate([h_ref[...], _split3(wexp[:, :q]), jnp.zeros((tm, q), BF16)], axis=1)
        used = start.astype(jnp.int32)
        pblk = 256
        for pb in range(npos // pblk):
            @pl.when(pb * pblk < used)
            def _():
                slot = lax.broadcasted_iota(jnp.int32, (pblk, tm), 0) + pb * pblk
                pm = jnp.where(slot == pos_row, 1.0, 0.0).astype(BF16)
                r = _dot(pm, hw)
                xs_ref[pb * pblk:(pb + 1) * pblk, :] = r[:, :d].astype(BF16)
                ws = r[:, d:d + q] + r[:, d + q:d + 2 * q] + r[:, d + 2 * q:d + 3 * q]
                ws_ref[pb * pblk:(pb + 1) * pblk, :] = jnp.concatenate(
                    [ws, jnp.zeros((pblk, LANES - q), F32)], axis=1)
        ys_ref[...] = jnp.zeros_like(ys_ref)

    start = meta_ref[g]

    def run_experts(r0, nrows):
        rows = pl.ds(pl.multiple_of(r0, rb), nrows)
        x = xs_ref[rows, :]
        w = ws_ref[rows, :]
        lane = lax.broadcasted_iota(jnp.int32, (nrows, LANES), 1)
        experts = range(epg)
        hg = [_dot(x, wg_ref[j]) for j in experts]
        hu = [_dot(x, wu_ref[j]) for j in experts]
        we = [jnp.sum(jnp.where(lane == N_GROUPS + g * epg + j, w, 0.0), axis=-1, keepdims=True) for j in experts]
        act = [(_silu(hg[j]) * hu[j] * we[j]).astype(BF16) for j in experts]
        out = _dot(act[0], wd_ref[0])
        for j in experts[1:]:
            out = out + _dot(act[j], wd_ref[j])
        ys_ref[rows, :] = out.astype(BF16)

    nblocks = meta_ref[N_GROUPS + g]
    npairs = lax.shift_right_logical(nblocks, 1)

    def pair(b, carry):
        run_experts(start + b * (2 * rb), 2 * rb)
        return carry

    lax.fori_loop(0, npairs, pair, 0)

    @pl.when(nblocks - 2 * npairs == 1)
    def _():
        run_experts(start + npairs * (2 * rb), rb)

    @pl.when(g == pl.num_programs(1) - 1)
    def _():
        nch = tm // CHUNK
        moe = _dot(pmt_ref[...], ys_ref[...])
        g2 = _chunk_rows(seq_ref, mod_ref, i, nch, 5, d)
        x2 = x1_ref[...] + _per_chunk(moe, lambda c, r: g2[c] * r)
        if len(outs) == 1:
            outs[0][...] = x2
        else:
            nw_ref, y_prompt_ref, y_sample_ref = outs
            y = x2 * lax.rsqrt(jnp.mean(x2 * x2, axis=-1, keepdims=True) + EPS) * nw_ref[...]

            @pl.when(i < n_first)
            def _():
                y_prompt_ref[...] = y

            @pl.when(i >= n_first)
            def _():
                y_sample_ref[...] = y


def _moe_call(seq, h2, wexp, x1, mod, wg, wu, wd, layer, tm, final=None):
    t, d = x1.shape
    de = wg.shape[2]
    w_off = layer * N_GROUPS
    epb = EXPERTS_PER_GROUP
    npos = -(-(tm + N_GROUPS * MOE_ROW_BLOCK) // 256) * 256
    tok =lambda w: pl.BlockSpec((tm, w), lambda i, e, s: (i, 0))
    in_specs = [
        tok(d), tok(LANES), tok(d),
        _const_spec(mod.shape),
        pl.BlockSpec((epb, d, de), lambda i, e, s: (w_off + e, 0, 0)),
        pl.BlockSpec((epb, d, de), lambda i, e, s: (w_off + e, 0, 0)),
        pl.BlockSpec((epb, de, d), lambda i, e, s: (w_off + e, 0, 0)),
    ]
    args = [seq, h2, wexp, x1, mod, wg, wu, wd]
    if final is None:
        n_first = 0
        out_shape = jax.ShapeDtypeStruct((t, d), F32)
        out_specs = tok(d)
    else:
        norm_w, tp = final
        assert tp % tm == 0 and (t - tp) % tm == 0
        n_first = tp // tm
        in_specs.append(_const_spec((1, d)))
        args.append(norm_w)
        out_shape = [jax.ShapeDtypeStruct((tp, d), F32), jax.ShapeDtypeStruct((t - tp, d), F32)]
        out_specs = [pl.BlockSpec((tm, d), lambda i, e, s: (jnp.minimum(i, n_first - 1), 0)),
                     pl.BlockSpec((tm, d), lambda i, e, s: (jnp.maximum(i - n_first, 0), 0))]
    return pl.pallas_call(
        functools.partial(_moe_kernel, n_first=n_first),
        out_shape=out_shape,
        grid_spec=pltpu.PrefetchScalarGridSpec(
            num_scalar_prefetch=1,
            grid=(t // tm, N_GROUPS),
            in_specs=in_specs,
            out_specs=out_specs,
            scratch_shapes=[pltpu.VMEM((npos, d), BF16), pltpu.VMEM((npos, LANES), F32),
                            pltpu.VMEM((npos, d), BF16), pltpu.VMEM((tm, npos), BF16),
                            pltpu.SMEM((2 * N_GROUPS,), jnp.int32)],
        ),
        compiler_params=pltpu.CompilerParams(dimension_semantics=("arbitrary", "arbitrary"),
                                             vmem_limit_bytes=VMEM_LIMIT),
        name="moe",
    )(*args)


def _rope_table(pos):
    half = ROPE_DIM // 2
    inv = ROPE_THETA ** (-jnp.arange(half, dtype=F32) / half)
    ang = pos.astype(F32)[:, None] * inv[None, :]
    return jnp.concatenate([jnp.cos(ang), jnp.sin(ang)], axis=1)


def _rope_expansion():
    half = ROPE_DIM // 2
    e = np.zeros((2 * half, 3 * LANES), np.float32)
    c = np.zeros((1, 3 * LANES), np.float32)
    for lane in range(LANES):
        dim = lane % HD_A
        if dim < half:
            e[dim, lane] = 1.0
            e[half + dim, 2 * LANES + lane] = -1.0
        elif dim < ROPE_DIM:
            e[dim - half, lane] = 1.0
            e[dim, LANES + lane] = 1.0
        else:
            c[0, lane] = 1.0
    return jnp.asarray(np.tile(e, (3, 1)), BF16), jnp.asarray(c)


def kernel(x_prompt, x_sample, cache_k_a, cache_v_a, state_conv_b, state_ssm_b, c_prompt, c_sample, w_mod, b_mod, norm_mix, w_in, sinks_a, w_o_a, conv_b, a_log_b, dt_bias_b, norm_o_b, w_o_b, w_out, norm_ffn, router_g, router_g_b, router_e, router_e_b, w_gate_e, w_up_e, w_down_e, norm_final):
    bp, seq, d = x_prompt.shape
    bs, dseq, _ = x_sample.shape
    depth = w_mod.shape[0]
    assert seq % CHUNK == 0 and dseq == CHUNK and d % LANES == 0
    assert cache_k_a.shape[2] == WIN_CHUNKS * CHUNK
    npc = seq // CHUNK
    tp, ts = bp * seq, bs * dseq
    t = tp + ts
    n_chunks = t // CHUNK
    nseq = bp + bs
    kvw = N_KV_A * HD_A
    cd = 2 * N_QK_B * DK_B + N_V_B * DV_B
    vw = N_V_B * DV_B
    nq = N_HEADS_A * HD_A

    seq_np = np.concatenate([np.repeat(np.arange(bp), npc), bp + np.arange(bs)]).astype(np.int32)
    local_np = np.concatenate([np.tile(np.arange(npc), bp), np.zeros(bs, np.int64)])
    first_np = (local_np == 0).astype(np.int32)
    last_np = np.concatenate([np.tile(np.arange(npc) == npc - 1, bp), np.ones(bs, bool)]).astype(np.int32)
    nvalid_np = np.concatenate([np.minimum(np.tile(np.arange(npc), bp), WIN_CHUNKS),
                                np.full(bs, WIN_CHUNKS)]).astype(np.int32)
    seq_i = jnp.asarray(seq_np)
    first_p, last_p = jnp.asarray(first_np[:bp * npc]), jnp.asarray(last_np[:bp * npc])
    nvalid_p, nvalid_s = jnp.asarray(nvalid_np[:bp * npc]), jnp.asarray(nvalid_np[bp * npc:])
    cb = 16 if npc % 16 == 0 else (4 if npc % 4 == 0 else 2)
    assert npc % cb == 0
    cbg = GDN_CHUNKS_PER_STEP if npc % GDN_CHUNKS_PER_STEP == 0 else cb

    pos = jnp.concatenate([jnp.tile(jnp.arange(seq), bp), jnp.tile(PAST_LEN + jnp.arange(dseq), bs)])
    cs = _rope_table(pos)
    rope_e, rope_c = _rope_expansion()

    xs = [x_prompt.reshape(tp, d), x_sample.reshape(ts, d)]
    c_all = jnp.concatenate([c_prompt, c_sample], axis=0)
    c_rows = -(-nseq // 8) * 8
    c_all = jnp.pad(c_all, ((0, c_rows - nseq), (0, 0)))
    mod = _mod_call(c_all, w_mod, b_mod)

    tm = _pick_tile(np.gcd(tp, ts), 256)
    tm_out = _pick_tile(np.gcd(tp, ts), 512)
    tm_moe = _pick_tile(np.gcd(tp, ts), 1024)
    sizes = np.cumsum([0, nq, kvw, kvw, cd, vw, N_V_B, N_V_B, d, d])
    wg_all = w_gate_e.astype(BF16).reshape(depth * N_EXPERTS, d, D_EXPERT)
    wu_all = w_up_e.astype(BF16).reshape(depth * N_EXPERTS, d, D_EXPERT)
    wd_all = w_down_e.astype(BF16).reshape(depth * N_EXPERTS, D_EXPERT, d)
    ssm_in = state_ssm_b.reshape(depth * bs, N_V_B, DK_B, DV_B)
    outs = {k: [] for k in ("k_p", "v_p", "c_p", "s_p", "k_s", "v_s", "c_s", "s_s")}
    for l in range(depth):
        wl = w_in[l]
        seg = lambda a, b: wl[:, sizes[a]:sizes[b]].astype(BF16)
        wqkv, wxb, wz = seg(0, 3), seg(3, 4), seg(4, 5)
        wba = jnp.pad(seg(5, 7), ((0, 0), (0, LANES - 2 * N_V_B)))
        wga, wgb = seg(7, 8), seg(8, 9)
        q, k, v, xb, z, ba, ga, gb = _inproj_call(
            seq_i, xs, mod[l], norm_mix[l].reshape(1, d), cs, rope_e, rope_c, wqkv, wxb, wz, wba, wga, wgb, tm)

        oa = _attn_call(nvalid_p, sinks_a[l], q, k, k, v, v, cb, tp // (cb * CHUNK), 0,
                        lambda i: jnp.maximum(i * (cb // WIN_CHUNKS) - 1, 0))
        oa = _attn_call(nvalid_s, sinks_a[l], q, cache_k_a[l].reshape(bs * WIN_CHUNKS * CHUNK, kvw), k,
                        cache_v_a[l].reshape(bs * WIN_CHUNKS * CHUNK, kvw), v, 1, bs, tp // CHUNK, lambda i: i,
                        out_full=oa)

        apar = jnp.zeros((2, LANES), F32)
        apar = apar.at[0, N_V_B:2 * N_V_B].set(a_log_b[l]).at[1, N_V_B:2 * N_V_B].set(dt_bias_b[l])
        nw_o = norm_o_b[l].reshape(1, DV_B)
        ob, ssm_p = _gdn_call(first_p, last_p, xb, jnp.zeros((bp, 8, cd), F32),
                              jnp.zeros((bp, N_V_B, DK_B, DV_B), F32), conv_b[l], ba, apar, z, nw_o,
                              cbg, bp * npc // cbg, 0, npc // cbg)
        conv_init = jnp.pad(state_conv_b[l], ((0, 0), (8 - (CONV_W - 1), 0), (0, 0)))
        ones_s = jnp.ones((bs,), jnp.int32)
        ob, ssm_s = _gdn_call(ones_s, ones_s, xb, conv_init, ssm_in, conv_b[l], ba, apar, z, nw_o,
                              1, bs, tp // CHUNK, 1, out_full=ob, ssm_off=l * bs)

        wr = jnp.concatenate([router_g[l], jnp.transpose(router_e[l], (1, 0, 2)).reshape(d, N_EXPERTS)], axis=1)
        wr = jnp.pad(wr, ((0, 0), (0, LANES - wr.shape[1])))
        wr_hi = wr.astype(BF16)
        wr = jnp.concatenate([wr_hi, (wr - wr_hi.astype(F32)).astype(BF16)], axis=1)
        br = jnp.concatenate([router_g_b[l], router_e_b[l].reshape(-1)])
        br = jnp.pad(br, (0, LANES - br.shape[0])).reshape(1, LANES)
        x1, h2, wexp = _outproj_call(seq_i, xs, oa, ob, ga, gb, mod[l], w_o_a[l].astype(BF16),
                                     w_o_b[l].astype(BF16), w_out[l].astype(BF16), norm_ffn[l].reshape(1, d),
                                     wr, br, tm_out)
        final =(norm_final.reshape(1, d), tp) if l == depth - 1 else None
        x_next = _moe_call(seq_i, h2, wexp, x1, mod[l], wg_all, wu_all, wd_all, l, tm_moe, final)
        xs = [x_next] if final is None else x_next

        keep = min(WIN_CHUNKS * CHUNK, seq)
        ends = [(b + 1) * seq for b in range(bp)]
        tail = lambda a, n: jnp.stack([a[e - n:e] for e in ends])
        outs["k_p"].append(tail(k, keep).reshape(bp, keep, N_KV_A, HD_A))
        outs["v_p"].append(tail(v, keep).reshape(bp, keep, N_KV_A, HD_A))
        outs["k_s"].append(k[tp:].reshape(bs, dseq, N_KV_A, HD_A))
        outs["v_s"].append(v[tp:].reshape(bs, dseq, N_KV_A, HD_A))
        outs["c_p"].append(tail(xb, CONV_W - 1))
        outs["c_s"].append(jnp.stack([xb[tp + (b + 1) * dseq - (CONV_W - 1):tp + (b + 1) * dseq]
                                      for b in range(bs)]))
        outs["s_p"].append(ssm_p)
        outs["s_s"].append(ssm_s)

    st = lambda key: jnp.stack(outs[key])
    return (xs[0].reshape(bp, seq, d), xs[1].reshape(bs, dseq, d),
            st("k_p"), st("v_p"), st("c_p"), st("s_p"), st("k_s"), st("v_s"), st("c_s"), st("s_s"))
```

```python
import functools

import numpy as np
import jax
import jax.numpy as jnp
from jax import lax
from jax.experimental import pallas as pl
from jax.experimental.pallas import tpu as pltpu

CHUNK = 64
PAST_LEN = 1024
N_HEADS_A = 8
N_KV_A = 2
HD_A = 64
WIN_CHUNKS = 2
ROPE_DIM = 16
ROPE_THETA = 500000.0
N_QK_B = 4
N_V_B = 8
DK_B = 128
DV_B = 128
CONV_W = 4
N_GROUPS = 4
EXPERTS_PER_GROUP = 4
N_EXPERTS = 16
D_EXPERT = 256
EPS = 1e-6
LANES = 128
VMEM_LIMIT = 56 * 1024 * 1024
MOE_ROW_BLOCK = 128
GDN_CHUNKS_PER_STEP = 8
GDN_CHUNKS_PER_ITER = 2

F32 = jnp.float32
BF16 = jnp.bfloat16
HIGHEST = lax.Precision.HIGHEST


def _pick_tile(total, pref):
    t = pref
    while total % t:
        t //= 2
    assert t >= CHUNK
    return t


def _const_spec(shape):
    nd = len(shape)
    return pl.BlockSpec(shape, lambda *_: (0,) * nd)


def _silu(x):
    return x * (1.0 / (1.0 + jnp.exp(-x)))


def _sigmoid(x):
    return 1.0 / (1.0 + jnp.exp(-x))


def _dot(a, b):
    return jnp.dot(a, b, preferred_element_type=F32)


def _dot_nt(a, b):
    return lax.dot_general(a, b, (((1,), (1,)), ((), ())), preferred_element_type=F32)


def _dot_tn(a, b):
    return lax.dot_general(a, b, (((0,), (0,)), ((), ())), preferred_element_type=F32)


def _dot_hi(a, b):
    return jnp.dot(a, b, preferred_element_type=F32, precision=HIGHEST)


def _mod_kernel(c_ref, w_ref, b_ref, o_ref):
    a = _silu(c_ref[...])
    rows = a.shape[0]
    a1 = a.astype(BF16)
    r1 = a - a1.astype(F32)
    a2 = r1.astype(BF16)
    stack = jnp.concatenate([a1, a2, (r1 - a2.astype(F32)).astype(BF16)], axis=0)
    w = w_ref[...]
    w_hi = w.astype(BF16)
    r = _dot(stack, w_hi) + _dot(stack, (w - w_hi.astype(F32)).astype(BF16))
    o_ref[...] = r[:rows] + r[rows:2 * rows] + r[2 * rows:] + b_ref[...]


def _mod_call(c_all, w_mod, b_mod):
    depth, d, d6 = w_mod.shape
    rows = c_all.shape[0]
    tn = 1024
    return pl.pallas_call(
        _mod_kernel,
        out_shape=jax.ShapeDtypeStruct((depth, rows, d6), F32),
        grid=(depth, d6 // tn),
        in_specs=[
            pl.BlockSpec((rows, d), lambda l, j: (0, 0)),
            pl.BlockSpec((None, d, tn), lambda l, j: (l, 0, j)),
            pl.BlockSpec((None, 1, tn), lambda l, j: (l, 0, j)),
        ],
        out_specs=pl.BlockSpec((None, rows, tn), lambda l, j: (l, 0, j)),
        compiler_params=pltpu.CompilerParams(vmem_limit_bytes=VMEM_LIMIT),
        name="mod",
    )(c_all, w_mod, b_mod.reshape(depth, 1, d6))


def _rope(x, cs):
    return (x * cs[:, :LANES]
            + pltpu.roll(x, 8, axis=1) * cs[:, LANES:2 * LANES]
            + pltpu.roll(x, LANES - 8, axis=1) * cs[:, 2 * LANES:])


def _x_specs(xs, tm):
    d = xs[0].shape[1]
    if len(xs) == 1:
        return [pl.BlockSpec((tm, d), lambda i, *_: (i, 0))]
    n_first = xs[0].shape[0] // tm
    assert xs[0].shape[0] % tm == 0 and xs[1].shape[0] % tm == 0
    return [pl.BlockSpec((tm, d), lambda i, *_: (jnp.minimum(i, n_first - 1), 0)),
            pl.BlockSpec((tm, d), lambda i, *_: (jnp.maximum(i - n_first, 0), 0))]


def _load_x(i, x_refs, n_first):
    if len(x_refs) == 1:
        return x_refs[0][...]
    return jnp.where(i < n_first, x_refs[0][...], x_refs[1][...])


def _chunk_rows(seq_ref, mod_ref, i, nch, col, d):
    return [mod_ref[pl.ds(seq_ref[i * nch + c], 1), col * d:(col + 1) * d] for c in range(nch)]


def _per_chunk(x, fn):
    nch = x.shape[0] // CHUNK
    return jnp.concatenate([fn(c, x[c * CHUNK:(c + 1) * CHUNK]) for c in range(nch)], axis=0)


def _inproj_kernel(seq_ref, *refs, n_x, n_first):
    x_refs, (mod_ref, nw_ref, cs_ref, rope_e_ref, rope_c_ref, wqkv_ref, wxb_ref, wz_ref, wba_ref, wga_ref, wgb_ref,
             q_ref, k_ref, v_ref, xb_ref, z_ref, ba_ref, ga_ref, gb_ref) = refs[:n_x], refs[n_x:]
    i = pl.program_id(0)
    x = _load_x(i, x_refs, n_first)
    tm, d = x.shape
    nch = tm // CHUNK
    xn = x * lax.rsqrt(jnp.mean(x * x, axis=-1, keepdims=True) + EPS) * nw_ref[...]
    shift = _chunk_rows(seq_ref, mod_ref, i, nch, 0, d)
    scale = _chunk_rows(seq_ref, mod_ref, i, nch, 1, d)
    h = _per_chunk(xn, lambda c, r: r * (1.0 + scale[c]) + shift[c]).astype(BF16)
    cs = _dot(_split3(cs_ref[...]), rope_e_ref[...]) + rope_c_ref[...]
    qkv = _dot(h, wqkv_ref[...])
    nq = N_HEADS_A * HD_A
    for g in range(nq // LANES):
        q_ref[:, g * LANES:(g + 1) * LANES] = _rope(qkv[:, g * LANES:(g + 1) * LANES], cs).astype(BF16)
    k_ref[...] = _rope(qkv[:, nq:nq + LANES], cs)
    v_ref[...] = qkv[:, nq + LANES:]
    xb_ref[...] = _dot(h, wxb_ref[...])
    z_ref[...] = _dot(h, wz_ref[...]).astype(BF16)
    ba_ref[...] = _dot(h, wba_ref[...])
    ga_ref[...] = _dot(h, wga_ref[...]).astype(BF16)
    gb_ref[...] = _dot(h, wgb_ref[...]).astype(BF16)


def _inproj_call(seq, xs, mod, norm_w, cs, rope_e, rope_c, wqkv, wxb, wz, wba, wga, wgb, tm):
    t = sum(x.shape[0] for x in xs)
    d = xs[0].shape[1]
    widths = [N_HEADS_A * HD_A, LANES, LANES, wxb.shape[1], wz.shape[1], LANES, d, d]
    dtypes = [BF16, F32, F32, F32, BF16, F32, BF16, BF16]
    tok = lambda w: pl.BlockSpec((tm, w), lambda i, s: (i, 0))
    return pl.pallas_call(
        functools.partial(_inproj_kernel, n_x=len(xs), n_first=xs[0].shape[0] // tm),
        out_shape=[jax.ShapeDtypeStruct((t, w), dt) for w, dt in zip(widths, dtypes)],
        grid_spec=pltpu.PrefetchScalarGridSpec(
            num_scalar_prefetch=1,
            grid=(t // tm,),
            in_specs=_x_specs(xs, tm) + [
                _const_spec(mod.shape),
                _const_spec((1, d)),
                tok(cs.shape[1]), _const_spec(rope_e.shape), _const_spec(rope_c.shape),
                _const_spec(wqkv.shape), _const_spec(wxb.shape), _const_spec(wz.shape),
                _const_spec(wba.shape), _const_spec(wga.shape), _const_spec(wgb.shape),
            ],
            out_specs=[tok(w) for w in widths],
        ),
        compiler_params=pltpu.CompilerParams(vmem_limit_bytes=VMEM_LIMIT),
        name="inproj",
    )(seq, *xs, mod, norm_w, cs, rope_e, rope_c, wqkv, wxb, wz, wba, wga, wgb)


def _attn_kernel(nvalid_ref, sink_ref, q_ref, kp_ref, kt_ref, vp_ref, vt_ref, *rest, cb, own_prev):
    o_ref = rest[-1]
    i = pl.program_id(0)
    nk = (WIN_CHUNKS + 1) * CHUNK
    npre = WIN_CHUNKS * CHUNK
    if own_prev:
        window = lambda pre, new: [jnp.concatenate(
            [pre[c * npre:(c + 1) * npre], new[c * CHUNK:(c + 1) * CHUNK]], axis=0).astype(BF16) for c in range(cb)]
    else:
        def window(pre, new):
            cat = jnp.concatenate([pre[...], new[...]], axis=0).astype(BF16)
            return [cat[c * CHUNK:c * CHUNK + nk] for c in range(cb)]
    kwin = window(kp_ref, kt_ref)
    vwin = window(vp_ref, vt_ref)
    key_chunk = lax.broadcasted_iota(jnp.int32, (CHUNK, nk), 1) // CHUNK
    group = N_HEADS_A // N_KV_A
    heads = range(N_HEADS_A)

    def scores(c):
        q = q_ref[c * CHUNK:(c + 1) * CHUNK, :]
        return [_dot_nt(q[:, h * HD_A:(h + 1) * HD_A], kwin[c][:, (h // group) * HD_A:(h // group + 1) * HD_A])
                for h in heads]

    s_next = scores(0)
    for c in range(cb):
        s_cur = s_next
        if c + 1 < cb:
            s_next = scores(c + 1)
        valid = key_chunk >= (WIN_CHUNKS - nvalid_ref[i * cb + c])
        p, den = [], []
        for h in heads:
            s = jnp.where(valid, s_cur[h] * (HD_A ** -0.5), -jnp.inf)
            sink = sink_ref[h]
            m = jnp.maximum(jnp.max(s, axis=-1, keepdims=True), sink)
            e = jnp.exp(s - m)
            p.append(e.astype(BF16))
            den.append(jnp.sum(e, axis=-1, keepdims=True) + jnp.exp(sink - m))
        o = [_dot(p[h], vwin[c][:, (h // group) * HD_A:(h // group + 1) * HD_A]) for h in heads]
        for h in heads:
            o_ref[c * CHUNK:(c + 1) * CHUNK, h * HD_A:(h + 1) * HD_A] = (o[h] / den[h]).astype(BF16)


def _attn_call(nvalid, sinks, q, k_prev, k_new, v_prev, v_new, cb, n_tiles, tile_off, prev_map, out_full=None,
               own_prev=False):
    kvw = N_KV_A * HD_A
    qw = N_HEADS_A * HD_A
    tile = lambda w: pl.BlockSpec((cb * CHUNK, w), lambda i, nv: (i + tile_off, 0))
    prev_rows = WIN_CHUNKS * CHUNK * (cb if own_prev else 1)
    prev = pl.BlockSpec((prev_rows, kvw), lambda i, nv: (prev_map(i), 0))
    args = [nvalid, sinks, q, k_prev, k_new, v_prev, v_new]
    in_specs = [pl.BlockSpec(memory_space=pltpu.SMEM), tile(qw), prev, tile(kvw), prev, tile(kvw)]
    aliases = {}
    if out_full is not None:
        aliases = {len(args): 0}
        args.append(out_full)
        in_specs.append(pl.BlockSpec(memory_space=pl.ANY))
    return pl.pallas_call(
        functools.partial(_attn_kernel, cb=cb, own_prev=own_prev),
        out_shape=jax.ShapeDtypeStruct((q.shape[0], qw), BF16),
        grid_spec=pltpu.PrefetchScalarGridSpec(
            num_scalar_prefetch=1,
            grid=(n_tiles,),
            in_specs=in_specs,
            out_specs=tile(qw),
        ),
        input_output_aliases=aliases,
        compiler_params=pltpu.CompilerParams(vmem_limit_bytes=VMEM_LIMIT),
        name="attn",
    )(*args)


def _l2norm(x):
    return x * lax.rsqrt(jnp.sum(x * x, axis=-1, keepdims=True) + EPS)


def _gdn_kernel(first_ref, last_ref, xb_ref, cinit_ref, sinit_ref, cw_ref, ba_ref, apar_ref, z_ref, nw_ref,
                *rest, cb):
    ob_ref, sout_ref, pad_ref, state_ref = rest[-4:]
    step = pl.program_id(0)

    nc = GDN_CHUNKS_PER_ITER if cb % GDN_CHUNKS_PER_ITER == 0 else 1

    def body(it, carry):
        _gdn_chunks(step * cb + it * nc, pl.multiple_of(it * (nc * CHUNK), nc * CHUNK), nc, first_ref, last_ref,
                    xb_ref, cinit_ref, sinit_ref, cw_ref, ba_ref, apar_ref, z_ref, nw_ref,
                    ob_ref, sout_ref, pad_ref, state_ref)
        return carry

    lax.fori_loop(0, cb // nc, body, 0)


def _gdn_chunks(i, row0, nc, first_ref, last_ref, xb_ref, cinit_ref, sinit_ref, cw_ref, ba_ref, apar_ref, z_ref,
                nw_ref, ob_ref, sout_ref, pad_ref, state_ref):
    c = CHUNK
    n = nc * c
    rows = pl.ds(row0, n)

    @pl.when(first_ref[i] == 1)
    def _():
        pad_ref[0:8, :] = cinit_ref[...]
        state_ref[...] = sinit_ref[...]

    pad_ref[8:8 + n, :] = xb_ref[rows, :]
    cw = cw_ref[...]
    conv = pad_ref[5:5 + n, :] * cw[0:1, :]
    for j in range(1, CONV_W):
        conv = conv + pad_ref[5 + j:5 + j + n, :] * cw[j:j + 1, :]
    pad_ref[0:8, :] = pad_ref[n:n + 8, :]
    qkv = _silu(conv)
    nqk = N_QK_B * DK_B

    ba = ba_ref[rows, :]
    apar = apar_ref[...]
    beta = _sigmoid(ba[:, 0:N_V_B])
    sp_in = ba + apar[1:2, :]
    softplus = jnp.maximum(sp_in, 0.0) + jnp.log(1.0 + jnp.exp(-jnp.abs(sp_in)))
    g_all = -jnp.exp(apar[0:1, :]) * softplus
    row = lax.broadcasted_iota(jnp.int32, (c, c), 0)
    col = lax.broadcasted_iota(jnp.int32, (c, c), 1)
    lower = (row >= col).astype(BF16)
    g1 = g_all.astype(BF16)
    r1 = g_all - g1.astype(F32)
    g2 = r1.astype(BF16)
    g3 = (r1 - g2.astype(F32)).astype(BF16)
    gsplit = jnp.concatenate([g1, g2, g3], axis=1)
    gc_all, gc_t = [], []
    for k in range(nc):
        gs = _dot(lower, gsplit[k * c:(k + 1) * c])
        gck = gs[:, :LANES] + gs[:, LANES:2 * LANES] + gs[:, 2 * LANES:]
        gc_all.append(gck)
        gc_t.append(gck.T)

    qn_all = [_l2norm(qkv[:, j * DK_B:(j + 1) * DK_B]) * (DK_B ** -0.5) for j in range(N_QK_B)]
    kn_all = [_l2norm(qkv[:, nqk + j * DK_B:nqk + (j + 1) * DK_B]) for j in range(N_QK_B)]
    pairs = [(k, j) for k in range(nc) for j in range(N_QK_B)]
    qn = {kj: qn_all[kj[1]][kj[0] * c:(kj[0] + 1) * c] for kj in pairs}
    kn = {kj: kn_all[kj[1]][kj[0] * c:(kj[0] + 1) * c] for kj in pairs}
    kt = {kj: kn[kj].T for kj in pairs}
    qkk = {kj: _dot(jnp.concatenate([qn[kj], kn[kj]], axis=0).astype(BF16), kt[kj].astype(BF16))
           for kj in pairs}

    rep = N_V_B // N_QK_B
    items = [(k, h) for k in range(nc) for h in range(N_V_B)]
    qk_of = lambda kh: (kh[0], kh[1] // rep)
    chunk_rows = lambda kh, x: x[kh[0] * c:(kh[0] + 1) * c]
    gcol = {kh: gc_all[kh[0]][:, N_V_B + kh[1]:N_V_B + kh[1] + 1] for kh in items}
    grow = {kh: gc_t[kh[0]][N_V_B + kh[1]:N_V_B + kh[1] + 1, :] for kh in items}
    bcol = {kh: chunk_rows(kh, beta)[:, kh[1]:kh[1] + 1] for kh in items}
    decay = {kh: jnp.exp(jnp.where(row >= col, gcol[kh] - grow[kh], -jnp.inf)) for kh in items}
    a = {kh: jnp.where(row > col, bcol[kh] * qkk[qk_of(kh)][c:] * decay[kh], 0.0) for kh in items}
    xm = {kh: -a[kh] for kh in items}
    p = {kh: _dot(a[kh].astype(BF16), a[kh].astype(BF16)) for kh in items}
    for _ in range(4):
        pb = {kh: p[kh].astype(BF16) for kh in items}
        r = {kh: _dot(jnp.concatenate([xm[kh].astype(BF16), pb[kh]], axis=0), pb[kh]) for kh in items}
        xm = {kh: xm[kh] + p[kh] + r[kh][:c] for kh in items}
        p = {kh: r[kh][c:] for kh in items}
    r = {kh: _dot(xm[kh].astype(BF16), p[kh].astype(BF16)) for kh in items}
    xm = {kh: xm[kh] + p[kh] + r[kh] for kh in items}
    egc = {kh: jnp.exp(gcol[kh]) for kh in items}
    rhs = {kh: jnp.concatenate(
        [chunk_rows(kh, qkv[:, 2 * nqk + kh[1] * DV_B:2 * nqk + (kh[1] + 1) * DV_B]) * bcol[kh],
         kn[qk_of(kh)] * (bcol[kh] * egc[kh])], axis=1) for kh in items}
    sol = {kh: rhs[kh] + _dot(xm[kh].astype(BF16), rhs[kh].astype(BF16)) for kh in items}
    wq = {kh: jnp.concatenate([sol[kh][:, DV_B:], qn[qk_of(kh)] * egc[kh]], axis=0).astype(BF16) for kh in items}
    a_qk = {kh: (qkk[qk_of(kh)][:c] * decay[kh]).astype(BF16) for kh in items}
    g_last = {kh: gcol[kh][c - 1:c, :] for kh in items}
    k_dec_t = {kh: (kt[qk_of(kh)] * jnp.exp(g_last[kh] - grow[kh])).astype(BF16) for kh in items}

    heads = range(N_V_B)
    state = [state_ref[h] for h in heads]
    nw = nw_ref[...]
    for k in range(nc):
        r = [_dot(wq[k, h], state[h].astype(BF16)) for h in heads]
        v_new = [(sol[k, h][:, :DV_B] - r[h][:c]).astype(BF16) for h in heads]
        state = [state[h] * jnp.exp(g_last[k, h]) + _dot(k_dec_t[k, h], v_new[h]) for h in heads]
        o = [r[h][c:] + _dot(a_qk[k, h], v_new[h]) for h in heads]
        out_rows = pl.ds(pl.multiple_of(row0 + k * c, c), c)
        for h in heads:
            on = o[h] * lax.rsqrt(jnp.mean(o[h] * o[h], axis=-1, keepdims=True) + EPS) * nw
            zh = z_ref[out_rows, h * DV_B:(h + 1) * DV_B].astype(F32)
            ob_ref[out_rows, h * DV_B:(h + 1) * DV_B] = (on * _silu(zh)).astype(BF16)
    for h in heads:
        state_ref[h] = state[h]

    @pl.when(last_ref[i + nc - 1] == 1)
    def _():
        sout_ref[...] = state_ref[...]


def _gdn_call(first, last, xb, conv_init, ssm_init, conv_w, ba, apar, z, norm_o, cb, n_steps, tile_off,
              steps_per_seq, out_full=None, ssm_off=0):
    t, cd = xb.shape
    nseq = conv_init.shape[0]
    vw = N_V_B * DV_B
    tile = lambda w: pl.BlockSpec((cb * CHUNK, w), lambda i, f, l: (i + tile_off, 0))
    const = lambda shape: pl.BlockSpec(shape, lambda i, f, l: (0,) * len(shape))
    state = pl.BlockSpec((None, N_V_B, DK_B, DV_B), lambda i, f, l: (i // steps_per_seq, 0, 0, 0))
    args = [first, last, xb, conv_init, ssm_init, conv_w, ba, apar, z, norm_o]
    in_specs = [
        tile(cd),
        pl.BlockSpec((None, 8, cd), lambda i, f, l: (i // steps_per_seq, 0, 0)),
        pl.BlockSpec((None, N_V_B, DK_B, DV_B), lambda i, f, l: (i // steps_per_seq + ssm_off, 0, 0, 0)),
        const((CONV_W, cd)),
        tile(LANES),
        const((2, LANES)),
        tile(vw),
        const((1, DV_B)),
    ]
    aliases = {}
    if out_full is not None:
        aliases = {len(args): 0}
        args.append(out_full)
        in_specs.append(pl.BlockSpec(memory_space=pl.ANY))
    return pl.pallas_call(
        functools.partial(_gdn_kernel, cb=cb),
        out_shape=[jax.ShapeDtypeStruct((t, vw), BF16),
                   jax.ShapeDtypeStruct((nseq, N_V_B, DK_B, DV_B), F32)],
        grid_spec=pltpu.PrefetchScalarGridSpec(
            num_scalar_prefetch=2,
            grid=(n_steps,),
            in_specs=in_specs,
            out_specs=[tile(vw), state],
            scratch_shapes=[pltpu.VMEM((GDN_CHUNKS_PER_ITER * CHUNK + 8, cd), F32),
                            pltpu.VMEM((N_V_B, DK_B, DV_B), F32)],
        ),
        input_output_aliases=aliases,
        compiler_params=pltpu.CompilerParams(dimension_semantics=("arbitrary",),
                                             vmem_limit_bytes=VMEM_LIMIT),
        name="gdn",
    )(*args)


def _outproj_kernel(seq_ref, *refs, n_x, n_first):
    x_refs, (oa_ref, ob_ref, ga_ref, gb_ref, mod_ref, woa_ref, wob_ref, wout_ref, nw_ref,
             wr_ref, br_ref, x1_ref, h2_ref, wexp_ref) = refs[:n_x], refs[n_x:]
    i = pl.program_id(0)
    x = _load_x(i, x_refs, n_first)
    tm, d = x.shape
    nch = tm // CHUNK
    ya = _dot(oa_ref[...], woa_ref[...])
    yb = _dot(ob_ref[...], wob_ref[...])
    merged = _sigmoid(ga_ref[...].astype(F32)) * ya + _sigmoid(gb_ref[...].astype(F32)) * yb
    mix = _dot(merged.astype(BF16), wout_ref[...])
    g1 = _chunk_rows(seq_ref, mod_ref, i, nch, 2, d)
    x1 = x + _per_chunk(mix, lambda c, r: g1[c] * r)
    x1_ref[...] = x1
    xn = x1 * lax.rsqrt(jnp.mean(x1 * x1, axis=-1, keepdims=True) + EPS) * nw_ref[...]
    shift = _chunk_rows(seq_ref, mod_ref, i, nch, 3, d)
    scale = _chunk_rows(seq_ref, mod_ref, i, nch, 4, d)
    h2 = _per_chunk(xn, lambda c, r: r * (1.0 + scale[c]) + shift[c])
    h2_hi = h2.astype(BF16)
    h2_ref[...] = h2_hi

    h2_lo = (h2 - h2_hi.astype(F32)).astype(BF16)
    wr = wr_ref[...]
    hi = _dot(h2_hi, wr)
    logits = hi[:, :LANES] + hi[:, LANES:] + _dot(h2_lo, wr[:, :LANES]) + br_ref[...]
    lane = lax.broadcasted_iota(jnp.int32, logits.shape, 1)
    neg = -jnp.inf
    lg = jnp.where(lane < N_GROUPS, logits, neg)
    mx = jnp.max(lg, axis=-1, keepdims=True)
    gi = jnp.min(jnp.where(lg == mx, lane, LANES), axis=-1, keepdims=True)
    p_group = 1.0 / jnp.sum(jnp.exp(lg - mx), axis=-1, keepdims=True)
    lo = N_GROUPS + EXPERTS_PER_GROUP * gi
    le = jnp.where((lane >= lo) & (lane < lo + EXPERTS_PER_GROUP), logits, neg)
    v1 = jnp.max(le, axis=-1, keepdims=True)
    i1 = jnp.min(jnp.where(le == v1, lane, LANES), axis=-1, keepdims=True)
    le2 = jnp.where(lane == i1, neg, le)
    v2 = jnp.max(le2, axis=-1, keepdims=True)
    i2 = jnp.min(jnp.where(le2 == v2, lane, LANES), axis=-1, keepdims=True)
    e2 = jnp.exp(v2 - v1)
    w1 = p_group / (1.0 + e2)
    w2 = p_group * e2 / (1.0 + e2)
    wexp_ref[...] = jnp.where(lane == i1, w1, 0.0) + jnp.where(lane == i2, w2, 0.0)


def _outproj_call(seq, xs, oa, ob, ga, gb, mod, woa, wob, wout, norm_w, wr, br, tm):
    t = sum(x.shape[0] for x in xs)
    d = xs[0].shape[1]
    tok = lambda w: pl.BlockSpec((tm, w), lambda i, s: (i, 0))
    return pl.pallas_call(
        functools.partial(_outproj_kernel, n_x=len(xs), n_first=xs[0].shape[0] // tm),
        out_shape=[jax.ShapeDtypeStruct((t, d), F32), jax.ShapeDtypeStruct((t, d), BF16),
                   jax.ShapeDtypeStruct((t, LANES), F32)],
        grid_spec=pltpu.PrefetchScalarGridSpec(
            num_scalar_prefetch=1,
            grid=(t // tm,),
            in_specs=_x_specs(xs, tm) + [
                tok(oa.shape[1]), tok(ob.shape[1]), tok(d), tok(d),
                _const_spec(mod.shape),
                _const_spec(woa.shape), _const_spec(wob.shape), _const_spec(wout.shape),
                _const_spec((1, d)), _const_spec(wr.shape), _const_spec((1, LANES)),
            ],
            out_specs=[tok(d), tok(d), tok(LANES)],
        ),
        compiler_params=pltpu.CompilerParams(vmem_limit_bytes=VMEM_LIMIT),
        name="outproj",
    )(seq, *xs, oa, ob, ga, gb, mod, woa, wob, wout, norm_w, wr, br)


def _split3(x):
    x1 = x.astype(BF16)
    r1 = x - x1.astype(F32)
    x2 = r1.astype(BF16)
    return jnp.concatenate([x1, x2, (r1 - x2.astype(F32)).astype(BF16)], axis=1)


def _moe_kernel(seq_ref, h_ref, wexp_ref, x1_ref, mod_ref, wg_ref, wu_ref, wd_ref, *rest, n_first):
    xs_ref, ws_ref, ys_ref, pmt_ref, meta_ref = rest[-5:]
    outs = rest[:-5]
    i = pl.program_id(0)
    g = pl.program_id(1)
    tm, d = x1_ref.shape
    npos = xs_ref.shape[0]
    rb = MOE_ROW_BLOCK
    epg = wg_ref.shape[0]

    @pl.when(g == 0)
    def _():
        wexp = wexp_ref[...]
        lane = lax.broadcasted_iota(jnp.int32, (tm, LANES), 1)
        sel = jnp.where(wexp > 0.0, 1.0, 0.0).astype(BF16)
        wl = lax.broadcasted_iota(jnp.int32, (LANES, LANES), 0) - N_GROUPS
        gl = lax.broadcasted_iota(jnp.int32, (LANES, LANES), 1)
        in_group = (wl >= gl * epg) & (wl < (gl + 1) * epg) & (gl < N_GROUPS)
        onehot = jnp.where(_dot(sel, jnp.where(in_group, 1.0, 0.0).astype(BF16)) > 0.0, 1.0, 0.0)
        before = lax.broadcasted_iota(jnp.int32, (tm, tm), 0) > lax.broadcasted_iota(jnp.int32, (tm, tm), 1)
        rank = _dot(jnp.where(before, 1.0, 0.0).astype(BF16), onehot.astype(BF16))
        cnt = rank[tm - 1:tm, :] + onehot[tm - 1:tm, :]
        nblk = jnp.floor((cnt + (rb - 1)) * (1.0 / rb))
        base_row = jnp.zeros((1, LANES), F32)
        start = jnp.zeros((), F32)
        for grp in range(N_GROUPS):
            nb = nblk[0, grp]
            meta_ref[grp] = start.astype(jnp.int32)
            meta_ref[N_GROUPS + grp] = nb.astype(jnp.int32)
            base_row = base_row + jnp.where(lane[0:1, :] == grp, start, 0.0)
            start = start + nb * rb
        pos_col = jnp.sum(onehot * (rank + base_row), axis=-1, keepdims=True)
        pmt_ref[...] = jnp.where(pos_col.astype(jnp.int32) == lax.broadcasted_iota(jnp.int32, (tm, npos), 1),
                                 1.0, 0.0).astype(BF16)
        pos_row = jnp.transpose(jnp.broadcast_to(pos_col, (tm, LANES)))[0:1, :].astype(jnp.int32)
        q = LANES // 4
        hw = jnp.concatenate([h_ref[...], _split3(wexp[:, :q]), jnp.zeros((tm, q), BF16)], axis=1)
        used = start.astype(jnp.int32)
        pblk = 256
        for pb in range(npos // pblk):
            @pl.when(pb * pblk < used)
            def _():
                slot = lax.broadcasted_iota(jnp.int32, (pblk, tm), 0) + pb * pblk
                pm = jnp.where(slot == pos_row, 1.0, 0.0).astype(BF16)
                r = _dot(pm, hw)
                xs_ref[pb * pblk:(pb + 1) * pblk, :] = r[:, :d].astype(BF16)
                ws = r[:, d:d + q] + r[:, d + q:d + 2 * q] + r[:, d + 2 * q:d + 3 * q]
                ws_ref[pb * pblk:(pb + 1) * pblk, :] = jnp.concatenate(
                    [ws, jnp.zeros((pblk, LANES - q), F32)], axis=1)
        ys_ref[...] = jnp.zeros_like(ys_ref)

    start = meta_ref[g]

    def run_experts(r0, nrows):
        rows = pl.ds(pl.multiple_of(r0, rb), nrows)
        x = xs_ref[rows, :]
        w = ws_ref[rows, :]
        lane = lax.broadcasted_iota(jnp.int32, (nrows, LANES), 1)
        experts = range(epg)
        hg = [_dot(x, wg_ref[j]) for j in experts]
        hu = [_dot(x, wu_ref[j]) for j in experts]
        we = [jnp.sum(jnp.where(lane == N_GROUPS + g * epg + j, w, 0.0), axis=-1, keepdims=True) for j in experts]
        act = [(_silu(hg[j]) * hu[j] * we[j]).astype(BF16) for j in experts]
        out = _dot(act[0], wd_ref[0])
        for j in experts[1:]:
            out = out + _dot(act[j], wd_ref[j])
        ys_ref[rows, :] = out.astype(BF16)

    nblocks = meta_ref[N_GROUPS + g]
    npairs = lax.shift_right_logical(nblocks, 1)

    def pair(b, carry):
        run_experts(start + b * (2 * rb), 2 * rb)
        return carry

    lax.fori_loop(0, npairs, pair, 0)

    @pl.when(nblocks - 2 * npairs == 1)
    def _():
        run_experts(start + npairs * (2 * rb), rb)

    @pl.when(g == pl.num_programs(1) - 1)
    def _():
        nch = tm // CHUNK
        moe = _dot(pmt_ref[...], ys_ref[...])
        g2 = _chunk_rows(seq_ref, mod_ref, i, nch, 5, d)
        x2 = x1_ref[...] + _per_chunk(moe, lambda c, r: g2[c] * r)
        if len(outs) == 1:
            outs[0][...] = x2
        else:
            nw_ref, y_prompt_ref, y_sample_ref = outs
            y = x2 * lax.rsqrt(jnp.mean(x2 * x2, axis=-1, keepdims=True) + EPS) * nw_ref[...]

            @pl.when(i < n_first)
            def _():
                y_prompt_ref[...] = y

            @pl.when(i >= n_first)
            def _():
                y_sample_ref[...] = y


def _moe_call(seq, h2, wexp, x1, mod, wg, wu, wd, layer, tm, final=None):
    t, d = x1.shape
    de = wg.shape[2]
    w_off = layer * N_GROUPS
    epb = EXPERTS_PER_GROUP
    npos = -(-(tm + N_GROUPS * MOE_ROW_BLOCK) // 256) * 256
    tok =lambda w: pl.BlockSpec((tm, w), lambda i, e, s: (i, 0))
    in_specs = [
        tok(d), tok(LANES), tok(d),
        _const_spec(mod.shape),
        pl.BlockSpec((epb, d, de), lambda i, e, s: (w_off + e, 0, 0)),
        pl.BlockSpec((epb, d, de), lambda i, e, s: (w_off + e, 0, 0)),
        pl.BlockSpec((epb, de, d), lambda i, e, s: (w_off + e, 0, 0)),
    ]
    args = [seq, h2, wexp, x1, mod, wg, wu, wd]
    if final is None:
        n_first = 0
        out_shape = jax.ShapeDtypeStruct((t, d), F32)
        out_specs = tok(d)
    else:
        norm_w, tp = final
        assert tp % tm == 0 and (t - tp) % tm == 0
        n_first = tp // tm
        in_specs.append(_const_spec((1, d)))
        args.append(norm_w)
        out_shape = [jax.ShapeDtypeStruct((tp, d), F32), jax.ShapeDtypeStruct((t - tp, d), F32)]
        out_specs = [pl.BlockSpec((tm, d), lambda i, e, s: (jnp.minimum(i, n_first - 1), 0)),
                     pl.BlockSpec((tm, d), lambda i, e, s: (jnp.maximum(i - n_first, 0), 0))]
    return pl.pallas_call(
        functools.partial(_moe_kernel, n_first=n_first),
        out_shape=out_shape,
        grid_spec=pltpu.PrefetchScalarGridSpec(
            num_scalar_prefetch=1,
            grid=(t // tm, N_GROUPS),
            in_specs=in_specs,
            out_specs=out_specs,
            scratch_shapes=[pltpu.VMEM((npos, d), BF16), pltpu.VMEM((npos, LANES), F32),
                            pltpu.VMEM((npos, d), BF16), pltpu.VMEM((tm, npos), BF16),
                            pltpu.SMEM((2 * N_GROUPS,), jnp.int32)],
        ),
        compiler_params=pltpu.CompilerParams(dimension_semantics=("arbitrary", "arbitrary"),
                                             vmem_limit_bytes=VMEM_LIMIT),
        name="moe",
    )(*args)


def _rope_table(pos):
    half = ROPE_DIM // 2
    inv = ROPE_THETA ** (-jnp.arange(half, dtype=F32) / half)
    ang = pos.astype(F32)[:, None] * inv[None, :]
    return jnp.concatenate([jnp.cos(ang), jnp.sin(ang)], axis=1)


def _rope_expansion():
    half = ROPE_DIM // 2
    e = np.zeros((2 * half, 3 * LANES), np.float32)
    c = np.zeros((1, 3 * LANES), np.float32)
    for lane in range(LANES):
        dim = lane % HD_A
        if dim < half:
            e[dim, lane] = 1.0
            e[half + dim, 2 * LANES + lane] = -1.0
        elif dim < ROPE_DIM:
            e[dim - half, lane] = 1.0
            e[dim, LANES + lane] = 1.0
        else:
            c[0, lane] = 1.0
    return jnp.asarray(np.tile(e, (3, 1)), BF16), jnp.asarray(c)


def kernel(x_prompt, x_sample, cache_k_a, cache_v_a, state_conv_b, state_ssm_b, c_prompt, c_sample, w_mod, b_mod, norm_mix, w_in, sinks_a, w_o_a, conv_b, a_log_b, dt_bias_b, norm_o_b, w_o_b, w_out, norm_ffn, router_g, router_g_b, router_e, router_e_b, w_gate_e, w_up_e, w_down_e, norm_final):
    bp, seq, d = x_prompt.shape
    bs, dseq, _ = x_sample.shape
    depth = w_mod.shape[0]
    assert seq % CHUNK == 0 and dseq == CHUNK and d % LANES == 0
    assert cache_k_a.shape[2] == WIN_CHUNKS * CHUNK
    npc = seq // CHUNK
    tp, ts = bp * seq, bs * dseq
    t = tp + ts
    n_chunks = t // CHUNK
    nseq = bp + bs
    kvw = N_KV_A * HD_A
    cd = 2 * N_QK_B * DK_B + N_V_B * DV_B
    vw = N_V_B * DV_B
    nq = N_HEADS_A * HD_A

    seq_np = np.concatenate([np.repeat(np.arange(bp), npc), bp + np.arange(bs)]).astype(np.int32)
    local_np = np.concatenate([np.tile(np.arange(npc), bp), np.zeros(bs, np.int64)])
    first_np = (local_np == 0).astype(np.int32)
    last_np = np.concatenate([np.tile(np.arange(npc) == npc - 1, bp), np.ones(bs, bool)]).astype(np.int32)
    nvalid_np = np.concatenate([np.minimum(np.tile(np.arange(npc), bp), WIN_CHUNKS),
                                np.full(bs, WIN_CHUNKS)]).astype(np.int32)
    seq_i = jnp.asarray(seq_np)
    first_p, last_p = jnp.asarray(first_np[:bp * npc]), jnp.asarray(last_np[:bp * npc])
    nvalid_p, nvalid_s = jnp.asarray(nvalid_np[:bp * npc]), jnp.asarray(nvalid_np[bp * npc:])
    cb = 16 if npc % 16 == 0 else (4 if npc % 4 == 0 else 2)
    assert npc % cb == 0
    cbg = GDN_CHUNKS_PER_STEP if npc % GDN_CHUNKS_PER_STEP == 0 else cb
    cbs = 4 if bs % 4 == 0 and tp % (4 * CHUNK) == 0 else 1

    pos = jnp.concatenate([jnp.tile(jnp.arange(seq), bp), jnp.tile(PAST_LEN + jnp.arange(dseq), bs)])
    cs = _rope_table(pos)
    rope_e, rope_c = _rope_expansion()

    xs = [x_prompt.reshape(tp, d), x_sample.reshape(ts, d)]
    c_all = jnp.concatenate([c_prompt, c_sample], axis=0)
    c_rows = -(-nseq // 16) * 16
    c_all = jnp.pad(c_all, ((0, c_rows - nseq), (0, 0)))
    mod = _mod_call(c_all, w_mod, b_mod)

    tm = _pick_tile(np.gcd(tp, ts), 256)
    tm_out = _pick_tile(np.gcd(tp, ts), 512)
    tm_moe = _pick_tile(np.gcd(tp, ts), 1024)
    sizes = np.cumsum([0, nq, kvw, kvw, cd, vw, N_V_B, N_V_B, d, d])
    wg_all = w_gate_e.astype(BF16).reshape(depth * N_EXPERTS, d, D_EXPERT)
    wu_all = w_up_e.astype(BF16).reshape(depth * N_EXPERTS, d, D_EXPERT)
    wd_all = w_down_e.astype(BF16).reshape(depth * N_EXPERTS, D_EXPERT, d)
    ssm_in = state_ssm_b.reshape(depth * bs, N_V_B, DK_B, DV_B)
    outs = {k: [] for k in ("k_p", "v_p", "c_p", "s_p", "k_s", "v_s", "c_s", "s_s")}
    for l in range(depth):
        wl = w_in[l]
        seg = lambda a, b: wl[:, sizes[a]:sizes[b]].astype(BF16)
        wqkv, wxb, wz = seg(0, 3), seg(3, 4), seg(4, 5)
        wba = jnp.pad(seg(5, 7), ((0, 0), (0, LANES - 2 * N_V_B)))
        wga, wgb = seg(7, 8), seg(8, 9)
        q, k, v, xb, z, ba, ga, gb = _inproj_call(
            seq_i, xs, mod[l], norm_mix[l].reshape(1, d), cs, rope_e, rope_c, wqkv, wxb, wz, wba, wga, wgb, tm)

        oa = _attn_call(nvalid_p, sinks_a[l], q, k, k, v, v, cb, tp // (cb * CHUNK), 0,
                        lambda i: jnp.maximum(i * (cb // WIN_CHUNKS) - 1, 0))
        oa = _attn_call(nvalid_s, sinks_a[l], q, cache_k_a[l].reshape(bs * WIN_CHUNKS * CHUNK, kvw), k,
                        cache_v_a[l].reshape(bs * WIN_CHUNKS * CHUNK, kvw), v, cbs, bs // cbs,
                        tp // (cbs * CHUNK), lambda i: i, out_full=oa, own_prev=True)

        apar = jnp.zeros((2, LANES), F32)
        apar = apar.at[0, N_V_B:2 * N_V_B].set(a_log_b[l]).at[1, N_V_B:2 * N_V_B].set(dt_bias_b[l])
        nw_o = norm_o_b[l].reshape(1, DV_B)
        ob, ssm_p = _gdn_call(first_p, last_p, xb, jnp.zeros((bp, 8, cd), F32),
                              jnp.zeros((bp, N_V_B, DK_B, DV_B), F32), conv_b[l], ba, apar, z, nw_o,
                              cbg, bp * npc // cbg, 0, npc // cbg)
        conv_init = jnp.pad(state_conv_b[l], ((0, 0), (8 - (CONV_W - 1), 0), (0, 0)))
        ones_s = jnp.ones((bs,), jnp.int32)
        ob, ssm_s = _gdn_call(ones_s, ones_s, xb, conv_init, ssm_in, conv_b[l], ba, apar, z, nw_o,
                              1, bs, tp // CHUNK, 1, out_full=ob, ssm_off=l * bs)

        wr = jnp.concatenate([router_g[l], jnp.transpose(router_e[l], (1, 0, 2)).reshape(d, N_EXPERTS)], axis=1)
        wr = jnp.pad(wr, ((0, 0), (0, LANES - wr.shape[1])))
        wr_hi = wr.astype(BF16)
        wr = jnp.concatenate([wr_hi, (wr - wr_hi.astype(F32)).astype(BF16)], axis=1)
        br = jnp.concatenate([router_g_b[l], router_e_b[l].reshape(-1)])
        br = jnp.pad(br, (0, LANES - br.shape[0])).reshape(1, LANES)
        x1, h2, wexp = _outproj_call(seq_i, xs, oa, ob, ga, gb, mod[l], w_o_a[l].astype(BF16),
                                     w_o_b[l].astype(BF16), w_out[l].astype(BF16), norm_ffn[l].reshape(1, d),
                                     wr, br, tm_out)
        final =(norm_final.reshape(1, d), tp) if l == depth - 1 else None
        x_next = _moe_call(seq_i, h2, wexp, x1, mod[l], wg_all, wu_all, wd_all, l, tm_moe, final)
        xs = [x_next] if final is None else x_next

        keep = min(WIN_CHUNKS * CHUNK, seq)
        ends = [(b + 1) * seq for b in range(bp)]
        tail = lambda a, n: jnp.stack([a[e - n:e] for e in ends])
        outs["k_p"].append(tail(k, keep).reshape(bp, keep, N_KV_A, HD_A))
        outs["v_p"].append(tail(v, keep).reshape(bp, keep, N_KV_A, HD_A))
        outs["k_s"].append(k[tp:].reshape(bs, dseq, N_KV_A, HD_A))
        outs["v_s"].append(v[tp:].reshape(bs, dseq, N_KV_A, HD_A))
        outs["c_p"].append(tail(xb, CONV_W - 1))
        outs["c_s"].append(jnp.stack([xb[tp + (b + 1) * dseq - (CONV_W - 1):tp + (b + 1) * dseq]
                                      for b in range(bs)]))
        outs["s_p"].append(ssm_p)
        outs["s_s"].append(ssm_s)

    st = lambda key: jnp.stack(outs[key])
    return (xs[0].reshape(bp, seq, d), xs[1].reshape(bs, dseq, d),
            st("k_p"), st("v_p"), st("c_p"), st("s_p"), st("k_s"), st("v_s"), st("c_s"), st("s_s"))
```

```python
import functools

import numpy as np
import jax
import jax.numpy as jnp
from jax import lax
from jax.experimental import pallas as pl
from jax.experimental.pallas import tpu as pltpu

CHUNK = 64
PAST_LEN = 1024
N_HEADS_A = 8
N_KV_A = 2
HD_A = 64
WIN_CHUNKS = 2
ROPE_DIM = 16
ROPE_THETA = 500000.0
N_QK_B = 4
N_V_B = 8
DK_B = 128
DV_B = 128
CONV_W = 4
N_GROUPS = 4
EXPERTS_PER_GROUP = 4
N_EXPERTS = 16
D_EXPERT = 256
EPS = 1e-6
LANES = 128
VMEM_LIMIT = 56 * 1024 * 1024
MOE_ROW_BLOCK = 128
GDN_CHUNKS_PER_STEP = 8
GDN_CHUNKS_PER_ITER = 2

F32 = jnp.float32
BF16 = jnp.bfloat16
HIGHEST = lax.Precision.HIGHEST


def _pick_tile(total, pref):
    t = pref
    while total % t:
        t //= 2
    assert t >= CHUNK
    return t


def _const_spec(shape):
    nd = len(shape)
    return pl.BlockSpec(shape, lambda *_: (0,) * nd)


def _silu(x):
    return x * (1.0 / (1.0 + jnp.exp(-x)))


def _sigmoid(x):
    return 1.0 / (1.0 + jnp.exp(-x))


def _dot(a, b):
    return jnp.dot(a, b, preferred_element_type=F32)


def _dot_nt(a, b):
    return lax.dot_general(a, b, (((1,), (1,)), ((), ())), preferred_element_type=F32)


def _dot_tn(a, b):
    return lax.dot_general(a, b, (((0,), (0,)), ((), ())), preferred_element_type=F32)


def _dot_hi(a, b):
    return jnp.dot(a, b, preferred_element_type=F32, precision=HIGHEST)


def _mod_kernel(c_ref, w_ref, b_ref, o_ref):
    a = _silu(c_ref[...])
    rows = a.shape[0]
    a1 = a.astype(BF16)
    r1 = a - a1.astype(F32)
    a2 = r1.astype(BF16)
    stack = jnp.concatenate([a1, a2, (r1 - a2.astype(F32)).astype(BF16)], axis=0)
    w = w_ref[...]
    w_hi = w.astype(BF16)
    r = _dot(stack, w_hi) + _dot(stack, (w - w_hi.astype(F32)).astype(BF16))
    o_ref[...] = r[:rows] + r[rows:2 * rows] + r[2 * rows:] + b_ref[...]


def _mod_call(c_all, w_mod, b_mod):
    depth, d, d6 = w_mod.shape
    rows = c_all.shape[0]
    tn = 1024
    return pl.pallas_call(
        _mod_kernel,
        out_shape=jax.ShapeDtypeStruct((depth, rows, d6), F32),
        grid=(depth, d6 // tn),
        in_specs=[
            pl.BlockSpec((rows, d), lambda l, j: (0, 0)),
            pl.BlockSpec((None, d, tn), lambda l, j: (l, 0, j)),
            pl.BlockSpec((None, 1, tn), lambda l, j: (l, 0, j)),
        ],
        out_specs=pl.BlockSpec((None, rows, tn), lambda l, j: (l, 0, j)),
        compiler_params=pltpu.CompilerParams(vmem_limit_bytes=VMEM_LIMIT),
        name="mod",
    )(c_all, w_mod, b_mod.reshape(depth, 1, d6))


def _rope(x, cs):
    return (x * cs[:, :LANES]
            + pltpu.roll(x, 8, axis=1) * cs[:, LANES:2 * LANES]
            + pltpu.roll(x, LANES - 8, axis=1) * cs[:, 2 * LANES:])


def _x_specs(xs, tm):
    d = xs[0].shape[1]
    if len(xs) == 1:
        return [pl.BlockSpec((tm, d), lambda i, *_: (i, 0))]
    n_first = xs[0].shape[0] // tm
    assert xs[0].shape[0] % tm == 0 and xs[1].shape[0] % tm == 0
    return [pl.BlockSpec((tm, d), lambda i, *_: (jnp.minimum(i, n_first - 1), 0)),
            pl.BlockSpec((tm, d), lambda i, *_: (jnp.maximum(i - n_first, 0), 0))]


def _load_x(i, x_refs, n_first):
    if len(x_refs) == 1:
        return x_refs[0][...]
    return jnp.where(i < n_first, x_refs[0][...], x_refs[1][...])


def _chunk_rows(seq_ref, mod_ref, i, nch, col, d):
    return [mod_ref[pl.ds(seq_ref[i * nch + c], 1), col * d:(col + 1) * d] for c in range(nch)]


def _per_chunk(x, fn):
    nch = x.shape[0] // CHUNK
    return jnp.concatenate([fn(c, x[c * CHUNK:(c + 1) * CHUNK]) for c in range(nch)], axis=0)


def _inproj_kernel(seq_ref, *refs, n_x, n_first):
    x_refs, (mod_ref, nw_ref, cs_ref, rope_e_ref, rope_c_ref, wqkv_ref, wxb_ref, wz_ref, wba_ref, wga_ref, wgb_ref,
             q_ref, k_ref, v_ref, xb_ref, z_ref, ba_ref, ga_ref, gb_ref) = refs[:n_x], refs[n_x:]
    i = pl.program_id(0)
    x = _load_x(i, x_refs, n_first)
    tm, d = x.shape
    nch = tm // CHUNK
    xn = x * lax.rsqrt(jnp.mean(x * x, axis=-1, keepdims=True) + EPS) * nw_ref[...]
    shift = _chunk_rows(seq_ref, mod_ref, i, nch, 0, d)
    scale = _chunk_rows(seq_ref, mod_ref, i, nch, 1, d)
    h = _per_chunk(xn, lambda c, r: r * (1.0 + scale[c]) + shift[c]).astype(BF16)
    cs = _dot(_split3(cs_ref[...]), rope_e_ref[...]) + rope_c_ref[...]
    qkv = _dot(h, wqkv_ref[...])
    nq = N_HEADS_A * HD_A
    for g in range(nq // LANES):
        q_ref[:, g * LANES:(g + 1) * LANES] = _rope(qkv[:, g * LANES:(g + 1) * LANES], cs).astype(BF16)
    k_ref[...] = _rope(qkv[:, nq:nq + LANES], cs)
    v_ref[...] = qkv[:, nq + LANES:]
    xb_ref[...] = _dot(h, wxb_ref[...])
    z_ref[...] = _dot(h, wz_ref[...]).astype(BF16)
    ba_ref[...] = _dot(h, wba_ref[...])
    ga_ref[...] = _dot(h, wga_ref[...]).astype(BF16)
    gb_ref[...] = _dot(h, wgb_ref[...]).astype(BF16)


def _inproj_call(seq, xs, mod, norm_w, cs, rope_e, rope_c, wqkv, wxb, wz, wba, wga, wgb, tm):
    t = sum(x.shape[0] for x in xs)
    d = xs[0].shape[1]
    widths = [N_HEADS_A * HD_A, LANES, LANES, wxb.shape[1], wz.shape[1], LANES, d, d]
    dtypes = [BF16, F32, F32, F32, BF16, F32, BF16, BF16]
    tok = lambda w: pl.BlockSpec((tm, w), lambda i, s: (i, 0))
    return pl.pallas_call(
        functools.partial(_inproj_kernel, n_x=len(xs), n_first=xs[0].shape[0] // tm),
        out_shape=[jax.ShapeDtypeStruct((t, w), dt) for w, dt in zip(widths, dtypes)],
        grid_spec=pltpu.PrefetchScalarGridSpec(
            num_scalar_prefetch=1,
            grid=(t // tm,),
            in_specs=_x_specs(xs, tm) + [
                _const_spec(mod.shape),
                _const_spec((1, d)),
                tok(cs.shape[1]), _const_spec(rope_e.shape), _const_spec(rope_c.shape),
                _const_spec(wqkv.shape), _const_spec(wxb.shape), _const_spec(wz.shape),
                _const_spec(wba.shape), _const_spec(wga.shape), _const_spec(wgb.shape),
            ],
            out_specs=[tok(w) for w in widths],
        ),
        compiler_params=pltpu.CompilerParams(vmem_limit_bytes=VMEM_LIMIT),
        name="inproj",
    )(seq, *xs, mod, norm_w, cs, rope_e, rope_c, wqkv, wxb, wz, wba, wga, wgb)


def _attn_kernel(nvalid_ref, sink_ref, q_ref, kp_ref, kt_ref, vp_ref, vt_ref, *rest, cb, own_prev):
    o_ref = rest[-1]
    i = pl.program_id(0)
    nk = (WIN_CHUNKS + 1) * CHUNK
    npre = WIN_CHUNKS * CHUNK
    if own_prev:
        window = lambda pre, new: [jnp.concatenate(
            [pre[c * npre:(c + 1) * npre], new[c * CHUNK:(c + 1) * CHUNK]], axis=0).astype(BF16) for c in range(cb)]
    else:
        def window(pre, new):
            cat = jnp.concatenate([pre[...], new[...]], axis=0).astype(BF16)
            return [cat[c * CHUNK:c * CHUNK + nk] for c in range(cb)]
    kwin = window(kp_ref, kt_ref)
    vwin = window(vp_ref, vt_ref)
    key_chunk = lax.broadcasted_iota(jnp.int32, (CHUNK, nk), 1) // CHUNK
    group = N_HEADS_A // N_KV_A
    heads = range(N_HEADS_A)

    def scores(c):
        q = q_ref[c * CHUNK:(c + 1) * CHUNK, :]
        return [_dot_nt(q[:, h * HD_A:(h + 1) * HD_A], kwin[c][:, (h // group) * HD_A:(h // group + 1) * HD_A])
                for h in heads]

    s_next = scores(0)
    for c in range(cb):
        s_cur = s_next
        if c + 1 < cb:
            s_next = scores(c + 1)
        valid = key_chunk >= (WIN_CHUNKS - nvalid_ref[i * cb + c])
        p, den = [], []
        for h in heads:
            s = jnp.where(valid, s_cur[h] * (HD_A ** -0.5), -jnp.inf)
            sink = sink_ref[h]
            m = jnp.maximum(jnp.max(s, axis=-1, keepdims=True), sink)
            e = jnp.exp(s - m)
            p.append(e.astype(BF16))
            den.append(jnp.sum(e, axis=-1, keepdims=True) + jnp.exp(sink - m))
        o = [_dot(p[h], vwin[c][:, (h // group) * HD_A:(h // group + 1) * HD_A]) for h in heads]
        for h in heads:
            o_ref[c * CHUNK:(c + 1) * CHUNK, h * HD_A:(h + 1) * HD_A] = (o[h] / den[h]).astype(BF16)


def _attn_call(nvalid, sinks, q, k_prev, k_new, v_prev, v_new, cb, n_tiles, tile_off, prev_map, out_full=None,
               own_prev=False):
    kvw = N_KV_A * HD_A
    qw = N_HEADS_A * HD_A
    tile = lambda w: pl.BlockSpec((cb * CHUNK, w), lambda i, nv: (i + tile_off, 0))
    prev_rows = WIN_CHUNKS * CHUNK * (cb if own_prev else 1)
    prev = pl.BlockSpec((prev_rows, kvw), lambda i, nv: (prev_map(i), 0))
    args = [nvalid, sinks, q, k_prev, k_new, v_prev, v_new]
    in_specs = [pl.BlockSpec(memory_space=pltpu.SMEM), tile(qw), prev, tile(kvw), prev, tile(kvw)]
    aliases = {}
    if out_full is not None:
        aliases = {len(args): 0}
        args.append(out_full)
        in_specs.append(pl.BlockSpec(memory_space=pl.ANY))
    return pl.pallas_call(
        functools.partial(_attn_kernel, cb=cb, own_prev=own_prev),
        out_shape=jax.ShapeDtypeStruct((q.shape[0], qw), BF16),
        grid_spec=pltpu.PrefetchScalarGridSpec(
            num_scalar_prefetch=1,
            grid=(n_tiles,),
            in_specs=in_specs,
            out_specs=tile(qw),
        ),
        input_output_aliases=aliases,
        compiler_params=pltpu.CompilerParams(vmem_limit_bytes=VMEM_LIMIT),
        name="attn",
    )(*args)


def _l2norm(x):
    return x * lax.rsqrt(jnp.sum(x * x, axis=-1, keepdims=True) + EPS)


def _gdn_kernel(first_ref, last_ref, xb_ref, cinit_ref, sinit_ref, cw_ref, ba_ref, apar_ref, z_ref, nw_ref,
                *rest, cb):
    ob_ref, sout_ref, pad_ref, state_ref = rest[-4:]
    step = pl.program_id(0)

    nc = GDN_CHUNKS_PER_ITER if cb % GDN_CHUNKS_PER_ITER == 0 else 1

    def body(it, carry):
        _gdn_chunks(step * cb + it * nc, pl.multiple_of(it * (nc * CHUNK), nc * CHUNK), nc, first_ref, last_ref,
                    xb_ref, cinit_ref, sinit_ref, cw_ref, ba_ref, apar_ref, z_ref, nw_ref,
                    ob_ref, sout_ref, pad_ref, state_ref)
        return carry

    lax.fori_loop(0, cb // nc, body, 0)


def _gdn_chunks(i, row0, nc, first_ref, last_ref, xb_ref, cinit_ref, sinit_ref, cw_ref, ba_ref, apar_ref, z_ref,
                nw_ref, ob_ref, sout_ref, pad_ref, state_ref):
    c = CHUNK
    n = nc * c
    rows = pl.ds(row0, n)

    @pl.when(first_ref[i] == 1)
    def _():
        pad_ref[0:8, :] = cinit_ref[...]
        state_ref[...] = sinit_ref[...]

    pad_ref[8:8 + n, :] = xb_ref[rows, :]
    cw = cw_ref[...]
    conv = pad_ref[5:5 + n, :] * cw[0:1, :]
    for j in range(1, CONV_W):
        conv = conv + pad_ref[5 + j:5 + j + n, :] * cw[j:j + 1, :]
    pad_ref[0:8, :] = pad_ref[n:n + 8, :]
    qkv = _silu(conv)
    nqk = N_QK_B * DK_B

    ba = ba_ref[rows, :]
    apar = apar_ref[...]
    beta = _sigmoid(ba[:, 0:N_V_B])
    sp_in = ba + apar[1:2, :]
    softplus = jnp.maximum(sp_in, 0.0) + jnp.log(1.0 + jnp.exp(-jnp.abs(sp_in)))
    g_all = -jnp.exp(apar[0:1, :]) * softplus
    row = lax.broadcasted_iota(jnp.int32, (c, c), 0)
    col = lax.broadcasted_iota(jnp.int32, (c, c), 1)
    lower = (row >= col).astype(BF16)
    g1 = g_all.astype(BF16)
    r1 = g_all - g1.astype(F32)
    g2 = r1.astype(BF16)
    g3 = (r1 - g2.astype(F32)).astype(BF16)
    gsplit = jnp.concatenate([g1, g2, g3], axis=1)
    gc_all, gc_t = [], []
    for k in range(nc):
        gs = _dot(lower, gsplit[k * c:(k + 1) * c])
        gck = gs[:, :LANES] + gs[:, LANES:2 * LANES] + gs[:, 2 * LANES:]
        gc_all.append(gck)
        gc_t.append(gck.T)

    qn_all = [_l2norm(qkv[:, j * DK_B:(j + 1) * DK_B]) * (DK_B ** -0.5) for j in range(N_QK_B)]
    kn_all = [_l2norm(qkv[:, nqk + j * DK_B:nqk + (j + 1) * DK_B]) for j in range(N_QK_B)]
    pairs = [(k, j) for k in range(nc) for j in range(N_QK_B)]
    qn = {kj: qn_all[kj[1]][kj[0] * c:(kj[0] + 1) * c] for kj in pairs}
    kn = {kj: kn_all[kj[1]][kj[0] * c:(kj[0] + 1) * c] for kj in pairs}
    kt = {kj: kn[kj].T for kj in pairs}
    qkk = {kj: _dot(jnp.concatenate([qn[kj], kn[kj]], axis=0).astype(BF16), kt[kj].astype(BF16))
           for kj in pairs}

    rep = N_V_B // N_QK_B
    items = [(k, h) for k in range(nc) for h in range(N_V_B)]
    qk_of = lambda kh: (kh[0], kh[1] // rep)
    chunk_rows = lambda kh, x: x[kh[0] * c:(kh[0] + 1) * c]
    gcol = {kh: gc_all[kh[0]][:, N_V_B + kh[1]:N_V_B + kh[1] + 1] for kh in items}
    grow = {kh: gc_t[kh[0]][N_V_B + kh[1]:N_V_B + kh[1] + 1, :] for kh in items}
    bcol = {kh: chunk_rows(kh, beta)[:, kh[1]:kh[1] + 1] for kh in items}
    decay = {kh: jnp.exp(jnp.where(row >= col, gcol[kh] - grow[kh], -jnp.inf)) for kh in items}
    a = {kh: jnp.where(row > col, bcol[kh] * qkk[qk_of(kh)][c:] * decay[kh], 0.0) for kh in items}
    xm = {kh: -a[kh] for kh in items}
    p = {kh: _dot(a[kh].astype(BF16), a[kh].astype(BF16)) for kh in items}
    for _ in range(4):
        pb = {kh: p[kh].astype(BF16) for kh in items}
        r = {kh: _dot(jnp.concatenate([xm[kh].astype(BF16), pb[kh]], axis=0), pb[kh]) for kh in items}
        xm = {kh: xm[kh] + p[kh] + r[kh][:c] for kh in items}
        p = {kh: r[kh][c:] for kh in items}
    r = {kh: _dot(xm[kh].astype(BF16), p[kh].astype(BF16)) for kh in items}
    xm = {kh: xm[kh] + p[kh] + r[kh] for kh in items}
    egc = {kh: jnp.exp(gcol[kh]) for kh in items}
    rhs = {kh: jnp.concatenate(
        [chunk_rows(kh, qkv[:, 2 * nqk + kh[1] * DV_B:2 * nqk + (kh[1] + 1) * DV_B]) * bcol[kh],
         kn[qk_of(kh)] * (bcol[kh] * egc[kh])], axis=1) for kh in items}
    sol = {kh: rhs[kh] + _dot(xm[kh].astype(BF16), rhs[kh].astype(BF16)) for kh in items}
    wq = {kh: jnp.concatenate([sol[kh][:, DV_B:], qn[qk_of(kh)] * egc[kh]], axis=0).astype(BF16) for kh in items}
    a_qk = {kh: (qkk[qk_of(kh)][:c] * decay[kh]).astype(BF16) for kh in items}
    g_last = {kh: gcol[kh][c - 1:c, :] for kh in items}
    k_dec_t = {kh: (kt[qk_of(kh)] * jnp.exp(g_last[kh] - grow[kh])).astype(BF16) for kh in items}

    heads = range(N_V_B)
    state = [state_ref[h] for h in heads]
    nw = nw_ref[...]
    for k in range(nc):
        r = [_dot(wq[k, h], state[h].astype(BF16)) for h in heads]
        v_new = [(sol[k, h][:, :DV_B] - r[h][:c]).astype(BF16) for h in heads]
        state = [state[h] * jnp.exp(g_last[k, h]) + _dot(k_dec_t[k, h], v_new[h]) for h in heads]
        o = [r[h][c:] + _dot(a_qk[k, h], v_new[h]) for h in heads]
        out_rows = pl.ds(pl.multiple_of(row0 + k * c, c), c)
        for h in heads:
            on = o[h] * lax.rsqrt(jnp.mean(o[h] * o[h], axis=-1, keepdims=True) + EPS) * nw
            zh = z_ref[out_rows, h * DV_B:(h + 1) * DV_B].astype(F32)
            ob_ref[out_rows, h * DV_B:(h + 1) * DV_B] = (on * _silu(zh)).astype(BF16)
    for h in heads:
        state_ref[h] = state[h]

    @pl.when(last_ref[i + nc - 1] == 1)
    def _():
        sout_ref[...] = state_ref[...]


def _gdn_call(first, last, xb, conv_init, ssm_init, conv_w, ba, apar, z, norm_o, cb, n_steps, tile_off,
              steps_per_seq, out_full=None, ssm_off=0):
    t, cd = xb.shape
    nseq = conv_init.shape[0]
    vw = N_V_B * DV_B
    tile = lambda w: pl.BlockSpec((cb * CHUNK, w), lambda i, f, l: (i + tile_off, 0))
    const = lambda shape: pl.BlockSpec(shape, lambda i, f, l: (0,) * len(shape))
    state = pl.BlockSpec((None, N_V_B, DK_B, DV_B), lambda i, f, l: (i // steps_per_seq, 0, 0, 0))
    args = [first, last, xb, conv_init, ssm_init, conv_w, ba, apar, z, norm_o]
    in_specs = [
        tile(cd),
        pl.BlockSpec((None, 8, cd), lambda i, f, l: (i // steps_per_seq, 0, 0)),
        pl.BlockSpec((None, N_V_B, DK_B, DV_B), lambda i, f, l: (i // steps_per_seq + ssm_off, 0, 0, 0)),
        const((CONV_W, cd)),
        tile(LANES),
        const((2, LANES)),
        tile(vw),
        const((1, DV_B)),
    ]
    aliases = {}
    if out_full is not None:
        aliases = {len(args): 0}
        args.append(out_full)
        in_specs.append(pl.BlockSpec(memory_space=pl.ANY))
    return pl.pallas_call(
        functools.partial(_gdn_kernel, cb=cb),
        out_shape=[jax.ShapeDtypeStruct((t, vw), BF16),
                   jax.ShapeDtypeStruct((nseq, N_V_B, DK_B, DV_B), F32)],
        grid_spec=pltpu.PrefetchScalarGridSpec(
            num_scalar_prefetch=2,
            grid=(n_steps,),
            in_specs=in_specs,
            out_specs=[tile(vw), state],
            scratch_shapes=[pltpu.VMEM((GDN_CHUNKS_PER_ITER * CHUNK + 8, cd), F32),
                            pltpu.VMEM((N_V_B, DK_B, DV_B), F32)],
        ),
        input_output_aliases=aliases,
        compiler_params=pltpu.CompilerParams(dimension_semantics=("arbitrary",),
                                             vmem_limit_bytes=VMEM_LIMIT),
        name="gdn",
    )(*args)


def _outproj_kernel(seq_ref, *refs, n_x, n_first):
    x_refs, (oa_ref, ob_ref, ga_ref, gb_ref, mod_ref, woa_ref, wob_ref, wout_ref, nw_ref,
             wr_ref, br_ref, x1_ref, h2_ref, wexp_ref) = refs[:n_x], refs[n_x:]
    i = pl.program_id(0)
    x = _load_x(i, x_refs, n_first)
    tm, d = x.shape
    nch = tm // CHUNK
    ya = _dot(oa_ref[...], woa_ref[...])
    yb = _dot(ob_ref[...], wob_ref[...])
    merged = _sigmoid(ga_ref[...].astype(F32)) * ya + _sigmoid(gb_ref[...].astype(F32)) * yb
    mix = _dot(merged.astype(BF16), wout_ref[...])
    g1 = _chunk_rows(seq_ref, mod_ref, i, nch, 2, d)
    x1 = x + _per_chunk(mix, lambda c, r: g1[c] * r)
    x1_ref[...] = x1
    xn = x1 * lax.rsqrt(jnp.mean(x1 * x1, axis=-1, keepdims=True) + EPS) * nw_ref[...]
    shift = _chunk_rows(seq_ref, mod_ref, i, nch, 3, d)
    scale = _chunk_rows(seq_ref, mod_ref, i, nch, 4, d)
    h2 = _per_chunk(xn, lambda c, r: r * (1.0 + scale[c]) + shift[c])
    h2_hi = h2.astype(BF16)
    h2_ref[...] = h2_hi

    h2_lo = (h2 - h2_hi.astype(F32)).astype(BF16)
    wr = wr_ref[...]
    hi = _dot(h2_hi, wr)
    logits = hi[:, :LANES] + hi[:, LANES:] + _dot(h2_lo, wr[:, :LANES]) + br_ref[...]
    lane = lax.broadcasted_iota(jnp.int32, logits.shape, 1)
    neg = -jnp.inf
    lg = jnp.where(lane < N_GROUPS, logits, neg)
    mx = jnp.max(lg, axis=-1, keepdims=True)
    gi = jnp.min(jnp.where(lg == mx, lane, LANES), axis=-1, keepdims=True)
    p_group = 1.0 / jnp.sum(jnp.exp(lg - mx), axis=-1, keepdims=True)
    lo = N_GROUPS + EXPERTS_PER_GROUP * gi
    le = jnp.where((lane >= lo) & (lane < lo + EXPERTS_PER_GROUP), logits, neg)
    v1 = jnp.max(le, axis=-1, keepdims=True)
    i1 = jnp.min(jnp.where(le == v1, lane, LANES), axis=-1, keepdims=True)
    le2 = jnp.where(lane == i1, neg, le)
    v2 = jnp.max(le2, axis=-1, keepdims=True)
    i2 = jnp.min(jnp.where(le2 == v2, lane, LANES), axis=-1, keepdims=True)
    e2 = jnp.exp(v2 - v1)
    w1 = p_group / (1.0 + e2)
    w2 = p_group * e2 / (1.0 + e2)
    wexp_ref[...] = jnp.where(lane == i1, w1, 0.0) + jnp.where(lane == i2, w2, 0.0)


def _outproj_call(seq, xs, oa, ob, ga, gb, mod, woa, wob, wout, norm_w, wr, br, tm):
    t = sum(x.shape[0] for x in xs)
    d = xs[0].shape[1]
    tok = lambda w: pl.BlockSpec((tm, w), lambda i, s: (i, 0))
    return pl.pallas_call(
        functools.partial(_outproj_kernel, n_x=len(xs), n_first=xs[0].shape[0] // tm),
        out_shape=[jax.ShapeDtypeStruct((t, d), F32), jax.ShapeDtypeStruct((t, d), BF16),
                   jax.ShapeDtypeStruct((t, LANES), F32)],
        grid_spec=pltpu.PrefetchScalarGridSpec(
            num_scalar_prefetch=1,
            grid=(t // tm,),
            in_specs=_x_specs(xs, tm) + [
                tok(oa.shape[1]), tok(ob.shape[1]), tok(d), tok(d),
                _const_spec(mod.shape),
                _const_spec(woa.shape), _const_spec(wob.shape), _const_spec(wout.shape),
                _const_spec((1, d)), _const_spec(wr.shape), _const_spec((1, LANES)),
            ],
            out_specs=[tok(d), tok(d), tok(LANES)],
        ),
        compiler_params=pltpu.CompilerParams(vmem_limit_bytes=VMEM_LIMIT),
        name="outproj",
    )(seq, *xs, oa, ob, ga, gb, mod, woa, wob, wout, norm_w, wr, br)


def _split3(x):
    x1 = x.astype(BF16)
    r1 = x - x1.astype(F32)
    x2 = r1.astype(BF16)
    return jnp.concatenate([x1, x2, (r1 - x2.astype(F32)).astype(BF16)], axis=1)


def _moe_kernel(seq_ref, h_ref, wexp_ref, x1_ref, mod_ref, wg_ref, wu_ref, wd_ref, *rest, n_first):
    xs_ref, ws_ref, ys_ref, pmt_ref, meta_ref = rest[-5:]
    outs = rest[:-5]
    i = pl.program_id(0)
    g = pl.program_id(1)
    tm, d = x1_ref.shape
    npos = xs_ref.shape[0]
    rb = MOE_ROW_BLOCK
    epg = wg_ref.shape[0]

    @pl.when(g == 0)
    def _():
        wexp = wexp_ref[...]
        lane = lax.broadcasted_iota(jnp.int32, (tm, LANES), 1)
        sel = jnp.where(wexp > 0.0, 1.0, 0.0).astype(BF16)
        wl = lax.broadcasted_iota(jnp.int32, (LANES, LANES), 0) - N_GROUPS
        gl = lax.broadcasted_iota(jnp.int32, (LANES, LANES), 1)
        in_group = (wl >= gl * epg) & (wl < (gl + 1) * epg) & (gl < N_GROUPS)
        onehot = jnp.where(_dot(sel, jnp.where(in_group, 1.0, 0.0).astype(BF16)) > 0.0, 1.0, 0.0)
        before = (lax.broadcasted_iota(jnp.int32, (LANES, LANES), 0)
                  > lax.broadcasted_iota(jnp.int32, (LANES, LANES), 1))
        before = jnp.where(before, 1.0, 0.0).astype(BF16)
        onehot_b = onehot.astype(BF16)
        cnt = jnp.zeros((1, LANES), F32)
        parts = []
        for blk in range(tm // LANES):
            rows_b = slice(blk * LANES, (blk + 1) * LANES)
            parts.append(_dot(before, onehot_b[rows_b]) + cnt)
            cnt = cnt + jnp.sum(onehot[rows_b], axis=0, keepdims=True)
        rank = jnp.concatenate(parts, axis=0)
        nblk = jnp.floor((cnt + (rb - 1)) * (1.0 / rb))
        base_row = jnp.zeros((1, LANES), F32)
        start = jnp.zeros((), F32)
        for grp in range(N_GROUPS):
            nb = nblk[0, grp]
            meta_ref[grp] = start.astype(jnp.int32)
            meta_ref[N_GROUPS + grp] = nb.astype(jnp.int32)
            base_row = base_row + jnp.where(lane[0:1, :] == grp, start, 0.0)
            start = start + nb * rb
        pos_col = jnp.sum(onehot * (rank + base_row), axis=-1, keepdims=True)
        pmt_ref[...] = jnp.where(pos_col.astype(jnp.int32) == lax.broadcasted_iota(jnp.int32, (tm, npos), 1),
                                 1.0, 0.0).astype(BF16)
        pos_row = jnp.transpose(jnp.broadcast_to(pos_col, (tm, LANES)))[0:1, :].astype(jnp.int32)
        q = LANES // 4
        hw = jnp.concatenate([h_ref[...], _split3(wexp[:, :q]), jnp.zeros((tm, q), BF16)], axis=1)
        used = start.astype(jnp.int32)
        pblk = 256
        for pb in range(npos // pblk):
            @pl.when(pb * pblk < used)
            def _():
                slot = lax.broadcasted_iota(jnp.int32, (pblk, tm), 0) + pb * pblk
                pm = jnp.where(slot == pos_row, 1.0, 0.0).astype(BF16)
                r = _dot(pm, hw)
                xs_ref[pb * pblk:(pb + 1) * pblk, :] = r[:, :d].astype(BF16)
                ws = r[:, d:d + q] + r[:, d + q:d + 2 * q] + r[:, d + 2 * q:d + 3 * q]
                ws_ref[pb * pblk:(pb + 1) * pblk, :] = jnp.concatenate(
                    [ws, jnp.zeros((pblk, LANES - q), F32)], axis=1)
        ys_ref[...] = jnp.zeros_like(ys_ref)

    start = meta_ref[g]

    def run_experts(r0, nrows):
        rows = pl.ds(pl.multiple_of(r0, rb), nrows)
        x = xs_ref[rows, :]
        w = ws_ref[rows, :]
        lane = lax.broadcasted_iota(jnp.int32, (nrows, LANES), 1)
        experts = range(epg)
        hg = [_dot(x, wg_ref[j]) for j in experts]
        hu = [_dot(x, wu_ref[j]) for j in experts]
        we = [jnp.sum(jnp.where(lane == N_GROUPS + g * epg + j, w, 0.0), axis=-1, keepdims=True) for j in experts]
        act = [(_silu(hg[j]) * hu[j] * we[j]).astype(BF16) for j in experts]
        out = _dot(act[0], wd_ref[0])
        for j in experts[1:]:
            out = out + _dot(act[j], wd_ref[j])
        ys_ref[rows, :] = out.astype(BF16)

    nblocks = meta_ref[N_GROUPS + g]
    npairs = lax.shift_right_logical(nblocks, 1)

    def pair(b, carry):
        run_experts(start + b * (2 * rb), 2 * rb)
        return carry

    lax.fori_loop(0, npairs, pair, 0)

    @pl.when(nblocks - 2 * npairs == 1)
    def _():
        run_experts(start + npairs * (2 * rb), rb)

    @pl.when(g == pl.num_programs(1) - 1)
    def _():
        nch = tm // CHUNK
        moe = _dot(pmt_ref[...], ys_ref[...])
        g2 = _chunk_rows(seq_ref, mod_ref, i, nch, 5, d)
        x2 = x1_ref[...] + _per_chunk(moe, lambda c, r: g2[c] * r)
        if len(outs) == 1:
            outs[0][...] = x2
        else:
            nw_ref, y_prompt_ref, y_sample_ref = outs
            y = x2 * lax.rsqrt(jnp.mean(x2 * x2, axis=-1, keepdims=True) + EPS) * nw_ref[...]

            @pl.when(i < n_first)
            def _():
                y_prompt_ref[...] = y

            @pl.when(i >= n_first)
            def _():
                y_sample_ref[...] = y


def _moe_call(seq, h2, wexp, x1, mod, wg, wu, wd, layer, tm, final=None):
    t, d = x1.shape
    de = wg.shape[2]
    assert tm % LANES == 0
    w_off = layer * N_GROUPS
    epb = EXPERTS_PER_GROUP
    npos = -(-(tm + N_GROUPS * MOE_ROW_BLOCK) // 256) * 256
    tok =lambda w: pl.BlockSpec((tm, w), lambda i, e, s: (i, 0))
    in_specs = [
        tok(d), tok(LANES), tok(d),
        _const_spec(mod.shape),
        pl.BlockSpec((epb, d, de), lambda i, e, s: (w_off + e, 0, 0)),
        pl.BlockSpec((epb, d, de), lambda i, e, s: (w_off + e, 0, 0)),
        pl.BlockSpec((epb, de, d), lambda i, e, s: (w_off + e, 0, 0)),
    ]
    args = [seq, h2, wexp, x1, mod, wg, wu, wd]
    if final is None:
        n_first = 0
        out_shape = jax.ShapeDtypeStruct((t, d), F32)
        out_specs = tok(d)
    else:
        norm_w, tp = final
        assert tp % tm == 0 and (t - tp) % tm == 0
        n_first = tp // tm
        in_specs.append(_const_spec((1, d)))
        args.append(norm_w)
        out_shape = [jax.ShapeDtypeStruct((tp, d), F32), jax.ShapeDtypeStruct((t - tp, d), F32)]
        out_specs = [pl.BlockSpec((tm, d), lambda i, e, s: (jnp.minimum(i, n_first - 1), 0)),
                     pl.BlockSpec((tm, d), lambda i, e, s: (jnp.maximum(i - n_first, 0), 0))]
    return pl.pallas_call(
        functools.partial(_moe_kernel, n_first=n_first),
        out_shape=out_shape,
        grid_spec=pltpu.PrefetchScalarGridSpec(
            num_scalar_prefetch=1,
            grid=(t // tm, N_GROUPS),
            in_specs=in_specs,
            out_specs=out_specs,
            scratch_shapes=[pltpu.VMEM((npos, d), BF16), pltpu.VMEM((npos, LANES), F32),
                            pltpu.VMEM((npos, d), BF16), pltpu.VMEM((tm, npos), BF16),
                            pltpu.SMEM((2 * N_GROUPS,), jnp.int32)],
        ),
        compiler_params=pltpu.CompilerParams(dimension_semantics=("arbitrary", "arbitrary"),
                                             vmem_limit_bytes=VMEM_LIMIT),
        name="moe",
    )(*args)


def _rope_table(pos):
    half = ROPE_DIM // 2
    inv = ROPE_THETA ** (-jnp.arange(half, dtype=F32) / half)
    ang = pos.astype(F32)[:, None] * inv[None, :]
    return jnp.concatenate([jnp.cos(ang), jnp.sin(ang)], axis=1)


def _rope_expansion():
    half = ROPE_DIM // 2
    e = np.zeros((2 * half, 3 * LANES), np.float32)
    c = np.zeros((1, 3 * LANES), np.float32)
    for lane in range(LANES):
        dim = lane % HD_A
        if dim < half:
            e[dim, lane] = 1.0
            e[half + dim, 2 * LANES + lane] = -1.0
        elif dim < ROPE_DIM:
            e[dim - half, lane] = 1.0
            e[dim, LANES + lane] = 1.0
        else:
            c[0, lane] = 1.0
    return jnp.asarray(np.tile(e, (3, 1)), BF16), jnp.asarray(c)


def kernel(x_prompt, x_sample, cache_k_a, cache_v_a, state_conv_b, state_ssm_b, c_prompt, c_sample, w_mod, b_mod, norm_mix, w_in, sinks_a, w_o_a, conv_b, a_log_b, dt_bias_b, norm_o_b, w_o_b, w_out, norm_ffn, router_g, router_g_b, router_e, router_e_b, w_gate_e, w_up_e, w_down_e, norm_final):
    bp, seq, d = x_prompt.shape
    bs, dseq, _ = x_sample.shape
    depth = w_mod.shape[0]
    assert seq % CHUNK == 0 and dseq == CHUNK and d % LANES == 0
    assert cache_k_a.shape[2] == WIN_CHUNKS * CHUNK
    npc = seq // CHUNK
    tp, ts = bp * seq, bs * dseq
    t = tp + ts
    n_chunks = t // CHUNK
    nseq = bp + bs
    kvw = N_KV_A * HD_A
    cd = 2 * N_QK_B * DK_B + N_V_B * DV_B
    vw = N_V_B * DV_B
    nq = N_HEADS_A * HD_A

    seq_np = np.concatenate([np.repeat(np.arange(bp), npc), bp + np.arange(bs)]).astype(np.int32)
    local_np = np.concatenate([np.tile(np.arange(npc), bp), np.zeros(bs, np.int64)])
    first_np = (local_np == 0).astype(np.int32)
    last_np = np.concatenate([np.tile(np.arange(npc) == npc - 1, bp), np.ones(bs, bool)]).astype(np.int32)
    nvalid_np = np.concatenate([np.minimum(np.tile(np.arange(npc), bp), WIN_CHUNKS),
                                np.full(bs, WIN_CHUNKS)]).astype(np.int32)
    seq_i = jnp.asarray(seq_np)
    first_p, last_p = jnp.asarray(first_np[:bp * npc]), jnp.asarray(last_np[:bp * npc])
    nvalid_p, nvalid_s = jnp.asarray(nvalid_np[:bp * npc]), jnp.asarray(nvalid_np[bp * npc:])
    cb = 16 if npc % 16 == 0 else (4 if npc % 4 == 0 else 2)
    assert npc % cb == 0
    cbg = GDN_CHUNKS_PER_STEP if npc % GDN_CHUNKS_PER_STEP == 0 else cb
    cbs = 4 if bs % 4 == 0 and tp % (4 * CHUNK) == 0 else 1

    pos = jnp.concatenate([jnp.tile(jnp.arange(seq), bp), jnp.tile(PAST_LEN + jnp.arange(dseq), bs)])
    cs = _rope_table(pos)
    rope_e, rope_c = _rope_expansion()

    xs = [x_prompt.reshape(tp, d), x_sample.reshape(ts, d)]
    c_all = jnp.concatenate([c_prompt, c_sample], axis=0)
    c_rows = -(-nseq // 16) * 16
    c_all = jnp.pad(c_all, ((0, c_rows - nseq), (0, 0)))
    mod = _mod_call(c_all, w_mod, b_mod)

    tm = _pick_tile(np.gcd(tp, ts), 256)
    tm_out = _pick_tile(np.gcd(tp, ts), 512)
    tm_moe = _pick_tile(np.gcd(tp, ts), 1024)
    sizes = np.cumsum([0, nq, kvw, kvw, cd, vw, N_V_B, N_V_B, d, d])
    wg_all = w_gate_e.astype(BF16).reshape(depth * N_EXPERTS, d, D_EXPERT)
    wu_all = w_up_e.astype(BF16).reshape(depth * N_EXPERTS, d, D_EXPERT)
    wd_all = w_down_e.astype(BF16).reshape(depth * N_EXPERTS, D_EXPERT, d)
    ssm_in = state_ssm_b.reshape(depth * bs, N_V_B, DK_B, DV_B)
    outs = {k: [] for k in ("k_p", "v_p", "c_p", "s_p", "k_s", "v_s", "c_s", "s_s")}
    for l in range(depth):
        wl = w_in[l]
        seg = lambda a, b: wl[:, sizes[a]:sizes[b]].astype(BF16)
        wqkv, wxb, wz = seg(0, 3), seg(3, 4), seg(4, 5)
        wba = jnp.pad(seg(5, 7), ((0, 0), (0, LANES - 2 * N_V_B)))
        wga, wgb = seg(7, 8), seg(8, 9)
        q, k, v, xb, z, ba, ga, gb = _inproj_call(
            seq_i, xs, mod[l], norm_mix[l].reshape(1, d), cs, rope_e, rope_c, wqkv, wxb, wz, wba, wga, wgb, tm)

        oa = _attn_call(nvalid_p, sinks_a[l], q, k, k, v, v, cb, tp // (cb * CHUNK), 0,
                        lambda i: jnp.maximum(i * (cb // WIN_CHUNKS) - 1, 0))
        oa = _attn_call(nvalid_s, sinks_a[l], q, cache_k_a[l].reshape(bs * WIN_CHUNKS * CHUNK, kvw), k,
                        cache_v_a[l].reshape(bs * WIN_CHUNKS * CHUNK, kvw), v, cbs, bs // cbs,
                        tp // (cbs * CHUNK), lambda i: i, out_full=oa, own_prev=True)

        apar = jnp.zeros((2, LANES), F32)
        apar = apar.at[0, N_V_B:2 * N_V_B].set(a_log_b[l]).at[1, N_V_B:2 * N_V_B].set(dt_bias_b[l])
        nw_o = norm_o_b[l].reshape(1, DV_B)
        ob, ssm_p = _gdn_call(first_p, last_p, xb, jnp.zeros((bp, 8, cd), F32),
                              jnp.zeros((bp, N_V_B, DK_B, DV_B), F32), conv_b[l], ba, apar, z, nw_o,
                              cbg, bp * npc // cbg, 0, npc // cbg)
        conv_init = jnp.pad(state_conv_b[l], ((0, 0), (8 - (CONV_W - 1), 0), (0, 0)))
        ones_s = jnp.ones((bs,), jnp.int32)
        ob, ssm_s = _gdn_call(ones_s, ones_s, xb, conv_init, ssm_in, conv_b[l], ba, apar, z, nw_o,
                              1, bs, tp // CHUNK, 1, out_full=ob, ssm_off=l * bs)

        wr = jnp.concatenate([router_g[l], jnp.transpose(router_e[l], (1, 0, 2)).reshape(d, N_EXPERTS)], axis=1)
        wr = jnp.pad(wr, ((0, 0), (0, LANES - wr.shape[1])))
        wr_hi = wr.astype(BF16)
        wr = jnp.concatenate([wr_hi, (wr - wr_hi.astype(F32)).astype(BF16)], axis=1)
        br = jnp.concatenate([router_g_b[l], router_e_b[l].reshape(-1)])
        br = jnp.pad(br, (0, LANES - br.shape[0])).reshape(1, LANES)
        x1, h2, wexp = _outproj_call(seq_i, xs, oa, ob, ga, gb, mod[l], w_o_a[l].astype(BF16),
                                     w_o_b[l].astype(BF16), w_out[l].astype(BF16), norm_ffn[l].reshape(1, d),
                                     wr, br, tm_out)
        final =(norm_final.reshape(1, d), tp) if l == depth - 1 else None
        x_next = _moe_call(seq_i, h2, wexp, x1, mod[l], wg_all, wu_all, wd_all, l, tm_moe, final)
        xs = [x_next] if final is None else x_next

        keep = min(WIN_CHUNKS * CHUNK, seq)
        ends = [(b + 1) * seq for b in range(bp)]
        tail = lambda a, n: jnp.stack([a[e - n:e] for e in ends])
        outs["k_p"].append(tail(k, keep).reshape(bp, keep, N_KV_A, HD_A))
        outs["v_p"].append(tail(v, keep).reshape(bp, keep, N_KV_A, HD_A))
        outs["k_s"].append(k[tp:].reshape(bs, dseq, N_KV_A, HD_A))
        outs["v_s"].append(v[tp:].reshape(bs, dseq, N_KV_A, HD_A))
        outs["c_p"].append(tail(xb, CONV_W - 1))
        outs["c_s"].append(jnp.stack([xb[tp + (b + 1) * dseq - (CONV_W - 1):tp + (b + 1) * dseq]
                                      for b in range(bs)]))
        outs["s_p"].append(ssm_p)
        outs["s_s"].append(ssm_s)

    st = lambda key: jnp.stack(outs[key])
    return (xs[0].reshape(bp, seq, d), xs[1].reshape(bs, dseq, d),
            st("k_p"), st("v_p"), st("c_p"), st("s_p"), st("k_s"), st("v_s"), st("c_s"), st("s_s"))
```

```python
import functools

import numpy as np
import jax
import jax.numpy as jnp
from jax import lax
from jax.experimental import pallas as pl
from jax.experimental.pallas import tpu as pltpu

CHUNK = 64
PAST_LEN = 1024
N_HEADS_A = 8
N_KV_A = 2
HD_A = 64
WIN_CHUNKS = 2
ROPE_DIM = 16
ROPE_THETA = 500000.0
N_QK_B = 4
N_V_B = 8
DK_B = 128
DV_B = 128
CONV_W = 4
N_GROUPS = 4
EXPERTS_PER_GROUP = 4
N_EXPERTS = 16
D_EXPERT = 256
EPS = 1e-6
LANES = 128
VMEM_LIMIT = 56 * 1024 * 1024
MOE_ROW_BLOCK = 128
MOE_PERMUTE_BLOCK = 256
GDN_CHUNKS_PER_STEP = 16
GDN_CHUNKS_PER_ITER = 2

F32 = jnp.float32
BF16 = jnp.bfloat16


def _pick_tile(total, pref):
    t = pref
    while total % t:
        t //= 2
    assert t >= CHUNK
    return t


def _const_spec(shape):
    nd = len(shape)
    return pl.BlockSpec(shape, lambda *_: (0,) * nd, pipeline_mode=pl.Buffered(1))


def _silu(x):
    return x * (1.0 / (1.0 + jnp.exp(-x)))


def _sigmoid(x):
    return 1.0 / (1.0 + jnp.exp(-x))


def _dot(a, b):
    return jnp.dot(a, b, preferred_element_type=F32)


def _dot_nt(a, b):
    return lax.dot_general(a, b, (((1,), (1,)), ((), ())), preferred_element_type=F32)


def _mod_kernel(c_ref, w_ref, b_ref, o_ref):
    a = _silu(c_ref[...])
    rows = a.shape[0]
    a1 = a.astype(BF16)
    r1 = a - a1.astype(F32)
    a2 = r1.astype(BF16)
    stack = jnp.concatenate([a1, a2, (r1 - a2.astype(F32)).astype(BF16)], axis=0)
    w = w_ref[...]
    w_hi = w.astype(BF16)
    r = _dot(stack, w_hi) + _dot(stack, (w - w_hi.astype(F32)).astype(BF16))
    o_ref[...] = r[:rows] + r[rows:2 * rows] + r[2 * rows:] + b_ref[...]


def _mod_call(c_all, w_mod, b_mod):
    depth, d, d6 = w_mod.shape
    rows = c_all.shape[0]
    tn = 1024
    return pl.pallas_call(
        _mod_kernel,
        out_shape=jax.ShapeDtypeStruct((depth, rows, d6), F32),
        grid=(depth, d6 // tn),
        in_specs=[
            pl.BlockSpec((rows, d), lambda l, j: (0, 0)),
            pl.BlockSpec((None, d, tn), lambda l, j: (l, 0, j)),
            pl.BlockSpec((None, 1, tn), lambda l, j: (l, 0, j)),
        ],
        out_specs=pl.BlockSpec((None, rows, tn), lambda l, j: (l, 0, j)),
        compiler_params=pltpu.CompilerParams(vmem_limit_bytes=VMEM_LIMIT),
        name="mod",
    )(c_all, w_mod, b_mod.reshape(depth, 1, d6))


def _rope(x, cs):
    return (x * cs[:, :LANES]
            + pltpu.roll(x, 8, axis=1) * cs[:, LANES:2 * LANES]
            + pltpu.roll(x, LANES - 8, axis=1) * cs[:, 2 * LANES:])


def _x_specs(xs, tm):
    d = xs[0].shape[1]
    if len(xs) == 1:
        return [pl.BlockSpec((tm, d), lambda i, *_: (i, 0))]
    n_first = xs[0].shape[0] // tm
    assert xs[0].shape[0] % tm == 0 and xs[1].shape[0] % tm == 0
    return [pl.BlockSpec((tm, d), lambda i, *_: (jnp.minimum(i, n_first - 1), 0)),
            pl.BlockSpec((tm, d), lambda i, *_: (jnp.maximum(i - n_first, 0), 0))]


def _load_x(i, x_refs, n_first):
    if len(x_refs) == 1:
        return x_refs[0][...]
    return jnp.where(i < n_first, x_refs[0][...], x_refs[1][...])


def _chunk_rows(seq_ref, mod_ref, i, nch, col, d):
    return [mod_ref[pl.ds(seq_ref[i * nch + c], 1), col * d:(col + 1) * d] for c in range(nch)]


def _per_chunk(x, fn):
    nch = x.shape[0] // CHUNK
    return jnp.concatenate([fn(c, x[c * CHUNK:(c + 1) * CHUNK]) for c in range(nch)], axis=0)


def _inproj_kernel(seq_ref, *refs, n_x, n_first):
    x_refs, (mod_ref, nw_ref, cs_ref, rope_e_ref, rope_c_ref, wqkv_ref, wxb_ref, wz_ref, wba_ref, wga_ref, wgb_ref,
             q_ref, k_ref, v_ref, xb_ref, z_ref, ba_ref, ga_ref, gb_ref) = refs[:n_x], refs[n_x:]
    i = pl.program_id(0)
    x = _load_x(i, x_refs, n_first)
    tm, d = x.shape
    nch = tm // CHUNK
    xn = x * lax.rsqrt(jnp.mean(x * x, axis=-1, keepdims=True) + EPS) * nw_ref[...]
    shift = _chunk_rows(seq_ref, mod_ref, i, nch, 0, d)
    scale = _chunk_rows(seq_ref, mod_ref, i, nch, 1, d)
    h = _per_chunk(xn, lambda c, r: r * (1.0 + scale[c]) + shift[c]).astype(BF16)
    cs = _dot(_split3(cs_ref[...]), rope_e_ref[...]) + rope_c_ref[...]
    qkv = _dot(h, wqkv_ref[...])
    nq = N_HEADS_A * HD_A
    for g in range(nq // LANES):
        q_ref[:, g * LANES:(g + 1) * LANES] = _rope(qkv[:, g * LANES:(g + 1) * LANES], cs).astype(BF16)
    k_ref[...] = _rope(qkv[:, nq:nq + LANES], cs)
    v_ref[...] = qkv[:, nq + LANES:]
    xb_ref[...] = _dot(h, wxb_ref[...])
    z_ref[...] = _dot(h, wz_ref[...]).astype(BF16)
    ba_ref[...] = _dot(h, wba_ref[...])
    ga_ref[...] = _dot(h, wga_ref[...]).astype(BF16)
    gb_ref[...] = _dot(h, wgb_ref[...]).astype(BF16)


def _inproj_call(seq, xs, mod, norm_w, cs, rope_e, rope_c, wqkv, wxb, wz, wba, wga, wgb, tm):
    t = sum(x.shape[0] for x in xs)
    d = xs[0].shape[1]
    widths = [N_HEADS_A * HD_A, LANES, LANES, wxb.shape[1], wz.shape[1], LANES, d, d]
    dtypes = [BF16, F32, F32, F32, BF16, F32, BF16, BF16]
    tok = lambda w: pl.BlockSpec((tm, w), lambda i, s: (i, 0))
    return pl.pallas_call(
        functools.partial(_inproj_kernel, n_x=len(xs), n_first=xs[0].shape[0] // tm),
        out_shape=[jax.ShapeDtypeStruct((t, w), dt) for w, dt in zip(widths, dtypes)],
        grid_spec=pltpu.PrefetchScalarGridSpec(
            num_scalar_prefetch=1,
            grid=(t // tm,),
            in_specs=_x_specs(xs, tm) + [
                _const_spec(mod.shape),
                _const_spec((1, d)),
                tok(cs.shape[1]), _const_spec(rope_e.shape), _const_spec(rope_c.shape),
                _const_spec(wqkv.shape), _const_spec(wxb.shape), _const_spec(wz.shape),
                _const_spec(wba.shape), _const_spec(wga.shape), _const_spec(wgb.shape),
            ],
            out_specs=[tok(w) for w in widths],
        ),
        compiler_params=pltpu.CompilerParams(vmem_limit_bytes=VMEM_LIMIT),
        name="inproj",
    )(seq, *xs, mod, norm_w, cs, rope_e, rope_c, wqkv, wxb, wz, wba, wga, wgb)


def _attn_kernel(nvalid_ref, sink_ref, q_ref, kp_ref, kt_ref, vp_ref, vt_ref, *rest, cb, own_prev):
    o_ref = rest[-1]
    i = pl.program_id(0)
    nk = (WIN_CHUNKS + 1) * CHUNK
    npre = WIN_CHUNKS * CHUNK
    if own_prev:
        window = lambda pre, new: [jnp.concatenate(
            [pre[c * npre:(c + 1) * npre], new[c * CHUNK:(c + 1) * CHUNK]], axis=0).astype(BF16) for c in range(cb)]
    else:
        def window(pre, new):
            cat = jnp.concatenate([pre[...], new[...]], axis=0).astype(BF16)
            return [cat[c * CHUNK:c * CHUNK + nk] for c in range(cb)]
    kwin = window(kp_ref, kt_ref)
    vwin = window(vp_ref, vt_ref)
    key_chunk = lax.broadcasted_iota(jnp.int32, (CHUNK, nk), 1) // CHUNK
    group = N_HEADS_A // N_KV_A
    heads = range(N_HEADS_A)

    def scores(c):
        q = q_ref[c * CHUNK:(c + 1) * CHUNK, :]
        return [_dot_nt(q[:, h * HD_A:(h + 1) * HD_A], kwin[c][:, (h // group) * HD_A:(h // group + 1) * HD_A])
                for h in heads]

    s_next = scores(0)
    for c in range(cb):
        s_cur = s_next
        if c + 1 < cb:
            s_next = scores(c + 1)
        valid = key_chunk >= (WIN_CHUNKS - nvalid_ref[i * cb + c])
        p, den = [], []
        for h in heads:
            s = jnp.where(valid, s_cur[h] * (HD_A ** -0.5), -jnp.inf)
            sink = sink_ref[h]
            m = jnp.maximum(jnp.max(s, axis=-1, keepdims=True), sink)
            e = jnp.exp(s - m)
            p.append(e.astype(BF16))
            den.append(jnp.sum(e, axis=-1, keepdims=True) + jnp.exp(sink - m))
        o = [_dot(p[h], vwin[c][:, (h // group) * HD_A:(h // group + 1) * HD_A]) for h in heads]
        for h in heads:
            o_ref[c * CHUNK:(c + 1) * CHUNK, h * HD_A:(h + 1) * HD_A] = (o[h] / den[h]).astype(BF16)


def _attn_call(nvalid, sinks, q, k_prev, k_new, v_prev, v_new, cb, n_tiles, tile_off, prev_map, out_full=None,
               own_prev=False):
    kvw = N_KV_A * HD_A
    qw = N_HEADS_A * HD_A
    tile = lambda w: pl.BlockSpec((cb * CHUNK, w), lambda i, nv: (i + tile_off, 0))
    prev_rows = WIN_CHUNKS * CHUNK * (cb if own_prev else 1)
    prev = pl.BlockSpec((prev_rows, kvw), lambda i, nv: (prev_map(i), 0))
    args = [nvalid, sinks, q, k_prev, k_new, v_prev, v_new]
    in_specs = [pl.BlockSpec(memory_space=pltpu.SMEM), tile(qw), prev, tile(kvw), prev, tile(kvw)]
    aliases = {}
    if out_full is not None:
        aliases = {len(args): 0}
        args.append(out_full)
        in_specs.append(pl.BlockSpec(memory_space=pl.ANY))
    return pl.pallas_call(
        functools.partial(_attn_kernel, cb=cb, own_prev=own_prev),
        out_shape=jax.ShapeDtypeStruct((q.shape[0], qw), BF16),
        grid_spec=pltpu.PrefetchScalarGridSpec(
            num_scalar_prefetch=1,
            grid=(n_tiles,),
            in_specs=in_specs,
            out_specs=tile(qw),
        ),
        input_output_aliases=aliases,
        compiler_params=pltpu.CompilerParams(vmem_limit_bytes=VMEM_LIMIT),
        name="attn",
    )(*args)


def _l2norm(x):
    return x * lax.rsqrt(jnp.sum(x * x, axis=-1, keepdims=True) + EPS)


def _gdn_kernel(first_ref, last_ref, xb_ref, cinit_ref, sinit_ref, cw_ref, ba_ref, apar_ref, z_ref, nw_ref,
                *rest, cb, independent):
    ob_ref, sout_ref, pad_ref, state_ref = rest[-4:]
    step = pl.program_id(0)
    args = (first_ref, last_ref, xb_ref, cinit_ref, sinit_ref, cw_ref, ba_ref, apar_ref, z_ref, nw_ref,
            ob_ref, sout_ref, pad_ref, state_ref)
    if independent:
        _gdn_chunks(step * cb, 0, cb, *args, independent=True)
        return

    nc = GDN_CHUNKS_PER_ITER if cb % GDN_CHUNKS_PER_ITER == 0 else 1

    def body(it, carry):
        _gdn_chunks(step * cb + it * nc, pl.multiple_of(it * (nc * CHUNK), nc * CHUNK), nc, *args)
        return carry

    lax.fori_loop(0, cb // nc, body, 0)


def _gdn_chunks(i, row0, nc, first_ref, last_ref, xb_ref, cinit_ref, sinit_ref, cw_ref, ba_ref, apar_ref, z_ref,
                nw_ref, ob_ref, sout_ref, pad_ref, state_ref, independent=False):
    c = CHUNK
    n = nc * c
    rows = pl.ds(row0, n)
    cw = cw_ref[...]

    def conv_rows(m):
        out = pad_ref[5:5 + m, :] * cw[0:1, :]
        for j in range(1, CONV_W):
            out = out + pad_ref[5 + j:5 + j + m, :] * cw[j:j + 1, :]
        return out

    if independent:
        parts = []
        for k in range(nc):
            pad_ref[0:8, :] = cinit_ref[k]
            pad_ref[8:8 + c, :] = xb_ref[k * c:(k + 1) * c, :]
            parts.append(conv_rows(c))
        conv = jnp.concatenate(parts, axis=0)
    else:
        @pl.when(first_ref[i] == 1)
        def _():
            pad_ref[0:8, :] = cinit_ref[...]
            state_ref[...] = sinit_ref[...]

        pad_ref[8:8 + n, :] = xb_ref[rows, :]
        conv = conv_rows(n)
        pad_ref[0:8, :] = pad_ref[n:n + 8, :]
    qkv = _silu(conv)
    nqk = N_QK_B * DK_B

    ba = ba_ref[rows, :]
    apar = apar_ref[...]
    beta = _sigmoid(ba[:, 0:N_V_B])
    sp_in = ba + apar[1:2, :]
    softplus = jnp.maximum(sp_in, 0.0) + jnp.log(1.0 + jnp.exp(-jnp.abs(sp_in)))
    g_all = -jnp.exp(apar[0:1, :]) * softplus
    row = lax.broadcasted_iota(jnp.int32, (c, c), 0)
    col = lax.broadcasted_iota(jnp.int32, (c, c), 1)
    lower = (row >= col).astype(BF16)
    g1 = g_all.astype(BF16)
    r1 = g_all - g1.astype(F32)
    g2 = r1.astype(BF16)
    g3 = (r1 - g2.astype(F32)).astype(BF16)
    gsplit = jnp.concatenate([g1, g2, g3], axis=1)
    gc_all, gc_t = [], []
    for k in range(nc):
        gs = _dot(lower, gsplit[k * c:(k + 1) * c])
        gck = gs[:, :LANES] + gs[:, LANES:2 * LANES] + gs[:, 2 * LANES:]
        gc_all.append(gck)
        gc_t.append(gck.T)

    qn_all = [_l2norm(qkv[:, j * DK_B:(j + 1) * DK_B]) * (DK_B ** -0.5) for j in range(N_QK_B)]
    kn_all = [_l2norm(qkv[:, nqk + j * DK_B:nqk + (j + 1) * DK_B]) for j in range(N_QK_B)]
    pairs = [(k, j) for k in range(nc) for j in range(N_QK_B)]
    qn = {kj: qn_all[kj[1]][kj[0] * c:(kj[0] + 1) * c] for kj in pairs}
    kn = {kj: kn_all[kj[1]][kj[0] * c:(kj[0] + 1) * c] for kj in pairs}
    kt = {kj: kn[kj].T for kj in pairs}
    qkk = {kj: _dot(jnp.concatenate([qn[kj], kn[kj]], axis=0).astype(BF16), kt[kj].astype(BF16))
           for kj in pairs}

    rep = N_V_B // N_QK_B
    items = [(k, h) for k in range(nc) for h in range(N_V_B)]
    qk_of = lambda kh: (kh[0], kh[1] // rep)
    chunk_rows = lambda kh, x: x[kh[0] * c:(kh[0] + 1) * c]
    gcol = {kh: gc_all[kh[0]][:, N_V_B + kh[1]:N_V_B + kh[1] + 1] for kh in items}
    grow = {kh: gc_t[kh[0]][N_V_B + kh[1]:N_V_B + kh[1] + 1, :] for kh in items}
    bcol = {kh: chunk_rows(kh, beta)[:, kh[1]:kh[1] + 1] for kh in items}
    decay = {kh: jnp.exp(jnp.where(row >= col, gcol[kh] - grow[kh], -jnp.inf)) for kh in items}
    a = {kh: jnp.where(row > col, bcol[kh] * qkk[qk_of(kh)][c:] * decay[kh], 0.0) for kh in items}
    xm = {kh: -a[kh] for kh in items}
    p = {kh: _dot(a[kh].astype(BF16), a[kh].astype(BF16)) for kh in items}
    for _ in range(4):
        pb = {kh: p[kh].astype(BF16) for kh in items}
        r = {kh: _dot(jnp.concatenate([xm[kh].astype(BF16), pb[kh]], axis=0), pb[kh]) for kh in items}
        xm = {kh: xm[kh] + p[kh] + r[kh][:c] for kh in items}
        p = {kh: r[kh][c:] for kh in items}
    r = {kh: _dot(xm[kh].astype(BF16), p[kh].astype(BF16)) for kh in items}
    xm = {kh: xm[kh] + p[kh] + r[kh] for kh in items}
    egc = {kh: jnp.exp(gcol[kh]) for kh in items}
    rhs = {kh: jnp.concatenate(
        [chunk_rows(kh, qkv[:, 2 * nqk + kh[1] * DV_B:2 * nqk + (kh[1] + 1) * DV_B]) * bcol[kh],
         kn[qk_of(kh)] * (bcol[kh] * egc[kh])], axis=1) for kh in items}
    sol = {kh: rhs[kh] + _dot(xm[kh].astype(BF16), rhs[kh].astype(BF16)) for kh in items}
    wq = {kh: jnp.concatenate([sol[kh][:, DV_B:], qn[qk_of(kh)] * egc[kh]], axis=0).astype(BF16) for kh in items}
    a_qk = {kh: (qkk[qk_of(kh)][:c] * decay[kh]).astype(BF16) for kh in items}
    g_last = {kh: gcol[kh][c - 1:c, :] for kh in items}
    k_dec_t = {kh: (kt[qk_of(kh)] * jnp.exp(g_last[kh] - grow[kh])).astype(BF16) for kh in items}

    heads = range(N_V_B)
    nw = nw_ref[...]

    def store_out(k, h, o):
        start = row0 + k * c
        out_rows = pl.ds(start if isinstance(start, int) else pl.multiple_of(start, c), c)
        on = o * lax.rsqrt(jnp.mean(o * o, axis=-1, keepdims=True) + EPS) * nw
        zh = z_ref[out_rows, h * DV_B:(h + 1) * DV_B].astype(F32)
        ob_ref[out_rows, h * DV_B:(h + 1) * DV_B] = (on * _silu(zh)).astype(BF16)

    if independent:
        st = {kh: sinit_ref[kh[0], kh[1]] for kh in items}
        r = {kh: _dot(wq[kh], st[kh].astype(BF16)) for kh in items}
        v_new = {kh: (sol[kh][:, :DV_B] - r[kh][:c]).astype(BF16) for kh in items}
        for kh in items:
            sout_ref[kh[0], kh[1]] = st[kh] * jnp.exp(g_last[kh]) + _dot(k_dec_t[kh], v_new[kh])
        o = {kh: r[kh][c:] + _dot(a_qk[kh], v_new[kh]) for kh in items}
        for kh in items:
            store_out(kh[0], kh[1], o[kh])
        return

    state = [state_ref[h] for h in heads]
    for k in range(nc):
        r = [_dot(wq[k, h], state[h].astype(BF16)) for h in heads]
        v_new = [(sol[k, h][:, :DV_B] - r[h][:c]).astype(BF16) for h in heads]
        state = [state[h] * jnp.exp(g_last[k, h]) + _dot(k_dec_t[k, h], v_new[h]) for h in heads]
        o = [r[h][c:] + _dot(a_qk[k, h], v_new[h]) for h in heads]
        out_rows = pl.ds(pl.multiple_of(row0 + k * c, c), c)
        for h in heads:
            on = o[h] * lax.rsqrt(jnp.mean(o[h] * o[h], axis=-1, keepdims=True) + EPS) * nw
            zh = z_ref[out_rows, h * DV_B:(h + 1) * DV_B].astype(F32)
            ob_ref[out_rows, h * DV_B:(h + 1) * DV_B] = (on * _silu(zh)).astype(BF16)
    for h in heads:
        state_ref[h] = state[h]

    @pl.when(last_ref[i + nc - 1] == 1)
    def _():
        sout_ref[...] = state_ref[...]


def _gdn_call(first, last, xb, conv_init, ssm_init, conv_w, ba, apar, z, norm_o, cb, n_steps, tile_off,
              steps_per_seq, out_full=None, ssm_off=0, independent=False):
    t, cd = xb.shape
    nseq = conv_init.shape[0]
    vw = N_V_B * DV_B
    tile = lambda w: pl.BlockSpec((cb * CHUNK, w), lambda i, f, l: (i + tile_off, 0))
    const = lambda shape: pl.BlockSpec(shape, lambda i, f, l: (0,) * len(shape))
    if independent:
        assert steps_per_seq == 1 and nseq % cb == 0 and ssm_off % cb == 0
        seqs = cb
        state = pl.BlockSpec((seqs, N_V_B, DK_B, DV_B), lambda i, f, l: (i, 0, 0, 0))
        state_in = pl.BlockSpec((seqs, N_V_B, DK_B, DV_B), lambda i, f, l: (i + ssm_off // cb, 0, 0, 0))
        conv_in = pl.BlockSpec((seqs, 8, cd), lambda i, f, l: (i, 0, 0))
    else:
        state = pl.BlockSpec((None, N_V_B, DK_B, DV_B), lambda i, f, l: (i // steps_per_seq, 0, 0, 0))
        state_in = pl.BlockSpec((None, N_V_B, DK_B, DV_B), lambda i, f, l: (i // steps_per_seq + ssm_off, 0, 0, 0))
        conv_in = pl.BlockSpec((None, 8, cd), lambda i, f, l: (i // steps_per_seq, 0, 0))
    args = [first, last, xb, conv_init, ssm_init, conv_w, ba, apar, z, norm_o]
    in_specs = [
        tile(cd),
        conv_in,
        state_in,
        const((CONV_W, cd)),
        tile(LANES),
        const((2, LANES)),
        tile(vw),
        const((1, DV_B)),
    ]
    aliases = {}
    if out_full is not None:
        aliases = {len(args): 0}
        args.append(out_full)
        in_specs.append(pl.BlockSpec(memory_space=pl.ANY))
    return pl.pallas_call(
        functools.partial(_gdn_kernel, cb=cb, independent=independent),
        out_shape=[jax.ShapeDtypeStruct((t, vw), BF16),
                   jax.ShapeDtypeStruct((nseq, N_V_B, DK_B, DV_B), F32)],
        grid_spec=pltpu.PrefetchScalarGridSpec(
            num_scalar_prefetch=2,
            grid=(n_steps,),
            in_specs=in_specs,
            out_specs=[tile(vw), state],
            scratch_shapes=[pltpu.VMEM((GDN_CHUNKS_PER_ITER * CHUNK + 8, cd), F32),
                            pltpu.VMEM((N_V_B, DK_B, DV_B), F32)],
        ),
        input_output_aliases=aliases,
        compiler_params=pltpu.CompilerParams(dimension_semantics=("arbitrary",),
                                             vmem_limit_bytes=VMEM_LIMIT),
        name="gdn",
    )(*args)


def _outproj_kernel(seq_ref, *refs, n_x, n_first):
    x_refs, (oa_ref, ob_ref, ga_ref, gb_ref, mod_ref, woa_ref, wob_ref, wout_ref, nw_ref,
             wr_ref, br_ref, x1_ref, h2_ref, wexp_ref) = refs[:n_x], refs[n_x:]
    i = pl.program_id(0)
    x = _load_x(i, x_refs, n_first)
    tm, d = x.shape
    nch = tm // CHUNK
    ya = _dot(oa_ref[...], woa_ref[...])
    yb = _dot(ob_ref[...], wob_ref[...])
    merged = _sigmoid(ga_ref[...].astype(F32)) * ya + _sigmoid(gb_ref[...].astype(F32)) * yb
    mix = _dot(merged.astype(BF16), wout_ref[...])
    g1 = _chunk_rows(seq_ref, mod_ref, i, nch, 2, d)
    x1 = x + _per_chunk(mix, lambda c, r: g1[c] * r)
    x1_ref[...] = x1
    xn = x1 * lax.rsqrt(jnp.mean(x1 * x1, axis=-1, keepdims=True) + EPS) * nw_ref[...]
    shift = _chunk_rows(seq_ref, mod_ref, i, nch, 3, d)
    scale = _chunk_rows(seq_ref, mod_ref, i, nch, 4, d)
    h2 = _per_chunk(xn, lambda c, r: r * (1.0 + scale[c]) + shift[c])
    h2_hi = h2.astype(BF16)
    h2_ref[...] = h2_hi

    h2_lo = (h2 - h2_hi.astype(F32)).astype(BF16)
    wr = wr_ref[...]
    hi = _dot(h2_hi, wr)
    logits = hi[:, :LANES] + hi[:, LANES:] + _dot(h2_lo, wr[:, :LANES]) + br_ref[...]
    lane = lax.broadcasted_iota(jnp.int32, logits.shape, 1)
    neg = -jnp.inf
    lg = jnp.where(lane < N_GROUPS, logits, neg)
    mx = jnp.max(lg, axis=-1, keepdims=True)
    gi = jnp.min(jnp.where(lg == mx, lane, LANES), axis=-1, keepdims=True)
    p_group = 1.0 / jnp.sum(jnp.exp(lg - mx), axis=-1, keepdims=True)
    lo = N_GROUPS + EXPERTS_PER_GROUP * gi
    le = jnp.where((lane >= lo) & (lane < lo + EXPERTS_PER_GROUP), logits, neg)
    v1 = jnp.max(le, axis=-1, keepdims=True)
    i1 = jnp.min(jnp.where(le == v1, lane, LANES), axis=-1, keepdims=True)
    le2 = jnp.where(lane == i1, neg, le)
    v2 = jnp.max(le2, axis=-1, keepdims=True)
    i2 = jnp.min(jnp.where(le2 == v2, lane, LANES), axis=-1, keepdims=True)
    e2 = jnp.exp(v2 - v1)
    w1 = p_group / (1.0 + e2)
    w2 = p_group * e2 / (1.0 + e2)
    wexp_ref[...] = jnp.where(lane == i1, w1, 0.0) + jnp.where(lane == i2, w2, 0.0)


def _outproj_call(seq, xs, oa, ob, ga, gb, mod, woa, wob, wout, norm_w, wr, br, tm):
    t = sum(x.shape[0] for x in xs)
    d = xs[0].shape[1]
    tok = lambda w: pl.BlockSpec((tm, w), lambda i, s: (i, 0))
    return pl.pallas_call(
        functools.partial(_outproj_kernel, n_x=len(xs), n_first=xs[0].shape[0] // tm),
        out_shape=[jax.ShapeDtypeStruct((t, d), F32), jax.ShapeDtypeStruct((t, d), BF16),
                   jax.ShapeDtypeStruct((t, LANES), F32)],
        grid_spec=pltpu.PrefetchScalarGridSpec(
            num_scalar_prefetch=1,
            grid=(t // tm,),
            in_specs=_x_specs(xs, tm) + [
                tok(oa.shape[1]), tok(ob.shape[1]), tok(d), tok(d),
                _const_spec(mod.shape),
                _const_spec(woa.shape), _const_spec(wob.shape), _const_spec(wout.shape),
                _const_spec((1, d)), _const_spec(wr.shape), _const_spec((1, LANES)),
            ],
            out_specs=[tok(d), tok(d), tok(LANES)],
        ),
        compiler_params=pltpu.CompilerParams(vmem_limit_bytes=VMEM_LIMIT),
        name="outproj",
    )(seq, *xs, oa, ob, ga, gb, mod, woa, wob, wout, norm_w, wr, br)


def _split3(x):
    x1 = x.astype(BF16)
    r1 = x - x1.astype(F32)
    x2 = r1.astype(BF16)
    return jnp.concatenate([x1, x2, (r1 - x2.astype(F32)).astype(BF16)], axis=1)


def _moe_kernel(seq_ref, h_ref, wexp_ref, x1_ref, mod_ref, wg_ref, wu_ref, wd_ref, *rest, n_first):
    xs_ref, ws_ref, ys_ref, pmt_ref, meta_ref = rest[-5:]
    outs = rest[:-5]
    i = pl.program_id(0)
    g = pl.program_id(1)
    tm, d = x1_ref.shape
    npos = xs_ref.shape[0]
    rb = MOE_ROW_BLOCK
    epg = wg_ref.shape[0]

    @pl.when(g == 0)
    def _():
        wexp = wexp_ref[...]
        lane = lax.broadcasted_iota(jnp.int32, (tm, LANES), 1)
        sel = jnp.where(wexp > 0.0, 1.0, 0.0).astype(BF16)
        wl = lax.broadcasted_iota(jnp.int32, (LANES, LANES), 0) - N_GROUPS
        gl = lax.broadcasted_iota(jnp.int32, (LANES, LANES), 1)
        in_group = (wl >= gl * epg) & (wl < (gl + 1) * epg) & (gl < N_GROUPS)
        onehot = jnp.where(_dot(sel, jnp.where(in_group, 1.0, 0.0).astype(BF16)) > 0.0, 1.0, 0.0)
        before = (lax.broadcasted_iota(jnp.int32, (LANES, LANES), 0)
                  > lax.broadcasted_iota(jnp.int32, (LANES, LANES), 1))
        before = jnp.where(before, 1.0, 0.0).astype(BF16)
        onehot_b = onehot.astype(BF16)
        cnt = jnp.zeros((1, LANES), F32)
        parts = []
        for blk in range(tm // LANES):
            rows_b = slice(blk * LANES, (blk + 1) * LANES)
            parts.append(_dot(before, onehot_b[rows_b]) + cnt)
            cnt = cnt + jnp.sum(onehot[rows_b], axis=0, keepdims=True)
        rank = jnp.concatenate(parts, axis=0)
        nblk = jnp.floor((cnt + (rb - 1)) * (1.0 / rb))
        base_row = jnp.zeros((1, LANES), F32)
        start = jnp.zeros((), F32)
        for grp in range(N_GROUPS):
            nb = nblk[0, grp]
            meta_ref[grp] = start.astype(jnp.int32)
            meta_ref[N_GROUPS + grp] = nb.astype(jnp.int32)
            base_row = base_row + jnp.where(lane[0:1, :] == grp, start, 0.0)
            start = start + nb * rb
        pos_col = jnp.sum(onehot * (rank + base_row), axis=-1, keepdims=True)
        pmt_ref[...] = jnp.where(pos_col.astype(jnp.int32) == lax.broadcasted_iota(jnp.int32, (tm, npos), 1),
                                 1.0, 0.0).astype(BF16)
        pos_row = jnp.transpose(jnp.broadcast_to(pos_col, (tm, LANES)))[0:1, :].astype(jnp.int32)
        q = LANES // 4
        hw = jnp.concatenate([h_ref[...], _split3(wexp[:, :q]), jnp.zeros((tm, q), BF16)], axis=1)
        used = start.astype(jnp.int32)
        pblk = MOE_PERMUTE_BLOCK
        for pb in range(npos // pblk):
            @pl.when(pb * pblk < used)
            def _():
                slot = lax.broadcasted_iota(jnp.int32, (pblk, tm), 0) + pb * pblk
                pm = jnp.where(slot == pos_row, 1.0, 0.0).astype(BF16)
                r = _dot(pm, hw)
                xs_ref[pb * pblk:(pb + 1) * pblk, :] = r[:, :d].astype(BF16)
                ws = r[:, d:d + q] + r[:, d + q:d + 2 * q] + r[:, d + 2 * q:d + 3 * q]
                ws_ref[pb * pblk:(pb + 1) * pblk, :] = jnp.concatenate(
                    [ws, jnp.zeros((pblk, LANES - q), F32)], axis=1)
        ys_ref[...] = jnp.zeros_like(ys_ref)

    start = meta_ref[g]

    def run_experts(r0, nrows):
        rows = pl.ds(pl.multiple_of(r0, rb), nrows)
        x = xs_ref[rows, :]
        w = ws_ref[rows, :]
        lane = lax.broadcasted_iota(jnp.int32, (nrows, LANES), 1)
        experts = range(epg)
        hg = [_dot(x, wg_ref[j]) for j in experts]
        hu = [_dot(x, wu_ref[j]) for j in experts]
        we = [jnp.sum(jnp.where(lane == N_GROUPS + g * epg + j, w, 0.0), axis=-1, keepdims=True) for j in experts]
        act = [(_silu(hg[j]) * hu[j] * we[j]).astype(BF16) for j in experts]
        out = _dot(act[0], wd_ref[0])
        for j in experts[1:]:
            out = out + _dot(act[j], wd_ref[j])
        ys_ref[rows, :] = out.astype(BF16)

    nblocks = meta_ref[N_GROUPS + g]
    npairs = lax.shift_right_logical(nblocks, 1)

    def pair(b, carry):
        run_experts(start + b * (2 * rb), 2 * rb)
        return carry

    lax.fori_loop(0, npairs, pair, 0)

    @pl.when(nblocks - 2 * npairs == 1)
    def _():
        run_experts(start + npairs * (2 * rb), rb)

    @pl.when(g == pl.num_programs(1) - 1)
    def _():
        nch = tm // CHUNK
        moe = _dot(pmt_ref[...], ys_ref[...])
        g2 = _chunk_rows(seq_ref, mod_ref, i, nch, 5, d)
        x2 = x1_ref[...] + _per_chunk(moe, lambda c, r: g2[c] * r)
        if len(outs) == 1:
            outs[0][...] = x2
        else:
            nw_ref, y_prompt_ref, y_sample_ref = outs
            y = x2 * lax.rsqrt(jnp.mean(x2 * x2, axis=-1, keepdims=True) + EPS) * nw_ref[...]

            @pl.when(i < n_first)
            def _():
                y_prompt_ref[...] = y

            @pl.when(i >= n_first)
            def _():
                y_sample_ref[...] = y


def _moe_call(seq, h2, wexp, x1, mod, wg, wu, wd, layer, tm, final=None):
    t, d = x1.shape
    de = wg.shape[2]
    assert tm % LANES == 0
    w_off = layer * N_GROUPS
    epb = EXPERTS_PER_GROUP
    npos = -(-(tm + N_GROUPS * MOE_ROW_BLOCK) // MOE_PERMUTE_BLOCK) * MOE_PERMUTE_BLOCK
    tok =lambda w: pl.BlockSpec((tm, w), lambda i, e, s: (i, 0))
    in_specs = [
        tok(d), tok(LANES), tok(d),
        _const_spec(mod.shape),
        pl.BlockSpec((epb, d, de), lambda i, e, s: (w_off + e, 0, 0)),
        pl.BlockSpec((epb, d, de), lambda i, e, s: (w_off + e, 0, 0)),
        pl.BlockSpec((epb, de, d), lambda i, e, s: (w_off + e, 0, 0)),
    ]
    args = [seq, h2, wexp, x1, mod, wg, wu, wd]
    if final is None:
        n_first = 0
        out_shape = jax.ShapeDtypeStruct((t, d), F32)
        out_specs = tok(d)
    else:
        norm_w, tp = final
        assert tp % tm == 0 and (t - tp) % tm == 0
        n_first = tp // tm
        in_specs.append(_const_spec((1, d)))
        args.append(norm_w)
        out_shape = [jax.ShapeDtypeStruct((tp, d), F32), jax.ShapeDtypeStruct((t - tp, d), F32)]
        out_specs = [pl.BlockSpec((tm, d), lambda i, e, s: (jnp.minimum(i, n_first - 1), 0)),
                     pl.BlockSpec((tm, d), lambda i, e, s: (jnp.maximum(i - n_first, 0), 0))]
    return pl.pallas_call(
        functools.partial(_moe_kernel, n_first=n_first),
        out_shape=out_shape,
        grid_spec=pltpu.PrefetchScalarGridSpec(
            num_scalar_prefetch=1,
            grid=(t // tm, N_GROUPS),
            in_specs=in_specs,
            out_specs=out_specs,
            scratch_shapes=[pltpu.VMEM((npos, d), BF16), pltpu.VMEM((npos, LANES), F32),
                            pltpu.VMEM((npos, d), BF16), pltpu.VMEM((tm, npos), BF16),
                            pltpu.SMEM((2 * N_GROUPS,), jnp.int32)],
        ),
        compiler_params=pltpu.CompilerParams(dimension_semantics=("arbitrary", "arbitrary"),
                                             vmem_limit_bytes=VMEM_LIMIT),
        name="moe",
    )(*args)


def _rope_table(pos):
    half = ROPE_DIM // 2
    inv = ROPE_THETA ** (-jnp.arange(half, dtype=F32) / half)
    ang = pos.astype(F32)[:, None] * inv[None, :]
    return jnp.concatenate([jnp.cos(ang), jnp.sin(ang)], axis=1)


def _rope_expansion():
    half = ROPE_DIM // 2
    e = np.zeros((2 * half, 3 * LANES), np.float32)
    c = np.zeros((1, 3 * LANES), np.float32)
    for lane in range(LANES):
        dim = lane % HD_A
        if dim < half:
            e[dim, lane] = 1.0
            e[half + dim, 2 * LANES + lane] = -1.0
        elif dim < ROPE_DIM:
            e[dim - half, lane] = 1.0
            e[dim, LANES + lane] = 1.0
        else:
            c[0, lane] = 1.0
    return jnp.asarray(np.tile(e, (3, 1)), BF16), jnp.asarray(c)


def kernel(x_prompt, x_sample, cache_k_a, cache_v_a, state_conv_b, state_ssm_b, c_prompt, c_sample, w_mod, b_mod, norm_mix, w_in, sinks_a, w_o_a, conv_b, a_log_b, dt_bias_b, norm_o_b, w_o_b, w_out, norm_ffn, router_g, router_g_b, router_e, router_e_b, w_gate_e, w_up_e, w_down_e, norm_final):
    bp, seq, d = x_prompt.shape
    bs, dseq, _ = x_sample.shape
    depth = w_mod.shape[0]
    assert seq % CHUNK == 0 and dseq == CHUNK and d % LANES == 0
    assert cache_k_a.shape[2] == WIN_CHUNKS * CHUNK
    npc = seq // CHUNK
    tp, ts = bp * seq, bs * dseq
    t = tp + ts
    n_chunks = t // CHUNK
    nseq = bp + bs
    kvw = N_KV_A * HD_A
    cd = 2 * N_QK_B * DK_B + N_V_B * DV_B
    vw = N_V_B * DV_B
    nq = N_HEADS_A * HD_A

    seq_np = np.concatenate([np.repeat(np.arange(bp), npc), bp + np.arange(bs)]).astype(np.int32)
    local_np = np.concatenate([np.tile(np.arange(npc), bp), np.zeros(bs, np.int64)])
    first_np = (local_np == 0).astype(np.int32)
    last_np = np.concatenate([np.tile(np.arange(npc) == npc - 1, bp), np.ones(bs, bool)]).astype(np.int32)
    nvalid_np = np.concatenate([np.minimum(np.tile(np.arange(npc), bp), WIN_CHUNKS),
                                np.full(bs, WIN_CHUNKS)]).astype(np.int32)
    seq_i = jnp.asarray(seq_np)
    first_p, last_p = jnp.asarray(first_np[:bp * npc]), jnp.asarray(last_np[:bp * npc])
    nvalid_p, nvalid_s = jnp.asarray(nvalid_np[:bp * npc]), jnp.asarray(nvalid_np[bp * npc:])
    cb = 16 if npc % 16 == 0 else (4 if npc % 4 == 0 else 2)
    assert npc % cb == 0
    cbg = GDN_CHUNKS_PER_STEP if npc % GDN_CHUNKS_PER_STEP == 0 else cb
    cbs = 4 if bs % 4 == 0 and tp % (4 * CHUNK) == 0 else 1
    cgs = GDN_CHUNKS_PER_ITER if bs % GDN_CHUNKS_PER_ITER == 0 and npc % GDN_CHUNKS_PER_ITER == 0 else 1

    pos = jnp.concatenate([jnp.tile(jnp.arange(seq), bp), jnp.tile(PAST_LEN + jnp.arange(dseq), bs)])
    cs = _rope_table(pos)
    rope_e, rope_c = _rope_expansion()

    xs = [x_prompt.reshape(tp, d), x_sample.reshape(ts, d)]
    c_all = jnp.concatenate([c_prompt, c_sample], axis=0)
    c_rows = -(-nseq // 16) * 16
    c_all = jnp.pad(c_all, ((0, c_rows - nseq), (0, 0)))
    mod = _mod_call(c_all, w_mod, b_mod)

    tm = _pick_tile(np.gcd(tp, ts), 512)
    tm_out = _pick_tile(np.gcd(tp, ts), 512)
    tm_moe = _pick_tile(np.gcd(tp, ts), 1024)
    sizes = np.cumsum([0, nq, kvw, kvw, cd, vw, N_V_B, N_V_B, d, d])
    wg_all = w_gate_e.astype(BF16).reshape(depth * N_EXPERTS, d, D_EXPERT)
    wu_all = w_up_e.astype(BF16).reshape(depth * N_EXPERTS, d, D_EXPERT)
    wd_all = w_down_e.astype(BF16).reshape(depth * N_EXPERTS, D_EXPERT, d)
    ssm_in = state_ssm_b.reshape(depth * bs, N_V_B, DK_B, DV_B)
    outs = {k: [] for k in ("k_p", "v_p", "c_p", "s_p", "k_s", "v_s", "c_s", "s_s")}
    for l in range(depth):
        wl = w_in[l]
        seg = lambda a, b: wl[:, sizes[a]:sizes[b]].astype(BF16)
        wqkv, wxb, wz = seg(0, 3), seg(3, 4), seg(4, 5)
        wba = jnp.pad(seg(5, 7), ((0, 0), (0, LANES - 2 * N_V_B)))
        wga, wgb = seg(7, 8), seg(8, 9)
        q, k, v, xb, z, ba, ga, gb = _inproj_call(
            seq_i, xs, mod[l], norm_mix[l].reshape(1, d), cs, rope_e, rope_c, wqkv, wxb, wz, wba, wga, wgb, tm)

        oa = _attn_call(nvalid_p, sinks_a[l], q, k, k, v, v, cb, tp // (cb * CHUNK), 0,
                        lambda i: jnp.maximum(i * (cb // WIN_CHUNKS) - 1, 0))
        oa = _attn_call(nvalid_s, sinks_a[l], q, cache_k_a[l].reshape(bs * WIN_CHUNKS * CHUNK, kvw), k,
                        cache_v_a[l].reshape(bs * WIN_CHUNKS * CHUNK, kvw), v, cbs, bs // cbs,
                        tp // (cbs * CHUNK), lambda i: i, out_full=oa, own_prev=True)

        apar = jnp.zeros((2, LANES), F32)
        apar = apar.at[0, N_V_B:2 * N_V_B].set(a_log_b[l]).at[1, N_V_B:2 * N_V_B].set(dt_bias_b[l])
        nw_o = norm_o_b[l].reshape(1, DV_B)
        ob, ssm_p = _gdn_call(first_p, last_p, xb, jnp.zeros((bp, 8, cd), F32),
                              jnp.zeros((bp, N_V_B, DK_B, DV_B), F32), conv_b[l], ba, apar, z, nw_o,
                              cbg, bp * npc // cbg, 0, npc // cbg)
        conv_init = jnp.pad(state_conv_b[l], ((0, 0), (8 - (CONV_W - 1), 0), (0, 0)))
        ones_s = jnp.ones((bs,), jnp.int32)
        ob, ssm_s = _gdn_call(ones_s, ones_s, xb, conv_init, ssm_in, conv_b[l], ba, apar, z, nw_o,
                              cgs, bs // cgs, tp // (cgs * CHUNK), 1, out_full=ob, ssm_off=l * bs,
                              independent=cgs > 1)

        wr = jnp.concatenate([router_g[l], jnp.transpose(router_e[l], (1, 0, 2)).reshape(d, N_EXPERTS)], axis=1)
        wr = jnp.pad(wr, ((0, 0), (0, LANES - wr.shape[1])))
        wr_hi = wr.astype(BF16)
        wr = jnp.concatenate([wr_hi, (wr - wr_hi.astype(F32)).astype(BF16)], axis=1)
        br = jnp.concatenate([router_g_b[l], router_e_b[l].reshape(-1)])
        br = jnp.pad(br, (0, LANES - br.shape[0])).reshape(1, LANES)
        x1, h2, wexp = _outproj_call(seq_i, xs, oa, ob, ga, gb, mod[l], w_o_a[l].astype(BF16),
                                     w_o_b[l].astype(BF16), w_out[l].astype(BF16), norm_ffn[l].reshape(1, d),
                                     wr, br, tm_out)
        final =(norm_final.reshape(1, d), tp) if l == depth - 1 else None
        x_next = _moe_call(seq_i, h2, wexp, x1, mod[l], wg_all, wu_all, wd_all, l, tm_moe, final)
        xs = [x_next] if final is None else x_next

        keep = min(WIN_CHUNKS * CHUNK, seq)
        ends = [(b + 1) * seq for b in range(bp)]
        tail = lambda a, n: jnp.stack([a[e - n:e] for e in ends])
        outs["k_p"].append(tail(k, keep).reshape(bp, keep, N_KV_A, HD_A))
        outs["v_p"].append(tail(v, keep).reshape(bp, keep, N_KV_A, HD_A))
        outs["k_s"].append(k[tp:].reshape(bs, dseq, N_KV_A, HD_A))
        outs["v_s"].append(v[tp:].reshape(bs, dseq, N_KV_A, HD_A))
        outs["c_p"].append(tail(xb, CONV_W - 1))
        outs["c_s"].append(jnp.stack([xb[tp + (b + 1) * dseq - (CONV_W - 1):tp + (b + 1) * dseq]
                                      for b in range(bs)]))
        outs["s_p"].append(ssm_p)
        outs["s_s"].append(ssm_s)

    st = lambda key: jnp.stack(outs[key])
    return (xs[0].reshape(bp, seq, d), xs[1].reshape(bs, dseq, d),
            st("k_p"), st("v_p"), st("c_p"), st("s_p"), st("k_s"), st("v_s"), st("c_s"), st("s_s"))
```
